```python
import jax, jax.numpy as jnp
from jax import lax
import numpy as np

D_MODEL = 1024
BATCH = 8
SEQ = 2048
DEPTH = 2
DEC_BATCH = 128
DEC_SEQ = 8
PAST_LEN = 16384
PAGE_SIZE = 128

N_META = 16
H_RET = 4
DK_RET = 128
DV_RET = 128
H_HGRN = 4
DK_HGRN = 128
DV_HGRN = 128
D_RET = H_RET * DV_RET
D_HGRN = H_HGRN * DV_HGRN
D_IN_AB = 2 * H_RET * DK_RET + 2 * D_RET + 2 * H_HGRN * DK_HGRN + 2 * D_HGRN
D_MIX_AB = D_RET + D_HGRN
RET_CHUNK = 128
HGRN_CHUNK = 16
ROPE_BASE = 10000.0
D_RNN = 1024
N_LRU_BLOCKS = 8
LRU_BLOCK = D_RNN // N_LRU_BLOCKS
CONV_W_LRU = 4
LRU_C = 8.0
D_FF = 2816
CONV_W_FFN = 3
LN_EPS = 1e-5
DEEPNORM_ALPHA = (2.0 * DEPTH) ** 0.25
DEEPNORM_BETA = (8.0 * DEPTH) ** -0.25

kernel_name = 'hybrid_retention_hgrn2_rglru_convffn_step'


def _layernorm(x, g, b):
    xf = x.astype(jnp.float32)
    mu = xf.mean(-1, keepdims=True)
    var = jnp.mean(jnp.square(xf - mu), -1, keepdims=True)
    return ((xf - mu) * lax.rsqrt(var + LN_EPS) * g.astype(jnp.float32) + b.astype(jnp.float32)).astype(x.dtype)


def _rope(x, pos):
    half = x.shape[-1] // 2
    inv = ROPE_BASE ** (-jnp.arange(half, dtype=jnp.float32) / half)
    ang = pos.astype(jnp.float32)[:, None] * inv[None, :]
    cos, sin = jnp.cos(ang), jnp.sin(ang)
    x1, x2 = x[..., :half], x[..., half:]
    return jnp.concatenate([x1 * cos - x2 * sin, x1 * sin + x2 * cos], axis=-1)


def _chunk_size(length, chunk):
    return chunk if length % chunk == 0 else length


def _to_chunks(t, chunk):
    b, h, length, d = t.shape
    return jnp.moveaxis(t.reshape(b, h, length // chunk, chunk, d), 2, 0)


def _from_chunks(o):
    nc, b, h, c, d = o.shape
    return jnp.moveaxis(o, 0, 2).reshape(b, h, nc * c, d)


def _retention_chunked(q, k, v, log_gamma, s0, chunk):
    idx = jnp.arange(chunk, dtype=jnp.float32)
    rel = idx[:, None] - idx[None, :]
    decay_intra = jnp.where(rel >= 0, jnp.exp(log_gamma[:, None, None] * jnp.maximum(rel, 0.0)), 0.0)
    w_q = jnp.exp(log_gamma[:, None] * (idx + 1.0))[..., None]
    w_k = jnp.exp(log_gamma[:, None] * (chunk - 1.0 - idx))[..., None]
    chunk_decay = jnp.exp(log_gamma * chunk)[:, None, None]

    def step(S, inp):
        qc, kc, vc = inp
        attn = jnp.einsum('bhid,bhjd->bhij', qc, kc) * decay_intra
        o = jnp.einsum('bhij,bhjv->bhiv', attn, vc) + jnp.einsum('bhid,bhdv->bhiv', qc * w_q, S)
        S = chunk_decay * S + jnp.einsum('bhjd,bhjv->bhdv', kc * w_k, vc)
        return S, o

    S, o = lax.scan(step, s0, (_to_chunks(q, chunk), _to_chunks(k, chunk), _to_chunks(v, chunk)))
    return _from_chunks(o), S


def _gla_chunked(q, k, v, log_f, s0, chunk):
    mask = jnp.tril(jnp.ones((chunk, chunk), dtype=bool))

    def step(S, inp):
        qc, kc, vc, gc = inp
        bcum = jnp.cumsum(gc, axis=2)
        diff = bcum[:, :, :, None, :] - bcum[:, :, None, :, :]
        decay = jnp.exp(jnp.where(mask[:, :, None], diff, -jnp.inf))
        attn = jnp.einsum('bhid,bhjd,bhijd->bhij', qc, kc, decay)
        o = jnp.einsum('bhij,bhjv->bhiv', attn, vc) + jnp.einsum('bhid,bhdv->bhiv', qc * jnp.exp(bcum), S)
        last = bcum[:, :, -1:, :]
        S = jnp.exp(last[:, :, 0, :])[..., None] * S + jnp.einsum('bhjd,bhjv->bhdv', kc * jnp.exp(last - bcum), vc)
        return S, o

    S, o = lax.scan(step, s0, tuple(_to_chunks(t, chunk) for t in (q, k, v, log_f)))
    return _from_chunks(o), S


def _segmented(fn, arrays, s0, segments, chunk):
    outs = []
    s = s0
    start = 0
    for length in segments:
        sl = tuple(a[:, :, start:start + length] for a in arrays)
        o, s = fn(*sl, s, _chunk_size(length, chunk))
        outs.append(o)
        start += length
    return jnp.concatenate(outs, axis=2), s


def _causal_dwconv(x, buf, w, b):
    width = w.shape[0]
    length = x.shape[1]
    xp = jnp.concatenate([buf.astype(x.dtype), x], axis=1)
    y = b + xp[:, 0:length] * w[0]
    for j in range(1, width):
        y = y + xp[:, j:j + length] * w[j]
    return y, xp[:, length:]


def _linear_combine(left, right):
    a1, b1 = left
    a2, b2 = right
    return a1 * a2, a2 * b1 + b2


def _mixer_ret_hgrn(x, pos, segments, s_ret, s_hgrn, layer, p):
    bsz, length, _ = x.shape
    f32 = jnp.float32
    sizes = [H_RET * DK_RET, H_RET * DK_RET, D_RET, D_RET, H_HGRN * DK_HGRN, H_HGRN * DK_HGRN, D_HGRN, D_HGRN]
    proj = x @ p['w_in_ab']
    rq, rk, rv, rg, hq, hf, hi, hg = jnp.split(proj, np.cumsum(sizes)[:-1].tolist(), axis=-1)

    def heads(t, h):
        return t.reshape(bsz, length, h, -1).transpose(0, 2, 1, 3).astype(f32)

    q = _rope(heads(rq, H_RET), pos)
    k = _rope(heads(rk, H_RET), pos) * (DK_RET ** -0.5)
    v = heads(rv, H_RET)
    log_gamma = jnp.log1p(-jnp.exp2(-5.0 - jnp.arange(H_RET, dtype=f32)))
    o_ret, s_ret_new = _segmented(
        lambda q_, k_, v_, s_, c_: _retention_chunked(q_, k_, v_, log_gamma, s_, c_),
        (q, k, v), s_ret.astype(f32), segments, RET_CHUNK)
    mu = o_ret.mean(-1, keepdims=True)
    var = jnp.mean(jnp.square(o_ret - mu), -1, keepdims=True)
    o_ret = (o_ret - mu) * lax.rsqrt(var + LN_EPS)
    o_ret = o_ret.transpose(0, 2, 1, 3).reshape(bsz, length, D_RET) * jax.nn.silu(rg.astype(f32))

    lb = jnp.cumsum(jax.nn.softmax(p['hgrn_lb_logits'].astype(f32), axis=0), axis=0)[layer]
    lb = lb.reshape(H_HGRN, 1, DK_HGRN)
    f = lb + (1.0 - lb) * jax.nn.sigmoid(heads(hf, H_HGRN))
    o_h, s_h_new = _segmented(_gla_chunked, (heads(hq, H_HGRN), 1.0 - f, heads(hi, H_HGRN), jnp.log(f)),
                              s_hgrn.astype(f32), segments, HGRN_CHUNK)
    o_h = o_h * lax.rsqrt(jnp.mean(jnp.square(o_h), -1, keepdims=True) + LN_EPS) * p['hgrn_norm_w'].astype(f32)
    o_h = o_h.transpose(0, 2, 1, 3).reshape(bsz, length, D_HGRN) * jax.nn.silu(hg.astype(f32))

    mixed = jnp.concatenate([o_ret, o_h], axis=-1).astype(x.dtype)
    return mixed @ p['w_out_ab'], s_ret_new.astype(s_ret.dtype), s_h_new.astype(s_hgrn.dtype)


def _mixer_rglru(x, h0, conv_buf, p):
    bsz, length, _ = x.shape
    f32 = jnp.float32
    gate_br, rnn_br = jnp.split(x @ p['w_in_c'], 2, axis=-1)
    xc, new_buf = _causal_dwconv(rnn_br, conv_buf, p['conv_w_c'], p['conv_b_c'])
    xb = xc.reshape(bsz, length, N_LRU_BLOCKS, LRU_BLOCK)
    r = jax.nn.sigmoid((jnp.einsum('blnc,ncd->blnd', xb, p['w_gate_a']).reshape(bsz, length, D_RNN) + p['b_gate_a']).astype(f32))
    i = jax.nn.sigmoid((jnp.einsum('blnc,ncd->blnd', xb, p['w_gate_x']).reshape(bsz, length, D_RNN) + p['b_gate_x']).astype(f32))
    log_a = LRU_C * r * jax.nn.log_sigmoid(p['lru_lambda'].astype(f32))
    a = jnp.exp(log_a)
    b = jnp.sqrt(-jnp.expm1(2.0 * log_a)) * (i * xc.astype(f32))
    b = b.at[:, 0].add(a[:, 0] * h0.astype(f32))
    _, h = lax.associative_scan(_linear_combine, (a, b), axis=1)
    y = jax.nn.gelu(gate_br.astype(f32)) * h
    return y.astype(x.dtype) @ p['w_out_c'], h[:, -1].astype(h0.dtype), new_buf.astype(conv_buf.dtype)


def _conv_ffn(x, buf, w_up, conv_w, conv_b, w_down):
    u, v = jnp.split(x @ w_up, 2, axis=-1)
    uc, new_buf = _causal_dwconv(u, buf, conv_w, conv_b)
    h = jax.nn.gelu(uc.astype(jnp.float32)) * v.astype(jnp.float32)
    return h.astype(x.dtype) @ w_down, new_buf.astype(buf.dtype)


def _trunk(x, pos, segments, s_ret, s_hgrn, h_lru, buf_lru, buf_ffn, p):
    new_ffn_bufs = []
    for layer in range(DEPTH):
        if layer % 2 == 0:
            m, s_ret, s_hgrn = _mixer_ret_hgrn(x, pos, segments, s_ret, s_hgrn, layer, p)
        else:
            m, h_lru, buf_lru = _mixer_rglru(x, h_lru, buf_lru, p)
        x = _layernorm(DEEPNORM_ALPHA * x + m, p['ln_mix_g'][layer], p['ln_mix_b'][layer])
        f, nb = _conv_ffn(x, buf_ffn[layer], p['w_ffn_up'][layer], p['ffn_conv_w'][layer],
                          p['ffn_conv_b'][layer], p['w_ffn_down'][layer])
        new_ffn_bufs.append(nb)
        x = _layernorm(DEEPNORM_ALPHA * x + f, p['ln_ffn_g'][layer], p['ln_ffn_b'][layer])
    return x, s_ret, s_hgrn, h_lru, buf_lru, jnp.stack(new_ffn_bufs)


def setup_inputs(seed: int = 0) -> dict:
    key = jax.random.key(seed)
    ks = jax.random.split(key, 32)

    def nrm(k, shape, scale):
        return scale * jax.random.normal(k, shape, jnp.float32)

    lb_base = jnp.where(jnp.arange(DEPTH + 1)[:, None] == 0, -2.0, 0.0)
    a_init = jax.random.uniform(ks[19], (D_RNN,), jnp.float32, 0.9, 0.999)
    a_root = a_init ** (1.0 / LRU_C)
    return {
        'x_prompt': nrm(ks[0], (BATCH, SEQ, D_MODEL), 1.0),
        'x_sample': nrm(ks[1], (DEC_BATCH, DEC_SEQ, D_MODEL), 1.0),
        'state_ret': nrm(ks[2], (DEC_BATCH, H_RET, DK_RET, DV_RET), 0.5),
        'state_hgrn': nrm(ks[3], (DEC_BATCH, H_HGRN, DK_HGRN, DV_HGRN), 0.5),
        'state_rglru_h': nrm(ks[4], (DEC_BATCH, D_RNN), 0.5),
        'state_rglru_conv': nrm(ks[5], (DEC_BATCH, CONV_W_LRU - 1, D_RNN), 1.0),
        'state_ffn_conv': nrm(ks[6], (DEPTH, DEC_BATCH, CONV_W_FFN - 1, D_FF), 1.0),
        'meta_tokens': nrm(ks[7], (N_META, D_MODEL), 1.0),
        'w_in_ab': nrm(ks[8], (D_MODEL, D_IN_AB), D_MODEL ** -0.5),
        'w_out_ab': nrm(ks[9], (D_MIX_AB, D_MODEL), DEEPNORM_BETA * D_MIX_AB ** -0.5),
        'hgrn_lb_logits': lb_base + nrm(ks[10], (DEPTH + 1, H_HGRN * DK_HGRN), 0.1),
        'hgrn_norm_w': 1.0 + nrm(ks[11], (DV_HGRN,), 0.02),
        'w_in_c': nrm(ks[12], (D_MODEL, 2 * D_RNN), D_MODEL ** -0.5),
        'conv_w_c': nrm(ks[13], (CONV_W_LRU, D_RNN), CONV_W_LRU ** -0.5),
        'conv_b_c': nrm(ks[14], (D_RNN,), 0.02),
        'w_gate_a': nrm(ks[15], (N_LRU_BLOCKS, LRU_BLOCK, LRU_BLOCK), LRU_BLOCK ** -0.5),
        'b_gate_a': nrm(ks[16], (D_RNN,), 0.02),
        'w_gate_x': nrm(ks[17], (N_LRU_BLOCKS, LRU_BLOCK, LRU_BLOCK), LRU_BLOCK ** -0.5),
        'b_gate_x': nrm(ks[18], (D_RNN,), 0.02),
        'lru_lambda': jnp.log(a_root) - jnp.log1p(-a_root),
        'w_out_c': nrm(ks[20], (D_RNN, D_MODEL), DEEPNORM_BETA * D_RNN ** -0.5),
        'ln_mix_g': 1.0 + nrm(ks[21], (DEPTH, D_MODEL), 0.02),
        'ln_mix_b': nrm(ks[22], (DEPTH, D_MODEL), 0.02),
        'ln_ffn_g': 1.0 + nrm(ks[23], (DEPTH, D_MODEL), 0.02),
        'ln_ffn_b': nrm(ks[24], (DEPTH, D_MODEL), 0.02),
        'w_ffn_up': nrm(ks[25], (DEPTH, D_MODEL, 2 * D_FF), D_MODEL ** -0.5),
        'ffn_conv_w': nrm(ks[26], (DEPTH, CONV_W_FFN, D_FF), CONV_W_FFN ** -0.5),
        'ffn_conv_b': nrm(ks[27], (DEPTH, D_FF), 0.02),
        'w_ffn_down': nrm(ks[28], (DEPTH, D_FF, D_MODEL), DEEPNORM_BETA * D_FF ** -0.5),
    }


def reference(x_prompt, x_sample, state_ret, state_hgrn, state_rglru_h, state_rglru_conv, state_ffn_conv,
              meta_tokens, w_in_ab, w_out_ab, hgrn_lb_logits, hgrn_norm_w, w_in_c, conv_w_c, conv_b_c,
              w_gate_a, b_gate_a, w_gate_x, b_gate_x, lru_lambda, w_out_c, ln_mix_g, ln_mix_b,
              ln_ffn_g, ln_ffn_b, w_ffn_up, ffn_conv_w, ffn_conv_b, w_ffn_down):
    p = dict(w_in_ab=w_in_ab, w_out_ab=w_out_ab, hgrn_lb_logits=hgrn_lb_logits, hgrn_norm_w=hgrn_norm_w,
             w_in_c=w_in_c, conv_w_c=conv_w_c, conv_b_c=conv_b_c, w_gate_a=w_gate_a, b_gate_a=b_gate_a,
             w_gate_x=w_gate_x, b_gate_x=b_gate_x, lru_lambda=lru_lambda, w_out_c=w_out_c,
             ln_mix_g=ln_mix_g, ln_mix_b=ln_mix_b, ln_ffn_g=ln_ffn_g, ln_ffn_b=ln_ffn_b,
             w_ffn_up=w_ffn_up, ffn_conv_w=ffn_conv_w, ffn_conv_b=ffn_conv_b, w_ffn_down=w_ffn_down)

    bsz, seq_len, _ = x_prompt.shape
    dt = x_prompt.dtype
    meta = jnp.broadcast_to(meta_tokens.astype(dt)[None], (bsz, N_META, D_MODEL))
    xp = jnp.concatenate([meta, x_prompt], axis=1)
    pos_p = jnp.arange(N_META + seq_len, dtype=jnp.int32)
    yp, ret_p, hgrn_p, h_p, cbuf_p, fbuf_p = _trunk(
        xp, pos_p, (N_META, seq_len),
        jnp.zeros((bsz, H_RET, DK_RET, DV_RET), dt),
        jnp.zeros((bsz, H_HGRN, DK_HGRN, DV_HGRN), dt),
        jnp.zeros((bsz, D_RNN), dt),
        jnp.zeros((bsz, CONV_W_LRU - 1, D_RNN), dt),
        jnp.zeros((DEPTH, bsz, CONV_W_FFN - 1, D_FF), dt),
        p)
    y_prompt = yp[:, N_META:]

    n_new = x_sample.shape[1]
    pos_s = PAST_LEN + jnp.arange(n_new, dtype=jnp.int32)
    y_sample, ret_s, hgrn_s, h_s, cbuf_s, fbuf_s = _trunk(
        x_sample, pos_s, (n_new,), state_ret, state_hgrn, state_rglru_h, state_rglru_conv, state_ffn_conv, p)

    return (y_prompt, y_sample, ret_p, ret_s, hgrn_p, hgrn_s, h_p, h_s, cbuf_p, cbuf_s, fbuf_p, fbuf_s)
```

```python
import functools
import math

import jax
import jax.numpy as jnp
from jax import lax
from jax.experimental import pallas as pl
from jax.experimental.pallas import tpu as pltpu

f32 = jnp.float32
bf16 = jnp.bfloat16

D_MODEL = 1024
N_META = 16
PAST_LEN = 16384
N_HEADS = 4
D_HEAD = 128
SEG = N_HEADS * D_HEAD
ROPE_BASE = 10000.0
D_RNN = 1024
N_LRU_BLOCKS = 8
LRU_BLOCK = D_RNN // N_LRU_BLOCKS
CONV_W_LRU = 4
LRU_C = 8.0
D_FF = 2816
CONV_W_FFN = 3
LN_EPS = 1e-5
DEPTH = 2
DEEPNORM_ALPHA = (2.0 * DEPTH) ** 0.25
LOG_GAMMA = tuple(math.log1p(-(2.0 ** (-5.0 - h))) for h in range(N_HEADS))

SUBLANES = 8
CHUNK_ROWS = 128
VMEM_LIMIT_BYTES = 56 * 1024 * 1024


def _dot(a, b):
    return jnp.dot(a, b, preferred_element_type=f32)


def _dot_nt(a, b):
    return lax.dot_general(a, b, (((1,), (1,)), ((), ())), preferred_element_type=f32)


def _layernorm_rows(y, g, b):
    mu = jnp.mean(y, axis=-1, keepdims=True)
    yc = y - mu
    var = jnp.mean(yc * yc, axis=-1, keepdims=True)
    return yc * lax.rsqrt(var + LN_EPS) * g + b


def _const_spec(shape):
    nd = len(shape)
    return pl.BlockSpec(shape, lambda b, t: (0,) * nd, pipeline_mode=pl.Buffered(1))


def _state_spec(block, per_seq):
    nd = len(block)
    if per_seq:
        return pl.BlockSpec(block, lambda b, t: (b,) + (0,) * (nd - 1))
    return pl.BlockSpec(block, lambda b, t: (0,) * nd)


def _params():
    return pltpu.CompilerParams(dimension_semantics=("arbitrary", "arbitrary"),
                                vmem_limit_bytes=VMEM_LIMIT_BYTES)


def _mixer0_kernel(x_ref, c2_ref, s2_ref, win_ref, wout_ref, lbl_ref, nw_ref, g_ref, be_ref,
                   sret0_ref, shg0_ref, y_ref, sret_ref, shg_ref,
                   dmask_ref, wq_ref, wk_ref, tri_ref, lvl_ref, *, BB, TL, cb, tl):
    T = cb * tl
    R = BB * TL
    nch = R // T
    tl_shift = tl.bit_length() - 1

    @pl.when((pl.program_id(0) == 0) & (pl.program_id(1) == 0))
    def _init_tables():
        r = lax.broadcasted_iota(jnp.int32, (T, T), 0)
        c = lax.broadcasted_iota(jnp.int32, (T, T), 1)
        same = (r >> tl_shift) == (c >> tl_shift)
        rel = (r & (tl - 1)) - (c & (tl - 1))
        causal = same & (rel >= 0)
        relf = jnp.maximum(rel, 0).astype(f32)
        tri_ref[...] = jnp.where(causal, 1.0, 0.0).astype(bf16)
        lvl = jnp.where(r == c, 0, -1)
        s, li = 1, 1
        while s < tl:
            blk = (r >> (li)) == (c >> (li))
            hit = blk & ((r & (2 * s - 1)) >= s) & ((c & (2 * s - 1)) < s)
            lvl = jnp.where(hit, li, lvl)
            s, li = 2 * s, li + 1
        lvl_ref[...] = lvl
        tr = (lax.broadcasted_iota(jnp.int32, (T, D_HEAD), 0) & (tl - 1)).astype(f32)
        for h in range(N_HEADS):
            lg = LOG_GAMMA[h]
            dmask_ref[h] = jnp.where(causal, jnp.exp(lg * relf), 0.0)
            wq_ref[h] = jnp.exp(lg * (tr + 1.0))
            wk_ref[h] = jnp.exp(lg * ((tl - 1.0) - tr))

    @pl.when(pl.program_id(1) == 0)
    def _init_state():
        sret_ref[...] = sret0_ref[...]
        shg_ref[...] = shg0_ref[...]

    x2 = x_ref[...].reshape(R, D_MODEL)
    xb = x2.astype(bf16)
    rq, rk, rv, rg, hq, hf, hi, hg = [
        _dot(xb, win_ref[:, i * SEG:(i + 1) * SEG]) for i in range(8)]

    lbl = lbl_ref[...]
    le = jnp.exp(lbl - jnp.max(lbl, axis=0, keepdims=True))
    lb = le[0:1] / jnp.sum(le, axis=0, keepdims=True)
    nw = nw_ref[...]
    row = lax.broadcasted_iota(jnp.int32, (T, D_HEAD), 0)

    def seqs(z):
        return z.reshape(cb, tl, D_HEAD)

    def rows(z3):
        return z3.reshape(T, D_HEAD)

    def qs(q3, s3):
        if cb == 1:
            return _dot(q3[0], s3[0])[None]
        return jnp.einsum('bqd,bdv->bqv', q3, s3, preferred_element_type=f32)

    def ktv(k3, v3):
        if cb == 1:
            return lax.dot_general(k3[0], v3[0], (((0,), (0,)), ((), ())),
                                   preferred_element_type=f32)[None]
        return jnp.einsum('btd,btv->bdv', k3, v3, preferred_element_type=f32)

    chunk_out = []
    for c in range(nch):
        rs = slice(c * T, (c + 1) * T)
        if BB == 1:
            s0, ts = 0, slice(c * tl, (c + 1) * tl)
        else:
            s0, ts = c * cb, slice(0, tl)
        ss = slice(s0, s0 + cb)
        c2 = c2_ref[ts, :][None]
        s2 = s2_ref[ts, :][None]

        def rope(z):
            return rows(seqs(z) * c2 + seqs(pltpu.roll(z, D_HEAD // 2, 1)) * s2)

        head_out = [None] * (2 * N_HEADS)
        for h in range(N_HEADS):
            hs = slice(h * D_HEAD, (h + 1) * D_HEAD)
            q = rope(rq[rs, hs])
            k = rope(rk[rs, hs]) * (D_HEAD ** -0.5)
            v = rv[rs, hs].astype(bf16)
            attn = _dot_nt(q.astype(bf16), k.astype(bf16)) * dmask_ref[h]
            sp = sret_ref[ss, h]
            o = _dot(attn.astype(bf16), v)
            o = o + rows(qs(seqs((q * wq_ref[h]).astype(bf16)), sp.astype(bf16)))
            upd = ktv(seqs((k * wk_ref[h]).astype(bf16)), seqs(v))
            sret_ref[ss, h] = math.exp(LOG_GAMMA[h] * tl) * sp + upd
            mu = jnp.mean(o, axis=-1, keepdims=True)
            oc = o - mu
            var = jnp.mean(oc * oc, axis=-1, keepdims=True)
            head_out[h] = oc * lax.rsqrt(var + LN_EPS) * jax.nn.silu(rg[rs, hs])

            lbh = lb[:, hs]
            f = lbh + (1.0 - lbh) * jax.nn.sigmoid(hf[rs, hs])
            logf = jnp.log(f)
            kk = 1.0 - f
            gq = hq[rs, hs]
            gv = hi[rs, hs].astype(bf16)
            l1 = logf.astype(bf16)
            r1 = logf - l1.astype(f32)
            l2 = r1.astype(bf16)
            l3 = (r1 - l2.astype(f32)).astype(bf16)
            cs = _dot(tri_ref[...], jnp.concatenate([l1, l2, l3], axis=1))
            b = cs[:, :D_HEAD] + cs[:, D_HEAD:2 * D_HEAD] + cs[:, 2 * D_HEAD:]
            attn = jnp.where(lvl_ref[...] == 0, _dot_nt(gq.astype(bf16), kk.astype(bf16)), 0.0)
            s, li = 1, 1
            while s < tl:
                up = (row & (2 * s - 1)) >= s
                if s == 1:
                    qe, ke = gq * f, kk
                else:
                    if s == 2:
                        w = row & 3
                        e = jnp.where(w == 0, pltpu.roll(f, T - 1, 0),
                                      jnp.where(w == 1, 1.0,
                                                jnp.where(w == 2, f, f * pltpu.roll(f, 1, 0))))
                    else:
                        parts = []
                        for m in range(T // (2 * s)):
                            r0 = m * 2 * s
                            parts.append(jnp.abs(b[r0:r0 + 2 * s, :] - b[r0 + s - 1:r0 + s, :]))
                        e = jnp.exp(-(parts[0] if len(parts) == 1 else jnp.concatenate(parts, axis=0)))
                    qe, ke = gq * e, kk * e
                qt = jnp.where(up, qe, 0.0).astype(bf16)
                kt = jnp.where(up, 0.0, ke).astype(bf16)
                attn = jnp.where(lvl_ref[...] == li, _dot_nt(qt, kt), attn)
                s, li = 2 * s, li + 1
            sg = shg_ref[ss, h]
            og = _dot(attn.astype(bf16), gv)
            og = og + rows(qs(seqs((gq * jnp.exp(b)).astype(bf16)), sg.astype(bf16)))
            b3 = seqs(b)
            bl = b3[:, tl - 1:tl, :]
            upd = ktv((seqs(kk) * jnp.exp(bl - b3)).astype(bf16), seqs(gv))
            ebl = jnp.exp(bl)
            for j in range(cb):
                decay = jnp.broadcast_to(ebl[j], (D_HEAD, D_HEAD)).T
                shg_ref[s0 + j, h] = decay * sg[j] + upd[j]
            ms = jnp.mean(og * og, axis=-1, keepdims=True)
            head_out[N_HEADS + h] = og * lax.rsqrt(ms + LN_EPS) * nw * jax.nn.silu(hg[rs, hs])
        chunk_out.append(jnp.concatenate(head_out, axis=1))
    mixed = chunk_out[0] if nch == 1 else jnp.concatenate(chunk_out, axis=0)
    m = _dot(mixed.astype(bf16), wout_ref[...])
    y = _layernorm_rows(DEEPNORM_ALPHA * x2 + m, g_ref[...], be_ref[...])
    y_ref[...] = y.reshape(BB, TL, D_MODEL)


def _mixer0(x, c2, s2, w_in, w_out, lb_logits, norm_w, ln_g, ln_b, sret0, shg0, *, BB, TL):
    B, L, _ = x.shape
    R = BB * TL
    T = min(R, CHUNK_ROWS)
    if BB == 1:
        cb, tl = 1, T
    else:
        assert T % TL == 0
        cb, tl = T // TL, TL
    per_seq = sret0.shape[0] == B and B > 1
    grid = (B // BB, L // TL)
    st_block = (BB, N_HEADS, D_HEAD, D_HEAD)
    kern = functools.partial(_mixer0_kernel, BB=BB, TL=TL, cb=cb, tl=tl)
    return pl.pallas_call(
        kern, grid=grid,
        in_specs=[
            pl.BlockSpec((BB, TL, D_MODEL), lambda b, t: (b, t, 0)),
            pl.BlockSpec((TL, D_HEAD), lambda b, t: (t, 0)),
            pl.BlockSpec((TL, D_HEAD), lambda b, t: (t, 0)),
            _const_spec(w_in.shape), _const_spec(w_out.shape), _const_spec(lb_logits.shape),
            _const_spec(norm_w.shape), _const_spec(ln_g.shape), _const_spec(ln_b.shape),
            _state_spec(st_block, per_seq), _state_spec(st_block, per_seq),
        ],
        out_specs=[
            pl.BlockSpec((BB, TL, D_MODEL), lambda b, t: (b, t, 0)),
            _state_spec(st_block, True), _state_spec(st_block, True),
        ],
        out_shape=[
            jax.ShapeDtypeStruct((B, L, D_MODEL), f32),
            jax.ShapeDtypeStruct((B, N_HEADS, D_HEAD, D_HEAD), f32),
            jax.ShapeDtypeStruct((B, N_HEADS, D_HEAD, D_HEAD), f32),
        ],
        scratch_shapes=[
            pltpu.VMEM((N_HEADS, T, T), f32),
            pltpu.VMEM((N_HEADS, T, D_HEAD), f32),
            pltpu.VMEM((N_HEADS, T, D_HEAD), f32),
            pltpu.VMEM((T, T), bf16),
            pltpu.VMEM((T, T), jnp.int32),
        ],
        compiler_params=_params(), name=f"mixer0_b{BB}_t{TL}",
    )(x, c2, s2, w_in, w_out, lb_logits, norm_w, ln_g, ln_b, sret0, shg0)


def _ffn_kernel(x_ref, wup_ref, cw_ref, cb_ref, wdn_ref, g_ref, be_ref, buf0_ref,
                y_ref, buf_ref, u_ref, *, BB, TL):
    R = BB * TL
    K = CONV_W_FFN - 1

    @pl.when(pl.program_id(1) == 0)
    def _init_state():
        buf_ref[...] = buf0_ref[...]

    x2 = x_ref[...].reshape(R, D_MODEL)
    xb = x2.astype(bf16)
    u = _dot(xb, wup_ref[:, :D_FF])
    v = _dot(xb, wup_ref[:, D_FF:])
    u_ref[:, SUBLANES - K:SUBLANES, :] = buf_ref[...]
    u_ref[:, SUBLANES:, :] = u.reshape(BB, TL, D_FF)
    cw = cw_ref[...]
    uc = cb_ref[...] + u_ref[:, SUBLANES - K:SUBLANES - K + TL, :] * cw[0:1]
    for j in range(1, CONV_W_FFN):
        uc = uc + u_ref[:, SUBLANES - K + j:SUBLANES - K + j + TL, :] * cw[j:j + 1]
    buf_ref[...] = u_ref[:, SUBLANES + TL - K:SUBLANES + TL, :]
    hmid = jax.nn.gelu(uc.reshape(R, D_FF)) * v
    fo = _dot(hmid.astype(bf16), wdn_ref[...])
    y = _layernorm_rows(DEEPNORM_ALPHA * x2 + fo, g_ref[...], be_ref[...])
    y_ref[...] = y.reshape(BB, TL, D_MODEL)


def _ffn(x, w_up, conv_w, conv_b, w_down, ln_g, ln_b, buf0, *, BB, TL):
    B, L, _ = x.shape
    per_seq = buf0.shape[0] == B and B > 1
    K = CONV_W_FFN - 1
    kern = functools.partial(_ffn_kernel, BB=BB, TL=TL)
    return pl.pallas_call(
        kern, grid=(B // BB, L // TL),
        in_specs=[
            pl.BlockSpec((BB, TL, D_MODEL), lambda b, t: (b, t, 0)),
            _const_spec(w_up.shape), _const_spec(conv_w.shape), _const_spec(conv_b.shape),
            _const_spec(w_down.shape), _const_spec(ln_g.shape), _const_spec(ln_b.shape),
            _state_spec((BB, K, D_FF), per_seq),
        ],
        out_specs=[
            pl.BlockSpec((BB, TL, D_MODEL), lambda b, t: (b, t, 0)),
            _state_spec((BB, K, D_FF), True),
        ],
        out_shape=[
            jax.ShapeDtypeStruct((B, L, D_MODEL), f32),
            jax.ShapeDtypeStruct((B, K, D_FF), f32),
        ],
        scratch_shapes=[pltpu.VMEM((BB, SUBLANES + TL, D_FF), f32)],
        compiler_params=_params(), name=f"ffn_b{BB}_t{TL}",
    )(x, w_up, conv_w, conv_b, w_down, ln_g, ln_b, buf0)


def _rglru_kernel(x_ref, win_ref, cw_ref, cb_ref, wga_ref, bga_ref, wgx_ref, bgx_ref, lam_ref,
                  wout_ref, g_ref, be_ref, h0_ref, buf0_ref,
                  y_ref, h_ref, buf_ref, r_ref, *, BB, TL):
    R = BB * TL
    K = CONV_W_LRU - 1
    G = TL // SUBLANES

    @pl.when(pl.program_id(1) == 0)
    def _init_state():
        h_ref[...] = h0_ref[...]
        buf_ref[...] = buf0_ref[...]

    x2 = x_ref[...].reshape(R, D_MODEL)
    xb = x2.astype(bf16)
    gate = _dot(xb, win_ref[:, :D_RNN])
    rnn = _dot(xb, win_ref[:, D_RNN:])
    r_ref[:, SUBLANES - K:SUBLANES, :] = buf_ref[...]
    r_ref[:, SUBLANES:, :] = rnn.reshape(BB, TL, D_RNN)
    cw = cw_ref[...]
    xc = cb_ref[...] + r_ref[:, SUBLANES - K:SUBLANES - K + TL, :] * cw[0:1]
    for j in range(1, CONV_W_LRU):
        xc = xc + r_ref[:, SUBLANES - K + j:SUBLANES - K + j + TL, :] * cw[j:j + 1]
    buf_ref[...] = r_ref[:, SUBLANES + TL - K:SUBLANES + TL, :]
    xc = xc.reshape(R, D_RNN)
    xcb = xc.astype(bf16)
    ga = jnp.concatenate([_dot(xcb[:, n * LRU_BLOCK:(n + 1) * LRU_BLOCK], wga_ref[n])
                          for n in range(N_LRU_BLOCKS)], axis=1)
    gx = jnp.concatenate([_dot(xcb[:, n * LRU_BLOCK:(n + 1) * LRU_BLOCK], wgx_ref[n])
                          for n in range(N_LRU_BLOCKS)], axis=1)
    rgate = jax.nn.sigmoid(ga + bga_ref[...])
    igate = jax.nn.sigmoid(gx + bgx_ref[...])
    log_a = LRU_C * rgate * jax.nn.log_sigmoid(lam_ref[...])
    a = jnp.exp(log_a)
    th = jnp.tanh(log_a)
    bv = jnp.sqrt(-2.0 * th / (1.0 - th)) * (igate * xc)

    a4 = a.reshape(R // SUBLANES, SUBLANES, D_RNN)
    b4 = bv.reshape(R // SUBLANES, SUBLANES, D_RNN)
    sub = lax.broadcasted_iota(jnp.int32, a4.shape, 1)
    for d in (1, 2, 4):
        keep = sub >= d
        b4 = jnp.where(keep, a4 * pltpu.roll(b4, d, 1) + b4, b4)
        a4 = jnp.where(keep, a4 * pltpu.roll(a4, d, 1), a4)
    a5 = a4.reshape(BB, G, SUBLANES, D_RNN)
    b5 = b4.reshape(BB, G, SUBLANES, D_RNN)
    carry = h_ref[...]
    tiles = []
    for gi in range(G):
        hg = a5[:, gi] * carry + b5[:, gi]
        carry = hg[:, SUBLANES - 1:SUBLANES, :]
        tiles.append(hg)
    h_ref[...] = carry
    hseq = tiles[0] if G == 1 else jnp.concatenate(tiles, axis=1)
    yv = jax.nn.gelu(gate) * hseq.reshape(R, D_RNN)
    m = _dot(yv.astype(bf16), wout_ref[...])
    y = _layernorm_rows(DEEPNORM_ALPHA * x2 + m, g_ref[...], be_ref[...])
    y_ref[...] = y.reshape(BB, TL, D_MODEL)


def _rglru(x, w_in, conv_w, conv_b, wga, bga, wgx, bgx, lam, w_out, ln_g, ln_b, h0, buf0, *, BB, TL):
    B, L, _ = x.shape
    per_seq = h0.shape[0] == B and B > 1
    K = CONV_W_LRU - 1
    kern = functools.partial(_rglru_kernel, BB=BB, TL=TL)
    consts = (w_in, conv_w, conv_b, wga, bga, wgx, bgx, lam, w_out, ln_g, ln_b)
    return pl.pallas_call(
        kern, grid=(B // BB, L // TL),
        in_specs=[pl.BlockSpec((BB, TL, D_MODEL), lambda b, t: (b, t, 0))]
        + [_const_spec(c.shape) for c in consts]
        + [_state_spec((BB, 1, D_RNN), per_seq), _state_spec((BB, K, D_RNN), per_seq)],
        out_specs=[
            pl.BlockSpec((BB, TL, D_MODEL), lambda b, t: (b, t, 0)),
            _state_spec((BB, 1, D_RNN), True), _state_spec((BB, K, D_RNN), True),
        ],
        out_shape=[
            jax.ShapeDtypeStruct((B, L, D_MODEL), f32),
            jax.ShapeDtypeStruct((B, 1, D_RNN), f32),
            jax.ShapeDtypeStruct((B, K, D_RNN), f32),
        ],
        scratch_shapes=[pltpu.VMEM((BB, SUBLANES + TL, D_RNN), f32)],
        compiler_params=_params(), name=f"rglru_b{BB}_t{TL}",
    )(x, *consts, h0, buf0)


def _rope_tables(pos0, length):
    half = D_HEAD // 2
    pos = pos0 + jnp.arange(length, dtype=jnp.int32)
    inv = ROPE_BASE ** (-jnp.arange(half, dtype=f32) / half)
    ang = pos.astype(f32)[:, None] * inv[None, :]
    cos, sin = jnp.cos(ang), jnp.sin(ang)
    return jnp.concatenate([cos, cos], axis=1), jnp.concatenate([-sin, sin], axis=1)


def _trunk(x, pos0, states, p, *, BB, TL, BB0):
    s_ret, s_hgrn, h_lru, buf_lru, buf_ffn0, buf_ffn1 = states
    c2, s2 = _rope_tables(pos0, x.shape[1])
    x, s_ret, s_hgrn = _mixer0(x, c2, s2, p['w_in_ab'], p['w_out_ab'], p['hgrn_lb_logits'], p['hgrn_norm_w'],
                               p['ln_mix_g'][0], p['ln_mix_b'][0], s_ret, s_hgrn, BB=BB0, TL=TL)
    x, buf_ffn0 = _ffn(x, p['w_ffn_up'][0], p['ffn_conv_w'][0], p['ffn_conv_b'][0], p['w_ffn_down'][0],
                       p['ln_ffn_g'][0], p['ln_ffn_b'][0], buf_ffn0, BB=BB, TL=TL)
    x, h_lru, buf_lru = _rglru(x, p['w_in_c'], p['conv_w_c'], p['conv_b_c'], p['w_gate_a'], p['b_gate_a'],
                               p['w_gate_x'], p['b_gate_x'], p['lru_lambda'], p['w_out_c'],
                               p['ln_mix_g'][1], p['ln_mix_b'][1], h_lru, buf_lru, BB=BB, TL=TL)
    x, buf_ffn1 = _ffn(x, p['w_ffn_up'][1], p['ffn_conv_w'][1], p['ffn_conv_b'][1], p['w_ffn_down'][1],
                       p['ln_ffn_g'][1], p['ln_ffn_b'][1], buf_ffn1, BB=BB, TL=TL)
    return x, (s_ret, s_hgrn, h_lru, buf_lru, buf_ffn0, buf_ffn1)


def kernel(x_prompt, x_sample, state_ret, state_hgrn, state_rglru_h, state_rglru_conv, state_ffn_conv, meta_tokens, w_in_ab, w_out_ab, hgrn_lb_logits, hgrn_norm_w, w_in_c, conv_w_c, conv_b_c, w_gate_a, b_gate_a, w_gate_x, b_gate_x, lru_lambda, w_out_c, ln_mix_g, ln_mix_b, ln_ffn_g, ln_ffn_b, w_ffn_up, ffn_conv_w, ffn_conv_b, w_ffn_down):
    row = lambda z: z.reshape(1, -1)
    p = dict(
        w_in_ab=w_in_ab.astype(bf16), w_out_ab=w_out_ab.astype(bf16),
        hgrn_lb_logits=hgrn_lb_logits, hgrn_norm_w=row(hgrn_norm_w),
        w_in_c=w_in_c.astype(bf16), conv_w_c=conv_w_c, conv_b_c=row(conv_b_c),
        w_gate_a=w_gate_a.astype(bf16), b_gate_a=row(b_gate_a),
        w_gate_x=w_gate_x.astype(bf16), b_gate_x=row(b_gate_x),
        lru_lambda=row(lru_lambda), w_out_c=w_out_c.astype(bf16),
        ln_mix_g=[row(ln_mix_g[i]) for i in range(DEPTH)], ln_mix_b=[row(ln_mix_b[i]) for i in range(DEPTH)],
        ln_ffn_g=[row(ln_ffn_g[i]) for i in range(DEPTH)], ln_ffn_b=[row(ln_ffn_b[i]) for i in range(DEPTH)],
        w_ffn_up=[w_ffn_up[i].astype(bf16) for i in range(DEPTH)],
        ffn_conv_w=[ffn_conv_w[i] for i in range(DEPTH)],
        ffn_conv_b=[row(ffn_conv_b[i]) for i in range(DEPTH)],
        w_ffn_down=[w_ffn_down[i].astype(bf16) for i in range(DEPTH)],
    )
    dt = x_prompt.dtype
    zero_states = (
        jnp.zeros((1, N_HEADS, D_HEAD, D_HEAD), dt), jnp.zeros((1, N_HEADS, D_HEAD, D_HEAD), dt),
        jnp.zeros((1, 1, D_RNN), dt), jnp.zeros((1, CONV_W_LRU - 1, D_RNN), dt),
        jnp.zeros((1, CONV_W_FFN - 1, D_FF), dt), jnp.zeros((1, CONV_W_FFN - 1, D_FF), dt),
    )
    _, meta_states = _trunk(meta_tokens.astype(dt)[None], 0, zero_states, p, BB=1, TL=N_META, BB0=1)
    y_prompt, ps = _trunk(x_prompt, N_META, meta_states, p, BB=1, TL=256, BB0=1)
    sample_states = (state_ret, state_hgrn, state_rglru_h[:, None, :], state_rglru_conv,
                     state_ffn_conv[0], state_ffn_conv[1])
    y_sample, ss = _trunk(x_sample, PAST_LEN, sample_states, p, BB=32, TL=x_sample.shape[1], BB0=16)
    return (y_prompt, y_sample, ps[0], ss[0], ps[1], ss[1], ps[2][:, 0, :], ss[2][:, 0, :], ps[3], ss[3],
            jnp.stack([ps[4], ps[5]]), jnp.stack([ss[4], ss[5]]))
```

```python
import functools
import math

import jax
import jax.numpy as jnp
from jax import lax
from jax.experimental import pallas as pl
from jax.experimental.pallas import tpu as pltpu

f32 = jnp.float32
bf16 = jnp.bfloat16

D_MODEL = 1024
N_META = 16
PAST_LEN = 16384
N_HEADS = 4
D_HEAD = 128
SEG = N_HEADS * D_HEAD
ROPE_BASE = 10000.0
D_RNN = 1024
N_LRU_BLOCKS = 8
LRU_BLOCK = D_RNN // N_LRU_BLOCKS
CONV_W_LRU = 4
LRU_C = 8.0
D_FF = 2816
CONV_W_FFN = 3
LN_EPS = 1e-5
DEPTH = 2
DEEPNORM_ALPHA = (2.0 * DEPTH) ** 0.25
LOG_GAMMA = tuple(math.log1p(-(2.0 ** (-5.0 - h))) for h in range(N_HEADS))

SUBLANES = 8
CHUNK_ROWS = 128
LRU_STEP_COLS = 256
LRU_OUT_ROW_GROUPS = 2
PROMPT_BLOCK_STEPS = 512
VMEM_LIMIT_BYTES = 56 * 1024 * 1024

_GELU_K1 = -2.0 * math.log2(math.e) * math.sqrt(2.0 / math.pi)
_GELU_K3 = _GELU_K1 * 0.044715


def _gelu(x):
    return x / (1.0 + jnp.exp2(x * (_GELU_K1 + _GELU_K3 * (x * x))))


def _dot(a, b):
    return jnp.dot(a, b, preferred_element_type=f32)


def _dot_nt(a, b):
    return lax.dot_general(a, b, (((1,), (1,)), ((), ())), preferred_element_type=f32)


def _layernorm_rows(y, g, b):
    mu = jnp.mean(y, axis=-1, keepdims=True)
    yc = y - mu
    var = jnp.mean(yc * yc, axis=-1, keepdims=True)
    return yc * lax.rsqrt(var + LN_EPS) * g + b


def _const_spec(shape):
    nd = len(shape)
    return pl.BlockSpec(shape, lambda b, t: (0,) * nd, pipeline_mode=pl.Buffered(1))


def _state_spec(block, per_seq):
    nd = len(block)
    if per_seq:
        return pl.BlockSpec(block, lambda b, t: (b,) + (0,) * (nd - 1))
    return pl.BlockSpec(block, lambda b, t: (0,) * nd)


def _params():
    return pltpu.CompilerParams(dimension_semantics=("arbitrary", "arbitrary"),
                                vmem_limit_bytes=VMEM_LIMIT_BYTES)


def _mixer0_init_tables(dmask_ref, wq_ref, wk_ref, tri_ref, lvl_ref, *, cb, tl):
    T = cb * tl
    tl_shift = tl.bit_length() - 1
    r = lax.broadcasted_iota(jnp.int32, (T, T), 0)
    c = lax.broadcasted_iota(jnp.int32, (T, T), 1)
    same = (r >> tl_shift) == (c >> tl_shift)
    rel = (r & (tl - 1)) - (c & (tl - 1))
    causal = same & (rel >= 0)
    relf = jnp.maximum(rel, 0).astype(f32)
    tri_ref[...] = jnp.where(causal, 1.0, 0.0).astype(bf16)
    lvl = jnp.where(r == c, 0, -1)
    s, li = 1, 1
    while s < tl:
        blk = (r >> li) == (c >> li)
        hit = blk & ((r & (2 * s - 1)) >= s) & ((c & (2 * s - 1)) < s)
        lvl = jnp.where(hit, li, lvl)
        s, li = 2 * s, li + 1
    lvl_ref[...] = lvl
    tr = (lax.broadcasted_iota(jnp.int32, (T, D_HEAD), 0) & (tl - 1)).astype(f32)
    for h in range(N_HEADS):
        lg = LOG_GAMMA[h]
        dmask_ref[h] = jnp.where(causal, jnp.exp(lg * relf), 0.0)
        wq_ref[h] = jnp.exp(lg * (tr + 1.0))
        wk_ref[h] = jnp.exp(lg * ((tl - 1.0) - tr))


def _hgrn_scores(gqs, kks, fs, bs, lvl_ref, *, T, tl):
    n = len(gqs)
    row = lax.broadcasted_iota(jnp.int32, (T, D_HEAD), 0)
    ntile = T // SUBLANES

    def owned(li, g):
        return lvl_ref[g * SUBLANES:(g + 1) * SUBLANES, :] == li

    full = [_dot_nt(gqs[u].astype(bf16), kks[u].astype(bf16)) for u in range(n)]
    tiles = [[jnp.where(owned(0, g), full[u][g * SUBLANES:(g + 1) * SUBLANES], 0.0) for g in range(ntile)]
             for u in range(n)]

    def take(u, li, p, first_tile, n_tiles, p_row0):
        for g in range(n_tiles):
            t = first_tile + g
            tiles[u][t] = jnp.where(owned(li, t), p[p_row0 + g * SUBLANES:p_row0 + (g + 1) * SUBLANES], tiles[u][t])

    s, li = 1, 1
    while s < tl:
        nblk = T // (2 * s)
        if s < SUBLANES:
            up = (row & (2 * s - 1)) >= s
            zs = []
            for u in range(n):
                gq, kk, f, b = gqs[u], kks[u], fs[u], bs[u]
                if s == 1:
                    z = jnp.where(up, gq * f, kk)
                elif s == 2:
                    w = row & 3
                    e = jnp.where(w == 0, pltpu.roll(f, T - 1, 0),
                                  jnp.where(w == 1, 1.0, jnp.where(w == 2, f, f * pltpu.roll(f, 1, 0))))
                    z = jnp.where(up, gq, kk) * e
                else:
                    parts = [jnp.abs(b[m * 2 * s:(m + 1) * 2 * s, :] - b[m * 2 * s + s - 1:m * 2 * s + s, :])
                             for m in range(nblk)]
                    z = jnp.where(up, gq, kk) * jnp.exp2(-jnp.concatenate(parts, axis=0))
                zs.append(z.astype(bf16))
            ps = [_dot_nt(zb, zb) for zb in zs]
            for u in range(n):
                take(u, li, ps[u], 0, ntile, 0)
        else:
            zs, qus = [], []
            for u in range(n):
                gq, kk, b = gqs[u], kks[u], bs[u]
                both, upper = [], []
                for m in range(nblk):
                    r0 = m * 2 * s
                    beta = b[r0 + s - 1:r0 + s, :]
                    k_lo = kk[r0:r0 + s] * jnp.exp2(beta - b[r0:r0 + s])
                    q_hi = gq[r0 + s:r0 + 2 * s] * jnp.exp2(b[r0 + s:r0 + 2 * s] - beta)
                    both += [k_lo, q_hi]
                    upper.append(q_hi)
                zs.append(jnp.concatenate(both, axis=0).astype(bf16))
                qus.append((upper[0] if nblk == 1 else jnp.concatenate(upper, axis=0)).astype(bf16))
            ps = [_dot_nt(qus[u], zs[u]) for u in range(n)]
            for u in range(n):
                for m in range(nblk):
                    take(u, li, ps[u], (m * 2 * s + s) // SUBLANES, s // SUBLANES, m * s)
        s, li = 2 * s, li + 1
    return [jnp.concatenate(tiles[u], axis=0).astype(bf16) for u in range(n)]


def _mixer0_kernel(x_ref, c2_ref, s2_ref, win_ref, wout_ref, lbl_ref, nw_ref, g_ref, be_ref,
                   sret0_ref, shg0_ref, y_ref, sret_ref, shg_ref,
                   dmask_ref, wq_ref, wk_ref, tri_ref, lvl_ref, *, BB, TL, cb, tl):
    T = cb * tl
    R = BB * TL
    nch = R // T

    @pl.when((pl.program_id(0) == 0) & (pl.program_id(1) == 0))
    def _init_tables():
        _mixer0_init_tables(dmask_ref, wq_ref, wk_ref, tri_ref, lvl_ref, cb=cb, tl=tl)

    @pl.when(pl.program_id(1) == 0)
    def _init_state():
        sret_ref[...] = sret0_ref[...]
        shg_ref[...] = shg0_ref[...]

    x2 = x_ref[...].reshape(R, D_MODEL)
    xb = x2.astype(bf16)

    def proj(i):
        return _dot(xb, win_ref[:, i * SEG:(i + 1) * SEG])

    lbl = lbl_ref[...]
    le = jnp.exp(lbl - jnp.max(lbl, axis=0, keepdims=True))
    lb = le[0:1] / jnp.sum(le, axis=0, keepdims=True)
    nw = nw_ref[...]

    def seqs(z):
        return z.reshape(cb, tl, D_HEAD)

    def rows(z3):
        return z3.reshape(T, D_HEAD)

    def qs(q3, s3):
        if cb == 1:
            return _dot(q3[0], s3[0])[None]
        return jnp.einsum('bqd,bdv->bqv', q3, s3, preferred_element_type=f32)

    def ktv(k3, v3):
        if cb == 1:
            return lax.dot_general(k3[0], v3[0], (((0,), (0,)), ((), ())),
                                   preferred_element_type=f32)[None]
        return jnp.einsum('btd,btv->bdv', k3, v3, preferred_element_type=f32)

    units = [(c, h) for c in range(nch) for h in range(N_HEADS)]
    nu = len(units)

    def sl(z, c, h):
        return z[c * T:(c + 1) * T, h * D_HEAD:(h + 1) * D_HEAD]

    def seq0(c):
        return 0 if BB == 1 else c * cb

    def rope_tables(c):
        ts = slice(c * tl, (c + 1) * tl) if BB == 1 else slice(0, tl)
        return c2_ref[ts, :][None], s2_ref[ts, :][None]

    def rope(z, cos2, sin2):
        return rows(seqs(z) * cos2 + seqs(pltpu.roll(z, D_HEAD // 2, 1)) * sin2)

    tabs = [rope_tables(c) for c in range(nch)]
    ktabs = [(c2 * (D_HEAD ** -0.5), s2 * (D_HEAD ** -0.5)) for c2, s2 in tabs]

    rq, rk = proj(0), proj(1)
    q = [rope(sl(rq, c, h), *tabs[c]) for c, h in units]
    k = [rope(sl(rk, c, h), *ktabs[c]) for c, h in units]
    rv = proj(2)
    v = [sl(rv, c, h).astype(bf16) for c, h in units]
    att = [_dot_nt(q[u].astype(bf16), k[u].astype(bf16)) for u in range(nu)]
    att = [(att[u] * dmask_ref[h]).astype(bf16) for u, (c, h) in enumerate(units)]
    o_ret = [_dot(att[u], v[u]) for u in range(nu)]
    qw = [seqs((q[u] * wq_ref[h]).astype(bf16)) for u, (c, h) in enumerate(units)]
    upd_ret = [ktv(seqs((k[u] * wk_ref[h]).astype(bf16)), seqs(v[u])) for u, (c, h) in enumerate(units)]

    hf = proj(5)
    lbs = [lb[:, h * D_HEAD:(h + 1) * D_HEAD] for h in range(N_HEADS)]
    f = [lbs[h] + (1.0 - lbs[h]) * jax.nn.sigmoid(sl(hf, c, h)) for c, h in units]
    lf = [jnp.log2(z) for z in f]
    kk = [1.0 - z for z in f]
    hq, hi = proj(4), proj(6)
    gq = [sl(hq, c, h) for c, h in units]
    gv = [sl(hi, c, h).astype(bf16) for c, h in units]
    split = []
    for z in lf:
        l1 = z.astype(bf16)
        r1 = z - l1.astype(f32)
        l2 = r1.astype(bf16)
        l3 = (r1 - l2.astype(f32)).astype(bf16)
        split.append(jnp.concatenate([l1, l2, l3], axis=1))
    cs = [_dot(tri_ref[...], z) for z in split]
    b = [z[:, :D_HEAD] + z[:, D_HEAD:2 * D_HEAD] + z[:, 2 * D_HEAD:] for z in cs]
    scores = _hgrn_scores(gq, kk, f, b, lvl_ref, T=T, tl=tl)
    o_hg = [_dot(scores[u], gv[u]) for u in range(nu)]
    qe = [seqs((gq[u] * jnp.exp2(b[u])).astype(bf16)) for u in range(nu)]
    b3 = [seqs(z) for z in b]
    bl = [z[:, tl - 1:tl, :] for z in b3]
    upd_hg = [ktv((seqs(kk[u]) * jnp.exp2(bl[u] - b3[u])).astype(bf16), seqs(gv[u])) for u in range(nu)]
    decay = []
    for z in bl:
        ez = jnp.exp2(z)
        d = [jnp.broadcast_to(ez[j], (D_HEAD, D_HEAD)).T for j in range(cb)]
        decay.append(d[0][None] if cb == 1 else jnp.stack(d))

    s_ret, s_hg = {}, {}
    for u, (c, h) in enumerate(units):
        s0 = seq0(c)
        first = BB > 1 or c == 0
        sp = sret_ref[s0:s0 + cb, h] if first else s_ret[h]
        sg = shg_ref[s0:s0 + cb, h] if first else s_hg[h]
        o_ret[u] = o_ret[u] + rows(qs(qw[u], sp.astype(bf16)))
        o_hg[u] = o_hg[u] + rows(qs(qe[u], sg.astype(bf16)))
        s_ret[h] = math.exp(LOG_GAMMA[h] * tl) * sp + upd_ret[u]
        s_hg[h] = decay[u] * sg + upd_hg[u]
        if BB > 1 or c == nch - 1:
            sret_ref[s0:s0 + cb, h] = s_ret[h]
            shg_ref[s0:s0 + cb, h] = s_hg[h]

    rg, hg = proj(3), proj(7)
    head_out = {}
    for u, (c, h) in enumerate(units):
        o = o_ret[u]
        mu = jnp.mean(o, axis=-1, keepdims=True)
        oc = o - mu
        var = jnp.mean(oc * oc, axis=-1, keepdims=True)
        head_out[(c, h)] = oc * lax.rsqrt(var + LN_EPS) * jax.nn.silu(sl(rg, c, h))
        og = o_hg[u]
        ms = jnp.mean(og * og, axis=-1, keepdims=True)
        head_out[(c, N_HEADS + h)] = og * lax.rsqrt(ms + LN_EPS) * nw * jax.nn.silu(sl(hg, c, h))
    chunk_out = [jnp.concatenate([head_out[(c, j)] for j in range(2 * N_HEADS)], axis=1).astype(bf16)
                 for c in range(nch)]
    mixed = chunk_out[0] if nch == 1 else jnp.concatenate(chunk_out, axis=0)
    m = _dot(mixed, wout_ref[...])
    y = _layernorm_rows(DEEPNORM_ALPHA * x2 + m, g_ref[...], be_ref[...])
    y_ref[...] = y.reshape(BB, TL, D_MODEL)


def _mixer0(x, c2, s2, w_in, w_out, lb_logits, norm_w, ln_g, ln_b, sret0, shg0, *, BB, TL):
    B, L, _ = x.shape
    R = BB * TL
    T = min(R, CHUNK_ROWS)
    if BB == 1:
        cb, tl = 1, T
    else:
        assert T % TL == 0
        cb, tl = T // TL, TL
    per_seq = sret0.shape[0] == B and B > 1
    st_block = (BB, N_HEADS, D_HEAD, D_HEAD)
    kern = functools.partial(_mixer0_kernel, BB=BB, TL=TL, cb=cb, tl=tl)
    return pl.pallas_call(
        kern, grid=(B // BB, L // TL),
        in_specs=[
            pl.BlockSpec((BB, TL, D_MODEL), lambda b, t: (b, t, 0)),
            pl.BlockSpec((TL, D_HEAD), lambda b, t: (t, 0)),
            pl.BlockSpec((TL, D_HEAD), lambda b, t: (t, 0)),
            _const_spec(w_in.shape), _const_spec(w_out.shape), _const_spec(lb_logits.shape),
            _const_spec(norm_w.shape), _const_spec(ln_g.shape), _const_spec(ln_b.shape),
            _state_spec(st_block, per_seq), _state_spec(st_block, per_seq),
        ],
        out_specs=[
            pl.BlockSpec((BB, TL, D_MODEL), lambda b, t: (b, t, 0)),
            _state_spec(st_block, True), _state_spec(st_block, True),
        ],
        out_shape=[
            jax.ShapeDtypeStruct((B, L, D_MODEL), f32),
            jax.ShapeDtypeStruct((B, N_HEADS, D_HEAD, D_HEAD), f32),
            jax.ShapeDtypeStruct((B, N_HEADS, D_HEAD, D_HEAD), f32),
        ],
        scratch_shapes=[
            pltpu.VMEM((N_HEADS, T, T), f32),
            pltpu.VMEM((N_HEADS, T, D_HEAD), f32),
            pltpu.VMEM((N_HEADS, T, D_HEAD), f32),
            pltpu.VMEM((T, T), bf16),
            pltpu.VMEM((T, T), jnp.int32),
        ],
        compiler_params=_params(), name=f"mixer0_b{BB}_t{TL}",
    )(x, c2, s2, w_in, w_out, lb_logits, norm_w, ln_g, ln_b, sret0, shg0)


def _ffn_kernel(x_ref, wup_ref, cw_ref, cb_ref, wdn_ref, g_ref, be_ref, buf0_ref,
                y_ref, buf_ref, u_ref, *, BB, TL):
    R = BB * TL
    K = CONV_W_FFN - 1

    @pl.when(pl.program_id(1) == 0)
    def _init_state():
        buf_ref[...] = buf0_ref[...]

    x2 = x_ref[...].reshape(R, D_MODEL)
    xb = x2.astype(bf16)
    u = _dot(xb, wup_ref[:, :D_FF])
    v = _dot(xb, wup_ref[:, D_FF:])
    u_ref[:, SUBLANES - K:SUBLANES, :] = buf_ref[...]
    u_ref[:, SUBLANES:, :] = u.reshape(BB, TL, D_FF)
    cw = cw_ref[...]
    uc = cb_ref[...] + u_ref[:, SUBLANES - K:SUBLANES - K + TL, :] * cw[0:1]
    for j in range(1, CONV_W_FFN):
        uc = uc + u_ref[:, SUBLANES - K + j:SUBLANES - K + j + TL, :] * cw[j:j + 1]
    buf_ref[...] = u_ref[:, SUBLANES + TL - K:SUBLANES + TL, :]
    hmid = _gelu(uc.reshape(R, D_FF)) * v
    fo = _dot(hmid.astype(bf16), wdn_ref[...])
    y = _layernorm_rows(DEEPNORM_ALPHA * x2 + fo, g_ref[...], be_ref[...])
    y_ref[...] = y.reshape(BB, TL, D_MODEL)


def _ffn(x, w_up, conv_w, conv_b, w_down, ln_g, ln_b, buf0, *, BB, TL):
    B, L, _ = x.shape
    per_seq = buf0.shape[0] == B and B > 1
    K = CONV_W_FFN - 1
    kern = functools.partial(_ffn_kernel, BB=BB, TL=TL)
    return pl.pallas_call(
        kern, grid=(B // BB, L // TL),
        in_specs=[
            pl.BlockSpec((BB, TL, D_MODEL), lambda b, t: (b, t, 0)),
            _const_spec(w_up.shape), _const_spec(conv_w.shape), _const_spec(conv_b.shape),
            _const_spec(w_down.shape), _const_spec(ln_g.shape), _const_spec(ln_b.shape),
            _state_spec((BB, K, D_FF), per_seq),
        ],
        out_specs=[
            pl.BlockSpec((BB, TL, D_MODEL), lambda b, t: (b, t, 0)),
            _state_spec((BB, K, D_FF), True),
        ],
        out_shape=[
            jax.ShapeDtypeStruct((B, L, D_MODEL), f32),
            jax.ShapeDtypeStruct((B, K, D_FF), f32),
        ],
        scratch_shapes=[pltpu.VMEM((BB, SUBLANES + TL, D_FF), f32)],
        compiler_params=_params(), name=f"ffn_b{BB}_t{TL}",
    )(x, w_up, conv_w, conv_b, w_down, ln_g, ln_b, buf0)


def _rglru_kernel(x_ref, win_ref, cw_ref, cb_ref, wga_ref, bga_ref, wgx_ref, bgx_ref, lam_ref,
                  wout_ref, g_ref, be_ref, h0_ref, buf0_ref,
                  y_ref, h_ref, buf_ref, r_ref, *, BB, TL):
    R = BB * TL
    K = CONV_W_LRU - 1
    G = TL // SUBLANES

    @pl.when(pl.program_id(1) == 0)
    def _init_state():
        h_ref[...] = h0_ref[...]
        buf_ref[...] = buf0_ref[...]

    x2 = x_ref[...].reshape(R, D_MODEL)
    xb = x2.astype(bf16)
    W = LRU_STEP_COLS
    nchunk = D_RNN // W
    r_ref[:, SUBLANES - K:SUBLANES, :] = buf_ref[...]
    sub = lax.broadcasted_iota(jnp.int32, (R // SUBLANES, SUBLANES, W), 1)
    h0 = h_ref[...]

    def project(ci):
        c0 = ci * W
        gate = _dot(xb, win_ref[:, c0:c0 + W])
        r_ref[:, SUBLANES:, c0:c0 + W] = _dot(xb, win_ref[:, D_RNN + c0:D_RNN + c0 + W]).reshape(BB, TL, W)
        return gate

    def recur(ci, gate):
        c0 = ci * W
        cs = slice(c0, c0 + W)
        xc = cb_ref[:, cs] + r_ref[:, SUBLANES - K:SUBLANES - K + TL, cs] * cw_ref[0:1, cs]
        for j in range(1, CONV_W_LRU):
            xc = xc + r_ref[:, SUBLANES - K + j:SUBLANES - K + j + TL, cs] * cw_ref[j:j + 1, cs]
        xc = xc.reshape(R, W)
        xcb = xc.astype(bf16)
        blocks = [(j, c0 // LRU_BLOCK + j) for j in range(W // LRU_BLOCK)]
        ga = jnp.concatenate([_dot(xcb[:, j * LRU_BLOCK:(j + 1) * LRU_BLOCK], wga_ref[n]) for j, n in blocks], axis=1)
        gx = jnp.concatenate([_dot(xcb[:, j * LRU_BLOCK:(j + 1) * LRU_BLOCK], wgx_ref[n]) for j, n in blocks], axis=1)
        rgate = jax.nn.sigmoid(ga + bga_ref[:, cs])
        igate = jax.nn.sigmoid(gx + bgx_ref[:, cs])
        log_a = LRU_C * rgate * jax.nn.log_sigmoid(lam_ref[:, cs])
        a = jnp.exp(log_a)
        th = jnp.tanh(log_a)
        bv = jnp.sqrt(-2.0 * th / (1.0 - th)) * (igate * xc)
        a4 = a.reshape(R // SUBLANES, SUBLANES, W)
        b4 = bv.reshape(R // SUBLANES, SUBLANES, W)
        for d in (1, 2, 4):
            keep = sub >= d
            b4 = jnp.where(keep, a4 * pltpu.roll(b4, d, 1) + b4, b4)
            a4 = jnp.where(keep, a4 * pltpu.roll(a4, d, 1), a4)
        a5 = a4.reshape(BB, G, SUBLANES, W)
        b5 = b4.reshape(BB, G, SUBLANES, W)
        hc = h0[:, :, cs]
        tiles = []
        for gi in range(G):
            hg = a5[:, gi] * hc + b5[:, gi]
            hc = hg[:, SUBLANES - 1:SUBLANES, :]
            tiles.append(hg)
        hseq = tiles[0] if G == 1 else jnp.concatenate(tiles, axis=1)
        return (_gelu(gate) * hseq.reshape(R, W)).astype(bf16), hc

    gate_next = project(0)
    ys, h_last = [], []
    for ci in range(nchunk):
        gate, gate_next = gate_next, (project(ci + 1) if ci + 1 < nchunk else None)
        yv, hc = recur(ci, gate)
        ys.append(yv)
        h_last.append(hc)
    buf_ref[...] = r_ref[:, SUBLANES + TL - K:SUBLANES + TL, :]
    h_ref[...] = jnp.concatenate(h_last, axis=2)
    yv = jnp.concatenate(ys, axis=1)
    groups = LRU_OUT_ROW_GROUPS if R % (LRU_OUT_ROW_GROUPS * 2 * SUBLANES) == 0 else 1
    step = R // groups
    outs = [_layernorm_rows(DEEPNORM_ALPHA * x2[i * step:(i + 1) * step]
                            + _dot(yv[i * step:(i + 1) * step], wout_ref[...]), g_ref[...], be_ref[...])
            for i in range(groups)]
    y = outs[0] if groups == 1 else jnp.concatenate(outs, axis=0)
    y_ref[...] = y.reshape(BB, TL, D_MODEL)


def _rglru(x, w_in, conv_w, conv_b, wga, bga, wgx, bgx, lam, w_out, ln_g, ln_b, h0, buf0, *, BB, TL):
    B, L, _ = x.shape
    per_seq = h0.shape[0] == B and B > 1
    K = CONV_W_LRU - 1
    kern = functools.partial(_rglru_kernel, BB=BB, TL=TL)
    consts = (w_in, conv_w, conv_b, wga, bga, wgx, bgx, lam, w_out, ln_g, ln_b)
    return pl.pallas_call(
        kern, grid=(B // BB, L // TL),
        in_specs=[pl.BlockSpec((BB, TL, D_MODEL), lambda b, t: (b, t, 0))]
        + [_const_spec(c.shape) for c in consts]
        + [_state_spec((BB, 1, D_RNN), per_seq), _state_spec((BB, K, D_RNN), per_seq)],
        out_specs=[
            pl.BlockSpec((BB, TL, D_MODEL), lambda b, t: (b, t, 0)),
            _state_spec((BB, 1, D_RNN), True), _state_spec((BB, K, D_RNN), True),
        ],
        out_shape=[
            jax.ShapeDtypeStruct((B, L, D_MODEL), f32),
            jax.ShapeDtypeStruct((B, 1, D_RNN), f32),
            jax.ShapeDtypeStruct((B, K, D_RNN), f32),
        ],
        scratch_shapes=[pltpu.VMEM((BB, SUBLANES + TL, D_RNN), f32)],
        compiler_params=_params(), name=f"rglru_b{BB}_t{TL}",
    )(x, *consts, h0, buf0)


def _rope_tables(pos0, length):
    half = D_HEAD // 2
    pos = pos0 + jnp.arange(length, dtype=jnp.int32)
    inv = ROPE_BASE ** (-jnp.arange(half, dtype=f32) / half)
    ang = pos.astype(f32)[:, None] * inv[None, :]
    cos, sin = jnp.cos(ang), jnp.sin(ang)
    return jnp.concatenate([cos, cos], axis=1), jnp.concatenate([-sin, sin], axis=1)


def _trunk(x, pos0, states, p, *, BB, TL, BB0):
    s_ret, s_hgrn, h_lru, buf_lru, buf_ffn0, buf_ffn1 = states
    c2, s2 = _rope_tables(pos0, x.shape[1])
    x, s_ret, s_hgrn = _mixer0(x, c2, s2, p['w_in_ab'], p['w_out_ab'], p['hgrn_lb_logits'], p['hgrn_norm_w'],
                               p['ln_mix_g'][0], p['ln_mix_b'][0], s_ret, s_hgrn, BB=BB0, TL=TL)
    x, buf_ffn0 = _ffn(x, p['w_ffn_up'][0], p['ffn_conv_w'][0], p['ffn_conv_b'][0], p['w_ffn_down'][0],
                       p['ln_ffn_g'][0], p['ln_ffn_b'][0], buf_ffn0, BB=BB, TL=TL)
    x, h_lru, buf_lru = _rglru(x, p['w_in_c'], p['conv_w_c'], p['conv_b_c'], p['w_gate_a'], p['b_gate_a'],
                               p['w_gate_x'], p['b_gate_x'], p['lru_lambda'], p['w_out_c'],
                               p['ln_mix_g'][1], p['ln_mix_b'][1], h_lru, buf_lru, BB=BB, TL=TL)
    x, buf_ffn1 = _ffn(x, p['w_ffn_up'][1], p['ffn_conv_w'][1], p['ffn_conv_b'][1], p['w_ffn_down'][1],
                       p['ln_ffn_g'][1], p['ln_ffn_b'][1], buf_ffn1, BB=BB, TL=TL)
    return x, (s_ret, s_hgrn, h_lru, buf_lru, buf_ffn0, buf_ffn1)


def kernel(x_prompt, x_sample, state_ret, state_hgrn, state_rglru_h, state_rglru_conv, state_ffn_conv, meta_tokens, w_in_ab, w_out_ab, hgrn_lb_logits, hgrn_norm_w, w_in_c, conv_w_c, conv_b_c, w_gate_a, b_gate_a, w_gate_x, b_gate_x, lru_lambda, w_out_c, ln_mix_g, ln_mix_b, ln_ffn_g, ln_ffn_b, w_ffn_up, ffn_conv_w, ffn_conv_b, w_ffn_down):
    row = lambda z: z.reshape(1, -1)
    p = dict(
        w_in_ab=w_in_ab.astype(bf16), w_out_ab=w_out_ab.astype(bf16),
        hgrn_lb_logits=hgrn_lb_logits, hgrn_norm_w=row(hgrn_norm_w),
        w_in_c=w_in_c.astype(bf16), conv_w_c=conv_w_c, conv_b_c=row(conv_b_c),
        w_gate_a=w_gate_a.astype(bf16), b_gate_a=row(b_gate_a),
        w_gate_x=w_gate_x.astype(bf16), b_gate_x=row(b_gate_x),
        lru_lambda=row(lru_lambda), w_out_c=w_out_c.astype(bf16),
        ln_mix_g=[row(ln_mix_g[i]) for i in range(DEPTH)], ln_mix_b=[row(ln_mix_b[i]) for i in range(DEPTH)],
        ln_ffn_g=[row(ln_ffn_g[i]) for i in range(DEPTH)], ln_ffn_b=[row(ln_ffn_b[i]) for i in range(DEPTH)],
        w_ffn_up=[w_ffn_up[i].astype(bf16) for i in range(DEPTH)],
        ffn_conv_w=[ffn_conv_w[i] for i in range(DEPTH)],
        ffn_conv_b=[row(ffn_conv_b[i]) for i in range(DEPTH)],
        w_ffn_down=[w_ffn_down[i].astype(bf16) for i in range(DEPTH)],
    )
    dt = x_prompt.dtype
    zero_states = (
        jnp.zeros((1, N_HEADS, D_HEAD, D_HEAD), dt), jnp.zeros((1, N_HEADS, D_HEAD, D_HEAD), dt),
        jnp.zeros((1, 1, D_RNN), dt), jnp.zeros((1, CONV_W_LRU - 1, D_RNN), dt),
        jnp.zeros((1, CONV_W_FFN - 1, D_FF), dt), jnp.zeros((1, CONV_W_FFN - 1, D_FF), dt),
    )
    _, meta_states = _trunk(meta_tokens.astype(dt)[None], 0, zero_states, p, BB=1, TL=N_META, BB0=1)
    y_prompt, ps = _trunk(x_prompt, N_META, meta_states, p, BB=1, TL=PROMPT_BLOCK_STEPS, BB0=1)
    sample_states = (state_ret, state_hgrn, state_rglru_h[:, None, :], state_rglru_conv,
                     state_ffn_conv[0], state_ffn_conv[1])
    y_sample, ss = _trunk(x_sample, PAST_LEN, sample_states, p, BB=32, TL=x_sample.shape[1], BB0=16)
    return (y_prompt, y_sample, ps[0], ss[0], ps[1], ss[1], ps[2][:, 0, :], ss[2][:, 0, :], ps[3], ss[3],
            jnp.stack([ps[4], ps[5]]), jnp.stack([ss[4], ss[5]]))
```

```python
import functools
import math

import jax
import jax.numpy as jnp
from jax import lax
from jax.experimental import pallas as pl
from jax.experimental.pallas import tpu as pltpu

f32 = jnp.float32
bf16 = jnp.bfloat16

D_MODEL = 1024
N_META = 16
PAST_LEN = 16384
N_HEADS = 4
D_HEAD = 128
SEG = N_HEADS * D_HEAD
ROPE_BASE = 10000.0
D_RNN = 1024
N_LRU_BLOCKS = 8
LRU_BLOCK = D_RNN // N_LRU_BLOCKS
CONV_W_LRU = 4
LRU_C = 8.0
D_FF = 2816
CONV_W_FFN = 3
LN_EPS = 1e-5
DEPTH = 2
DEEPNORM_ALPHA = (2.0 * DEPTH) ** 0.25
LOG_GAMMA = tuple(math.log1p(-(2.0 ** (-5.0 - h))) for h in range(N_HEADS))

SUBLANES = 8
CHUNK_ROWS = 128
LRU_STEP_COLS = 256
LRU_OUT_ROW_GROUPS = 2
PROMPT_BLOCK_STEPS = 512
VMEM_LIMIT_BYTES = 56 * 1024 * 1024

_GELU_K1 = -2.0 * math.log2(math.e) * math.sqrt(2.0 / math.pi)
_GELU_K3 = _GELU_K1 * 0.044715


def _gelu(x):
    return x / (1.0 + jnp.exp2(x * (_GELU_K1 + _GELU_K3 * (x * x))))


def _dot(a, b):
    return jnp.dot(a, b, preferred_element_type=f32)


def _dot_nt(a, b):
    return lax.dot_general(a, b, (((1,), (1,)), ((), ())), preferred_element_type=f32)


def _layernorm_rows(y, g, b):
    mu = jnp.mean(y, axis=-1, keepdims=True)
    yc = y - mu
    var = jnp.mean(yc * yc, axis=-1, keepdims=True)
    return yc * lax.rsqrt(var + LN_EPS) * g + b


def _causal_conv(x3, carry, w_ref, b_ref, cs):
    BB, TL, W = x3.shape
    K = carry.shape[1]
    w = [w_ref[j:j + 1, cs] for j in range(K + 1)]

    def shift(p, j):
        bnd = w[0] * carry[:, K - j:K - j + 1, :]
        for i in range(1, j):
            bnd = bnd + w[i] * carry[:, K - j + i:K - j + i + 1, :]
        if TL == SUBLANES:
            t = lax.broadcasted_iota(jnp.int32, p.shape, 1)
            return jnp.where(t == 0, bnd, pltpu.roll(p, 1, 1))
        assert BB == 1
        rolled = pltpu.roll(p.reshape(TL, W), 1, 0)
        t = lax.broadcasted_iota(jnp.int32, (SUBLANES, W), 0)
        head = jnp.where(t == 0, bnd.reshape(1, W), rolled[:SUBLANES])
        return jnp.concatenate([head, rolled[SUBLANES:]], axis=0).reshape(BB, TL, W)

    p = w[0] * x3
    for j in range(1, K + 1):
        p = w[j] * x3 + shift(p, j)
    return b_ref[:, cs] + p


def _const_spec(shape):
    nd = len(shape)
    return pl.BlockSpec(shape, lambda b, t: (0,) * nd, pipeline_mode=pl.Buffered(1))


def _state_spec(block, per_seq):
    nd = len(block)
    if per_seq:
        return pl.BlockSpec(block, lambda b, t: (b,) + (0,) * (nd - 1))
    return pl.BlockSpec(block, lambda b, t: (0,) * nd)


def _params():
    return pltpu.CompilerParams(dimension_semantics=("arbitrary", "arbitrary"),
                                vmem_limit_bytes=VMEM_LIMIT_BYTES)


def _mixer0_init_tables(dmask_ref, wq_ref, wk_ref, tri_ref, lvl_ref, *, cb, tl):
    T = cb * tl
    tl_shift = tl.bit_length() - 1
    r = lax.broadcasted_iota(jnp.int32, (T, T), 0)
    c = lax.broadcasted_iota(jnp.int32, (T, T), 1)
    same = (r >> tl_shift) == (c >> tl_shift)
    rel = (r & (tl - 1)) - (c & (tl - 1))
    causal = same & (rel >= 0)
    relf = jnp.maximum(rel, 0).astype(f32)
    tri_ref[...] = jnp.where(causal, 1.0, 0.0).astype(bf16)
    lvl = jnp.where(r == c, 0, -1)
    s, li = 1, 1
    while s < tl:
        blk = (r >> li) == (c >> li)
        hit = blk & ((r & (2 * s - 1)) >= s) & ((c & (2 * s - 1)) < s)
        lvl = jnp.where(hit, li, lvl)
        s, li = 2 * s, li + 1
    lvl_ref[...] = lvl
    tr = (lax.broadcasted_iota(jnp.int32, (T, D_HEAD), 0) & (tl - 1)).astype(f32)
    for h in range(N_HEADS):
        lg = LOG_GAMMA[h]
        dmask_ref[h] = jnp.where(causal, jnp.exp(lg * relf), 0.0)
        wq_ref[h] = jnp.exp(lg * (tr + 1.0))
        wk_ref[h] = jnp.exp(lg * ((tl - 1.0) - tr))


def _hgrn_scores(gqs, kks, fs, bs, lvl_ref, *, T, tl):
    n = len(gqs)
    row = lax.broadcasted_iota(jnp.int32, (T, D_HEAD), 0)
    ntile = T // SUBLANES

    def owned(li, g):
        return lvl_ref[g * SUBLANES:(g + 1) * SUBLANES, :] == li

    full = [_dot_nt(gqs[u].astype(bf16), kks[u].astype(bf16)) for u in range(n)]
    tiles = [[jnp.where(owned(0, g), full[u][g * SUBLANES:(g + 1) * SUBLANES], 0.0) for g in range(ntile)]
             for u in range(n)]

    def take(u, li, p, first_tile, n_tiles, p_row0):
        for g in range(n_tiles):
            t = first_tile + g
            tiles[u][t] = jnp.where(owned(li, t), p[p_row0 + g * SUBLANES:p_row0 + (g + 1) * SUBLANES], tiles[u][t])

    s, li = 1, 1
    while s < tl:
        nblk = T // (2 * s)
        if s < SUBLANES:
            up = (row & (2 * s - 1)) >= s
            zs = []
            for u in range(n):
                gq, kk, f, b = gqs[u], kks[u], fs[u], bs[u]
                if s == 1:
                    z = jnp.where(up, gq * f, kk)
                elif s == 2:
                    w = row & 3
                    e = jnp.where(w == 0, pltpu.roll(f, T - 1, 0),
                                  jnp.where(w == 1, 1.0, jnp.where(w == 2, f, f * pltpu.roll(f, 1, 0))))
                    z = jnp.where(up, gq, kk) * e
                else:
                    parts = [jnp.abs(b[m * 2 * s:(m + 1) * 2 * s, :] - b[m * 2 * s + s - 1:m * 2 * s + s, :])
                             for m in range(nblk)]
                    z = jnp.where(up, gq, kk) * jnp.exp2(-jnp.concatenate(parts, axis=0))
                zs.append(z.astype(bf16))
            ps = [_dot_nt(zb, zb) for zb in zs]
            for u in range(n):
                take(u, li, ps[u], 0, ntile, 0)
        else:
            zs, qus = [], []
            for u in range(n):
                gq, kk, b = gqs[u], kks[u], bs[u]
                both, upper = [], []
                for m in range(nblk):
                    r0 = m * 2 * s
                    beta = b[r0 + s - 1:r0 + s, :]
                    k_lo = kk[r0:r0 + s] * jnp.exp2(beta - b[r0:r0 + s])
                    q_hi = gq[r0 + s:r0 + 2 * s] * jnp.exp2(b[r0 + s:r0 + 2 * s] - beta)
                    both += [k_lo, q_hi]
                    upper.append(q_hi)
                zs.append(jnp.concatenate(both, axis=0).astype(bf16))
                qus.append((upper[0] if nblk == 1 else jnp.concatenate(upper, axis=0)).astype(bf16))
            ps = [_dot_nt(qus[u], zs[u]) for u in range(n)]
            for u in range(n):
                for m in range(nblk):
                    take(u, li, ps[u], (m * 2 * s + s) // SUBLANES, s // SUBLANES, m * s)
        s, li = 2 * s, li + 1
    return [jnp.concatenate(tiles[u], axis=0).astype(bf16) for u in range(n)]


def _mixer0_kernel(x_ref, c2_ref, s2_ref, win_ref, wout_ref, lbl_ref, nw_ref, g_ref, be_ref,
                   sret0_ref, shg0_ref, y_ref, sret_ref, shg_ref,
                   dmask_ref, wq_ref, wk_ref, tri_ref, lvl_ref, *, BB, TL, cb, tl):
    T = cb * tl
    R = BB * TL
    nch = R // T

    @pl.when((pl.program_id(0) == 0) & (pl.program_id(1) == 0))
    def _init_tables():
        _mixer0_init_tables(dmask_ref, wq_ref, wk_ref, tri_ref, lvl_ref, cb=cb, tl=tl)

    @pl.when(pl.program_id(1) == 0)
    def _init_state():
        sret_ref[...] = sret0_ref[...]
        shg_ref[...] = shg0_ref[...]

    x2 = x_ref[...].reshape(R, D_MODEL)
    xb = x2.astype(bf16)

    def proj(i):
        return _dot(xb, win_ref[:, i * SEG:(i + 1) * SEG])

    lbl = lbl_ref[...]
    le = jnp.exp(lbl - jnp.max(lbl, axis=0, keepdims=True))
    lb = le[0:1] / jnp.sum(le, axis=0, keepdims=True)
    nw = nw_ref[...]

    def seqs(z):
        return z.reshape(cb, tl, D_HEAD)

    def rows(z3):
        return z3.reshape(T, D_HEAD)

    def qs(q3, s3):
        if cb == 1:
            return _dot(q3[0], s3[0])[None]
        return jnp.einsum('bqd,bdv->bqv', q3, s3, preferred_element_type=f32)

    def ktv(k3, v3):
        if cb == 1:
            return lax.dot_general(k3[0], v3[0], (((0,), (0,)), ((), ())),
                                   preferred_element_type=f32)[None]
        return jnp.einsum('btd,btv->bdv', k3, v3, preferred_element_type=f32)

    units = [(c, h) for c in range(nch) for h in range(N_HEADS)]
    nu = len(units)

    def sl(z, c, h):
        return z[c * T:(c + 1) * T, h * D_HEAD:(h + 1) * D_HEAD]

    def seq0(c):
        return 0 if BB == 1 else c * cb

    def rope_tables(c):
        ts = slice(c * tl, (c + 1) * tl) if BB == 1 else slice(0, tl)
        return c2_ref[ts, :][None], s2_ref[ts, :][None]

    def rope(z, cos2, sin2):
        return rows(seqs(z) * cos2 + seqs(pltpu.roll(z, D_HEAD // 2, 1)) * sin2)

    tabs = [rope_tables(c) for c in range(nch)]
    ktabs = [(c2 * (D_HEAD ** -0.5), s2 * (D_HEAD ** -0.5)) for c2, s2 in tabs]

    rq, rk = proj(0), proj(1)
    q = [rope(sl(rq, c, h), *tabs[c]) for c, h in units]
    k = [rope(sl(rk, c, h), *ktabs[c]) for c, h in units]
    rv = proj(2)
    v = [sl(rv, c, h).astype(bf16) for c, h in units]
    att = [_dot_nt(q[u].astype(bf16), k[u].astype(bf16)) for u in range(nu)]
    att = [(att[u] * dmask_ref[h]).astype(bf16) for u, (c, h) in enumerate(units)]
    o_ret = [_dot(att[u], v[u]) for u in range(nu)]
    qw = [seqs((q[u] * wq_ref[h]).astype(bf16)) for u, (c, h) in enumerate(units)]
    upd_ret = [ktv(seqs((k[u] * wk_ref[h]).astype(bf16)), seqs(v[u])) for u, (c, h) in enumerate(units)]

    hf = proj(5)
    lbs = [lb[:, h * D_HEAD:(h + 1) * D_HEAD] for h in range(N_HEADS)]
    f = [lbs[h] + (1.0 - lbs[h]) * jax.nn.sigmoid(sl(hf, c, h)) for c, h in units]
    lf = [jnp.log2(z) for z in f]
    kk = [1.0 - z for z in f]
    hq, hi = proj(4), proj(6)
    gq = [sl(hq, c, h) for c, h in units]
    gv = [sl(hi, c, h).astype(bf16) for c, h in units]
    split = []
    for z in lf:
        l1 = z.astype(bf16)
        r1 = z - l1.astype(f32)
        l2 = r1.astype(bf16)
        l3 = (r1 - l2.astype(f32)).astype(bf16)
        split.append(jnp.concatenate([l1, l2, l3], axis=1))
    cs = [_dot(tri_ref[...], z) for z in split]
    b = [z[:, :D_HEAD] + z[:, D_HEAD:2 * D_HEAD] + z[:, 2 * D_HEAD:] for z in cs]
    scores = _hgrn_scores(gq, kk, f, b, lvl_ref, T=T, tl=tl)
    o_hg = [_dot(scores[u], gv[u]) for u in range(nu)]
    qe = [seqs((gq[u] * jnp.exp2(b[u])).astype(bf16)) for u in range(nu)]
    b3 = [seqs(z) for z in b]
    bl = [z[:, tl - 1:tl, :] for z in b3]
    upd_hg = [ktv((seqs(kk[u]) * jnp.exp2(bl[u] - b3[u])).astype(bf16), seqs(gv[u])) for u in range(nu)]
    decay = []
    for z in bl:
        ez = jnp.exp2(z)
        d = [jnp.broadcast_to(ez[j], (D_HEAD, D_HEAD)).T for j in range(cb)]
        decay.append(d[0][None] if cb == 1 else jnp.stack(d))

    s_ret, s_hg = {}, {}
    for u, (c, h) in enumerate(units):
        s0 = seq0(c)
        first = BB > 1 or c == 0
        sp = sret_ref[s0:s0 + cb, h] if first else s_ret[h]
        sg = shg_ref[s0:s0 + cb, h] if first else s_hg[h]
        o_ret[u] = o_ret[u] + rows(qs(qw[u], sp.astype(bf16)))
        o_hg[u] = o_hg[u] + rows(qs(qe[u], sg.astype(bf16)))
        s_ret[h] = math.exp(LOG_GAMMA[h] * tl) * sp + upd_ret[u]
        s_hg[h] = decay[u] * sg + upd_hg[u]
        if BB > 1 or c == nch - 1:
            sret_ref[s0:s0 + cb, h] = s_ret[h]
            shg_ref[s0:s0 + cb, h] = s_hg[h]

    rg, hg = proj(3), proj(7)
    head_out = {}
    for u, (c, h) in enumerate(units):
        o = o_ret[u]
        mu = jnp.mean(o, axis=-1, keepdims=True)
        oc = o - mu
        var = jnp.mean(oc * oc, axis=-1, keepdims=True)
        head_out[(c, h)] = oc * lax.rsqrt(var + LN_EPS) * jax.nn.silu(sl(rg, c, h))
        og = o_hg[u]
        ms = jnp.mean(og * og, axis=-1, keepdims=True)
        head_out[(c, N_HEADS + h)] = og * lax.rsqrt(ms + LN_EPS) * nw * jax.nn.silu(sl(hg, c, h))
    chunk_out = [jnp.concatenate([head_out[(c, j)] for j in range(2 * N_HEADS)], axis=1).astype(bf16)
                 for c in range(nch)]
    mixed = chunk_out[0] if nch == 1 else jnp.concatenate(chunk_out, axis=0)
    m = _dot(mixed, wout_ref[...])
    y = _layernorm_rows(DEEPNORM_ALPHA * x2 + m, g_ref[...], be_ref[...])
    y_ref[...] = y.reshape(BB, TL, D_MODEL)


def _mixer0(x, c2, s2, w_in, w_out, lb_logits, norm_w, ln_g, ln_b, sret0, shg0, *, BB, TL):
    B, L, _ = x.shape
    R = BB * TL
    T = min(R, CHUNK_ROWS)
    if BB == 1:
        cb, tl = 1, T
    else:
        assert T % TL == 0
        cb, tl = T // TL, TL
    per_seq = sret0.shape[0] == B and B > 1
    st_block = (BB, N_HEADS, D_HEAD, D_HEAD)
    kern = functools.partial(_mixer0_kernel, BB=BB, TL=TL, cb=cb, tl=tl)
    return pl.pallas_call(
        kern, grid=(B // BB, L // TL),
        in_specs=[
            pl.BlockSpec((BB, TL, D_MODEL), lambda b, t: (b, t, 0)),
            pl.BlockSpec((TL, D_HEAD), lambda b, t: (t, 0)),
            pl.BlockSpec((TL, D_HEAD), lambda b, t: (t, 0)),
            _const_spec(w_in.shape), _const_spec(w_out.shape), _const_spec(lb_logits.shape),
            _const_spec(norm_w.shape), _const_spec(ln_g.shape), _const_spec(ln_b.shape),
            _state_spec(st_block, per_seq), _state_spec(st_block, per_seq),
        ],
        out_specs=[
            pl.BlockSpec((BB, TL, D_MODEL), lambda b, t: (b, t, 0)),
            _state_spec(st_block, True), _state_spec(st_block, True),
        ],
        out_shape=[
            jax.ShapeDtypeStruct((B, L, D_MODEL), f32),
            jax.ShapeDtypeStruct((B, N_HEADS, D_HEAD, D_HEAD), f32),
            jax.ShapeDtypeStruct((B, N_HEADS, D_HEAD, D_HEAD), f32),
        ],
        scratch_shapes=[
            pltpu.VMEM((N_HEADS, T, T), f32),
            pltpu.VMEM((N_HEADS, T, D_HEAD), f32),
            pltpu.VMEM((N_HEADS, T, D_HEAD), f32),
            pltpu.VMEM((T, T), bf16),
            pltpu.VMEM((T, T), jnp.int32),
        ],
        compiler_params=_params(), name=f"mixer0_b{BB}_t{TL}",
    )(x, c2, s2, w_in, w_out, lb_logits, norm_w, ln_g, ln_b, sret0, shg0)


def _ffn_kernel(x_ref, wup_ref, cw_ref, cb_ref, wdn_ref, g_ref, be_ref, buf0_ref,
                y_ref, buf_ref, u_ref, *, BB, TL):
    R = BB * TL
    K = CONV_W_FFN - 1

    @pl.when(pl.program_id(1) == 0)
    def _init_state():
        buf_ref[...] = buf0_ref[...]

    x2 = x_ref[...].reshape(R, D_MODEL)
    xb = x2.astype(bf16)
    u = _dot(xb, wup_ref[:, :D_FF])
    v = _dot(xb, wup_ref[:, D_FF:])
    u_ref[:, SUBLANES - K:SUBLANES, :] = buf_ref[...]
    u_ref[:, SUBLANES:, :] = u.reshape(BB, TL, D_FF)
    cw = cw_ref[...]
    uc = cb_ref[...] + u_ref[:, SUBLANES - K:SUBLANES - K + TL, :] * cw[0:1]
    for j in range(1, CONV_W_FFN):
        uc = uc + u_ref[:, SUBLANES - K + j:SUBLANES - K + j + TL, :] * cw[j:j + 1]
    buf_ref[...] = u_ref[:, SUBLANES + TL - K:SUBLANES + TL, :]
    hmid = _gelu(uc.reshape(R, D_FF)) * v
    fo = _dot(hmid.astype(bf16), wdn_ref[...])
    y = _layernorm_rows(DEEPNORM_ALPHA * x2 + fo, g_ref[...], be_ref[...])
    y_ref[...] = y.reshape(BB, TL, D_MODEL)


def _ffn(x, w_up, conv_w, conv_b, w_down, ln_g, ln_b, buf0, *, layer, BB, TL):
    B, L, _ = x.shape
    per_seq = buf0.shape[0] == B and B > 1
    K = CONV_W_FFN - 1
    kern = functools.partial(_ffn_kernel, BB=BB, TL=TL)
    layer_spec = lambda rows, cols: pl.BlockSpec((rows, cols), lambda b, t: (layer, 0), pipeline_mode=pl.Buffered(1))
    return pl.pallas_call(
        kern, grid=(B // BB, L // TL),
        in_specs=[
            pl.BlockSpec((BB, TL, D_MODEL), lambda b, t: (b, t, 0)),
            layer_spec(D_MODEL, 2 * D_FF), _const_spec(conv_w.shape), _const_spec(conv_b.shape),
            layer_spec(D_FF, D_MODEL), _const_spec(ln_g.shape), _const_spec(ln_b.shape),
            _state_spec((BB, K, D_FF), per_seq),
        ],
        out_specs=[
            pl.BlockSpec((BB, TL, D_MODEL), lambda b, t: (b, t, 0)),
            _state_spec((BB, K, D_FF), True),
        ],
        out_shape=[
            jax.ShapeDtypeStruct((B, L, D_MODEL), f32),
            jax.ShapeDtypeStruct((B, K, D_FF), f32),
        ],
        scratch_shapes=[pltpu.VMEM((BB, SUBLANES + TL, D_FF), f32)],
        compiler_params=_params(), name=f"ffn_b{BB}_t{TL}",
    )(x, w_up, conv_w, conv_b, w_down, ln_g, ln_b, buf0)


def _rglru_kernel(x_ref, win_ref, cw_ref, cb_ref, wga_ref, bga_ref, wgx_ref, bgx_ref, lam_ref,
                  wout_ref, g_ref, be_ref, h0_ref, buf0_ref,
                  y_ref, h_ref, buf_ref, *, BB, TL):
    R = BB * TL
    K = CONV_W_LRU - 1
    G = TL // SUBLANES

    @pl.when(pl.program_id(1) == 0)
    def _init_state():
        h_ref[...] = h0_ref[...]
        buf_ref[...] = buf0_ref[...]

    x2 = x_ref[...].reshape(R, D_MODEL)
    xb = x2.astype(bf16)
    W = LRU_STEP_COLS
    nchunk = D_RNN // W
    carry = buf_ref[...]
    new_carry = []
    sub = lax.broadcasted_iota(jnp.int32, (R // SUBLANES, SUBLANES, W), 1)
    h0 = h_ref[...]

    def project(ci):
        c0 = ci * W
        return _dot(xb, win_ref[:, c0:c0 + W]), _dot(xb, win_ref[:, D_RNN + c0:D_RNN + c0 + W])

    def recur(ci, gate, rnn):
        c0 = ci * W
        cs = slice(c0, c0 + W)
        rnn3 = rnn.reshape(BB, TL, W)
        xc = _causal_conv(rnn3, carry[:, :, cs], cw_ref, cb_ref, cs).reshape(R, W)
        new_carry.append(rnn3[:, TL - K:, :])
        xcb = xc.astype(bf16)
        blocks = [(j, c0 // LRU_BLOCK + j) for j in range(W // LRU_BLOCK)]
        ga = jnp.concatenate([_dot(xcb[:, j * LRU_BLOCK:(j + 1) * LRU_BLOCK], wga_ref[n]) for j, n in blocks], axis=1)
        gx = jnp.concatenate([_dot(xcb[:, j * LRU_BLOCK:(j + 1) * LRU_BLOCK], wgx_ref[n]) for j, n in blocks], axis=1)
        rgate = jax.nn.sigmoid(ga + bga_ref[:, cs])
        igate = jax.nn.sigmoid(gx + bgx_ref[:, cs])
        nla = rgate * (-LRU_C * jax.nn.log_sigmoid(lam_ref[:, cs]))
        a = jnp.exp(-nla)
        w1 = jnp.tanh(nla) * (1.0 + a * a)
        bv = jnp.where(w1 > 0.0, w1 * lax.rsqrt(w1), 0.0) * (igate * xc)
        a4 = a.reshape(R // SUBLANES, SUBLANES, W)
        b4 = bv.reshape(R // SUBLANES, SUBLANES, W)
        for d in (1, 2, 4):
            keep = sub >= d
            b4 = jnp.where(keep, a4 * pltpu.roll(b4, d, 1) + b4, b4)
            a4 = jnp.where(keep, a4 * pltpu.roll(a4, d, 1), a4)
        a5 = a4.reshape(BB, G, SUBLANES, W)
        b5 = b4.reshape(BB, G, SUBLANES, W)
        hc = h0[:, :, cs]
        tiles = []
        for gi in range(G):
            hg = a5[:, gi] * hc + b5[:, gi]
            hc = hg[:, SUBLANES - 1:SUBLANES, :]
            tiles.append(hg)
        hseq = tiles[0] if G == 1 else jnp.concatenate(tiles, axis=1)
        return (_gelu(gate) * hseq.reshape(R, W)).astype(bf16), hc

    nxt = project(0)
    ys, h_last = [], []
    for ci in range(nchunk):
        cur, nxt = nxt, (project(ci + 1) if ci + 1 < nchunk else None)
        yv, hc = recur(ci, *cur)
        ys.append(yv)
        h_last.append(hc)
    buf_ref[...] = jnp.concatenate(new_carry, axis=2)
    h_ref[...] = jnp.concatenate(h_last, axis=2)
    yv = jnp.concatenate(ys, axis=1)
    groups = LRU_OUT_ROW_GROUPS if R % (LRU_OUT_ROW_GROUPS * 2 * SUBLANES) == 0 else 1
    step = R // groups
    outs = [_layernorm_rows(DEEPNORM_ALPHA * x2[i * step:(i + 1) * step]
                            + _dot(yv[i * step:(i + 1) * step], wout_ref[...]), g_ref[...], be_ref[...])
            for i in range(groups)]
    y = outs[0] if groups == 1 else jnp.concatenate(outs, axis=0)
    y_ref[...] = y.reshape(BB, TL, D_MODEL)


def _rglru(x, w_in, conv_w, conv_b, wga, bga, wgx, bgx, lam, w_out, ln_g, ln_b, h0, buf0, *, BB, TL):
    B, L, _ = x.shape
    per_seq = h0.shape[0] == B and B > 1
    K = CONV_W_LRU - 1
    kern = functools.partial(_rglru_kernel, BB=BB, TL=TL)
    consts = (w_in, conv_w, conv_b, wga, bga, wgx, bgx, lam, w_out, ln_g, ln_b)
    return pl.pallas_call(
        kern, grid=(B // BB, L // TL),
        in_specs=[pl.BlockSpec((BB, TL, D_MODEL), lambda b, t: (b, t, 0))]
        + [_const_spec(c.shape) for c in consts]
        + [_state_spec((BB, 1, D_RNN), per_seq), _state_spec((BB, K, D_RNN), per_seq)],
        out_specs=[
            pl.BlockSpec((BB, TL, D_MODEL), lambda b, t: (b, t, 0)),
            _state_spec((BB, 1, D_RNN), True), _state_spec((BB, K, D_RNN), True),
        ],
        out_shape=[
            jax.ShapeDtypeStruct((B, L, D_MODEL), f32),
            jax.ShapeDtypeStruct((B, 1, D_RNN), f32),
            jax.ShapeDtypeStruct((B, K, D_RNN), f32),
        ],
        compiler_params=_params(), name=f"rglru_b{BB}_t{TL}",
    )(x, *consts, h0, buf0)


CAST_BLOCK_BYTES = 1 << 20


def _cast_kernel(*refs, starts, counts):
    n = len(starts)
    i = pl.program_id(0)
    for k in range(n):
        @pl.when((i >= starts[k]) & (i < starts[k] + counts[k]))
        def _(k=k):
            refs[n + k][...] = refs[k][...].astype(bf16)


def _cast_weights(arrays):
    mats = [a.reshape(-1, a.shape[-1]) for a in arrays]
    row_blocks, counts, starts = [], [], []
    total = 0
    for m in mats:
        nrows, ncols = m.shape
        rb = max(2 * SUBLANES, min(nrows, CAST_BLOCK_BYTES // (4 * ncols) // (2 * SUBLANES) * (2 * SUBLANES)))
        while nrows % rb:
            rb -= 2 * SUBLANES
        row_blocks.append(rb)
        counts.append(nrows // rb)
        starts.append(total)
        total += nrows // rb

    def spec(k):
        return pl.BlockSpec((row_blocks[k], mats[k].shape[1]),
                            lambda i, k=k: (jnp.clip(i - starts[k], 0, counts[k] - 1), 0))

    outs = pl.pallas_call(
        functools.partial(_cast_kernel, starts=tuple(starts), counts=tuple(counts)), grid=(total,),
        in_specs=[spec(k) for k in range(len(mats))], out_specs=[spec(k) for k in range(len(mats))],
        out_shape=[jax.ShapeDtypeStruct(m.shape, bf16) for m in mats],
        compiler_params=pltpu.CompilerParams(dimension_semantics=("arbitrary",)), name="cast_weights",
    )(*mats)
    return [o.reshape(a.shape) for o, a in zip(outs, arrays)]


def _rope_tables(pos0, length):
    half = D_HEAD // 2
    pos = pos0 + jnp.arange(length, dtype=jnp.int32)
    inv = ROPE_BASE ** (-jnp.arange(half, dtype=f32) / half)
    ang = pos.astype(f32)[:, None] * inv[None, :]
    cos, sin = jnp.cos(ang), jnp.sin(ang)
    return jnp.concatenate([cos, cos], axis=1), jnp.concatenate([-sin, sin], axis=1)


def _trunk(x, pos0, states, p, *, BB, TL, BB0):
    s_ret, s_hgrn, h_lru, buf_lru, buf_ffn0, buf_ffn1 = states
    c2, s2 = _rope_tables(pos0, x.shape[1])
    x, s_ret, s_hgrn = _mixer0(x, c2, s2, p['w_in_ab'], p['w_out_ab'], p['hgrn_lb_logits'], p['hgrn_norm_w'],
                               p['ln_mix_g'][0], p['ln_mix_b'][0], s_ret, s_hgrn, BB=BB0, TL=TL)
    x, buf_ffn0 = _ffn(x, p['w_ffn_up'], p['ffn_conv_w'][0], p['ffn_conv_b'][0], p['w_ffn_down'],
                       p['ln_ffn_g'][0], p['ln_ffn_b'][0], buf_ffn0, layer=0, BB=BB, TL=TL)
    x, h_lru, buf_lru = _rglru(x, p['w_in_c'], p['conv_w_c'], p['conv_b_c'], p['w_gate_a'], p['b_gate_a'],
                               p['w_gate_x'], p['b_gate_x'], p['lru_lambda'], p['w_out_c'],
                               p['ln_mix_g'][1], p['ln_mix_b'][1], h_lru, buf_lru, BB=BB, TL=TL)
    x, buf_ffn1 = _ffn(x, p['w_ffn_up'], p['ffn_conv_w'][1], p['ffn_conv_b'][1], p['w_ffn_down'],
                       p['ln_ffn_g'][1], p['ln_ffn_b'][1], buf_ffn1, layer=1, BB=BB, TL=TL)
    return x, (s_ret, s_hgrn, h_lru, buf_lru, buf_ffn0, buf_ffn1)


def kernel(x_prompt, x_sample, state_ret, state_hgrn, state_rglru_h, state_rglru_conv, state_ffn_conv, meta_tokens, w_in_ab, w_out_ab, hgrn_lb_logits, hgrn_norm_w, w_in_c, conv_w_c, conv_b_c, w_gate_a, b_gate_a, w_gate_x, b_gate_x, lru_lambda, w_out_c, ln_mix_g, ln_mix_b, ln_ffn_g, ln_ffn_b, w_ffn_up, ffn_conv_w, ffn_conv_b, w_ffn_down):
    row = lambda z: z.reshape(1, -1)
    wb = _cast_weights([w_in_ab, w_out_ab, w_in_c, w_gate_a, w_gate_x, w_out_c,
                        w_ffn_up.reshape(DEPTH * D_MODEL, 2 * D_FF), w_ffn_down.reshape(DEPTH * D_FF, D_MODEL)])
    p = dict(
        w_in_ab=wb[0], w_out_ab=wb[1],
        hgrn_lb_logits=hgrn_lb_logits, hgrn_norm_w=row(hgrn_norm_w),
        w_in_c=wb[2], conv_w_c=conv_w_c, conv_b_c=row(conv_b_c),
        w_gate_a=wb[3], b_gate_a=row(b_gate_a),
        w_gate_x=wb[4], b_gate_x=row(b_gate_x),
        lru_lambda=row(lru_lambda), w_out_c=wb[5],
        ln_mix_g=[row(ln_mix_g[i]) for i in range(DEPTH)], ln_mix_b=[row(ln_mix_b[i]) for i in range(DEPTH)],
        ln_ffn_g=[row(ln_ffn_g[i]) for i in range(DEPTH)], ln_ffn_b=[row(ln_ffn_b[i]) for i in range(DEPTH)],
        w_ffn_up=wb[6], w_ffn_down=wb[7],
        ffn_conv_w=[ffn_conv_w[i] for i in range(DEPTH)],
        ffn_conv_b=[row(ffn_conv_b[i]) for i in range(DEPTH)],
    )
    dt = x_prompt.dtype
    zero_states = (
        jnp.zeros((1, N_HEADS, D_HEAD, D_HEAD), dt), jnp.zeros((1, N_HEADS, D_HEAD, D_HEAD), dt),
        jnp.zeros((1, 1, D_RNN), dt), jnp.zeros((1, CONV_W_LRU - 1, D_RNN), dt),
        jnp.zeros((1, CONV_W_FFN - 1, D_FF), dt), jnp.zeros((1, CONV_W_FFN - 1, D_FF), dt),
    )
    _, meta_states = _trunk(meta_tokens.astype(dt)[None], 0, zero_states, p, BB=1, TL=N_META, BB0=1)
    y_prompt, ps = _trunk(x_prompt, N_META, meta_states, p, BB=1, TL=PROMPT_BLOCK_STEPS, BB0=1)
    sample_states = (state_ret, state_hgrn, state_rglru_h[:, None, :], state_rglru_conv,
                     state_ffn_conv[0], state_ffn_conv[1])
    y_sample, ss = _trunk(x_sample, PAST_LEN, sample_states, p, BB=32, TL=x_sample.shape[1], BB0=16)
    return (y_prompt, y_sample, ps[0], ss[0], ps[1], ss[1], ps[2][:, 0, :], ss[2][:, 0, :], ps[3], ss[3],
            jnp.stack([ps[4], ps[5]]), jnp.stack([ss[4], ss[5]]))
```

```python
import functools
import math

import jax
import jax.numpy as jnp
from jax import lax
from jax.experimental import pallas as pl
from jax.experimental.pallas import tpu as pltpu

f32 = jnp.float32
bf16 = jnp.bfloat16

D_MODEL = 1024
N_META = 16
PAST_LEN = 16384
N_HEADS = 4
D_HEAD = 128
SEG = N_HEADS * D_HEAD
ROPE_BASE = 10000.0
D_RNN = 1024
N_LRU_BLOCKS = 8
LRU_BLOCK = D_RNN // N_LRU_BLOCKS
CONV_W_LRU = 4
LRU_C = 8.0
D_FF = 2816
CONV_W_FFN = 3
LN_EPS = 1e-5
DEPTH = 2
DEEPNORM_ALPHA = (2.0 * DEPTH) ** 0.25
LOG_GAMMA = tuple(math.log1p(-(2.0 ** (-5.0 - h))) for h in range(N_HEADS))

SUBLANES = 8
CHUNK_ROWS = 128
LRU_STEP_COLS = 256
LRU_OUT_ROW_GROUPS = 2
PROMPT_BLOCK_STEPS = 512
LRU_TM_BLOCK_STEPS = 128
LRU_TM_PARTS = 4
VMEM_LIMIT_BYTES = 56 * 1024 * 1024

_GELU_K1 = -2.0 * math.log2(math.e) * math.sqrt(2.0 / math.pi)
_GELU_K3 = _GELU_K1 * 0.044715


def _gelu(x):
    return x / (1.0 + jnp.exp2(x * (_GELU_K1 + _GELU_K3 * (x * x))))


def _dot(a, b):
    return jnp.dot(a, b, preferred_element_type=f32)


def _dot_nt(a, b):
    return lax.dot_general(a, b, (((1,), (1,)), ((), ())), preferred_element_type=f32)


def _layernorm_rows(y, g, b):
    mu = jnp.mean(y, axis=-1, keepdims=True)
    yc = y - mu
    var = jnp.mean(yc * yc, axis=-1, keepdims=True)
    return yc * lax.rsqrt(var + LN_EPS) * g + b


def _causal_conv(x3, carry, w_ref, b_ref, cs):
    BB, TL, W = x3.shape
    K = carry.shape[1]
    w = [w_ref[j:j + 1, cs] for j in range(K + 1)]

    def shift(p, j):
        bnd = w[0] * carry[:, K - j:K - j + 1, :]
        for i in range(1, j):
            bnd = bnd + w[i] * carry[:, K - j + i:K - j + i + 1, :]
        if TL == SUBLANES:
            t = lax.broadcasted_iota(jnp.int32, p.shape, 1)
            return jnp.where(t == 0, bnd, pltpu.roll(p, 1, 1))
        assert BB == 1
        rolled = pltpu.roll(p.reshape(TL, W), 1, 0)
        t = lax.broadcasted_iota(jnp.int32, (SUBLANES, W), 0)
        head = jnp.where(t == 0, bnd.reshape(1, W), rolled[:SUBLANES])
        return jnp.concatenate([head, rolled[SUBLANES:]], axis=0).reshape(BB, TL, W)

    p = w[0] * x3
    for j in range(1, K + 1):
        p = w[j] * x3 + shift(p, j)
    return b_ref[:, cs] + p


def _const_spec(shape):
    nd = len(shape)
    return pl.BlockSpec(shape, lambda b, t: (0,) * nd, pipeline_mode=pl.Buffered(1))


def _state_spec(block, per_seq):
    nd = len(block)
    if per_seq:
        return pl.BlockSpec(block, lambda b, t: (b,) + (0,) * (nd - 1))
    return pl.BlockSpec(block, lambda b, t: (0,) * nd)


def _params():
    return pltpu.CompilerParams(dimension_semantics=("arbitrary", "arbitrary"),
                                vmem_limit_bytes=VMEM_LIMIT_BYTES)


def _mixer0_init_tables(dmask_ref, wq_ref, wk_ref, tri_ref, lvl_ref, *, cb, tl):
    T = cb * tl
    tl_shift = tl.bit_length() - 1
    r = lax.broadcasted_iota(jnp.int32, (T, T), 0)
    c = lax.broadcasted_iota(jnp.int32, (T, T), 1)
    same = (r >> tl_shift) == (c >> tl_shift)
    rel = (r & (tl - 1)) - (c & (tl - 1))
    causal = same & (rel >= 0)
    relf = jnp.maximum(rel, 0).astype(f32)
    tri_ref[...] = jnp.where(causal, 1.0, 0.0).astype(bf16)
    lvl = jnp.where(r == c, 0, -1)
    s, li = 1, 1
    while s < tl:
        blk = (r >> li) == (c >> li)
        hit = blk & ((r & (2 * s - 1)) >= s) & ((c & (2 * s - 1)) < s)
        lvl = jnp.where(hit, li, lvl)
        s, li = 2 * s, li + 1
    lvl_ref[...] = lvl
    tr = (lax.broadcasted_iota(jnp.int32, (T, D_HEAD), 0) & (tl - 1)).astype(f32)
    for h in range(N_HEADS):
        lg = LOG_GAMMA[h]
        dmask_ref[h] = jnp.where(causal, jnp.exp(lg * relf), 0.0)
        wq_ref[h] = jnp.exp(lg * (tr + 1.0))
        wk_ref[h] = jnp.exp(lg * ((tl - 1.0) - tr))


def _hgrn_scores(gqs, kks, fs, bs, lvl_ref, *, T, tl):
    n = len(gqs)
    row = lax.broadcasted_iota(jnp.int32, (T, D_HEAD), 0)
    ntile = T // SUBLANES

    def owned(li, g):
        return lvl_ref[g * SUBLANES:(g + 1) * SUBLANES, :] == li

    full = [_dot_nt(gqs[u].astype(bf16), kks[u].astype(bf16)) for u in range(n)]
    tiles = [[jnp.where(owned(0, g), full[u][g * SUBLANES:(g + 1) * SUBLANES], 0.0) for g in range(ntile)]
             for u in range(n)]

    def take(u, li, p, first_tile, n_tiles, p_row0):
        for g in range(n_tiles):
            t = first_tile + g
            tiles[u][t] = jnp.where(owned(li, t), p[p_row0 + g * SUBLANES:p_row0 + (g + 1) * SUBLANES], tiles[u][t])

    s, li = 1, 1
    while s < tl:
        nblk = T // (2 * s)
        if s < SUBLANES:
            up = (row & (2 * s - 1)) >= s
            zs = []
            for u in range(n):
                gq, kk, f, b = gqs[u], kks[u], fs[u], bs[u]
                if s == 1:
                    z = jnp.where(up, gq * f, kk)
                elif s == 2:
                    w = row & 3
                    e = jnp.where(w == 0, pltpu.roll(f, T - 1, 0),
                                  jnp.where(w == 1, 1.0, jnp.where(w == 2, f, f * pltpu.roll(f, 1, 0))))
                    z = jnp.where(up, gq, kk) * e
                else:
                    parts = [jnp.abs(b[m * 2 * s:(m + 1) * 2 * s, :] - b[m * 2 * s + s - 1:m * 2 * s + s, :])
                             for m in range(nblk)]
                    z = jnp.where(up, gq, kk) * jnp.exp2(-jnp.concatenate(parts, axis=0))
                zs.append(z.astype(bf16))
            ps = [_dot_nt(zb, zb) for zb in zs]
            for u in range(n):
                take(u, li, ps[u], 0, ntile, 0)
        else:
            zs, qus = [], []
            for u in range(n):
                gq, kk, b = gqs[u], kks[u], bs[u]
                both, upper = [], []
                for m in range(nblk):
                    r0 = m * 2 * s
                    beta = b[r0 + s - 1:r0 + s, :]
                    k_lo = kk[r0:r0 + s] * jnp.exp2(beta - b[r0:r0 + s])
                    q_hi = gq[r0 + s:r0 + 2 * s] * jnp.exp2(b[r0 + s:r0 + 2 * s] - beta)
                    both += [k_lo, q_hi]
                    upper.append(q_hi)
                zs.append(jnp.concatenate(both, axis=0).astype(bf16))
                qus.append((upper[0] if nblk == 1 else jnp.concatenate(upper, axis=0)).astype(bf16))
            ps = [_dot_nt(qus[u], zs[u]) for u in range(n)]
            for u in range(n):
                for m in range(nblk):
                    take(u, li, ps[u], (m * 2 * s + s) // SUBLANES, s // SUBLANES, m * s)
        s, li = 2 * s, li + 1
    return [jnp.concatenate(tiles[u], axis=0).astype(bf16) for u in range(n)]


def _mixer0_kernel(x_ref, c2_ref, s2_ref, win_ref, wout_ref, lbl_ref, nw_ref, g_ref, be_ref,
                   sret0_ref, shg0_ref, y_ref, sret_ref, shg_ref,
                   dmask_ref, wq_ref, wk_ref, tri_ref, lvl_ref, *, BB, TL, cb, tl):
    T = cb * tl
    R = BB * TL
    nch = R // T

    @pl.when((pl.program_id(0) == 0) & (pl.program_id(1) == 0))
    def _init_tables():
        _mixer0_init_tables(dmask_ref, wq_ref, wk_ref, tri_ref, lvl_ref, cb=cb, tl=tl)

    @pl.when(pl.program_id(1) == 0)
    def _init_state():
        sret_ref[...] = sret0_ref[...]
        shg_ref[...] = shg0_ref[...]

    x2 = x_ref[...].reshape(R, D_MODEL)
    xb = x2.astype(bf16)

    def proj(i):
        return _dot(xb, win_ref[:, i * SEG:(i + 1) * SEG])

    lbl = lbl_ref[...]
    le = jnp.exp(lbl - jnp.max(lbl, axis=0, keepdims=True))
    lb = le[0:1] / jnp.sum(le, axis=0, keepdims=True)
    nw = nw_ref[...]

    def seqs(z):
        return z.reshape(cb, tl, D_HEAD)

    def rows(z3):
        return z3.reshape(T, D_HEAD)

    def qs(q3, s3):
        if cb == 1:
            return _dot(q3[0], s3[0])[None]
        return jnp.einsum('bqd,bdv->bqv', q3, s3, preferred_element_type=f32)

    def ktv(k3, v3):
        if cb == 1:
            return lax.dot_general(k3[0], v3[0], (((0,), (0,)), ((), ())),
                                   preferred_element_type=f32)[None]
        return jnp.einsum('btd,btv->bdv', k3, v3, preferred_element_type=f32)

    units = [(c, h) for c in range(nch) for h in range(N_HEADS)]
    nu = len(units)

    def sl(z, c, h):
        return z[c * T:(c + 1) * T, h * D_HEAD:(h + 1) * D_HEAD]

    def seq0(c):
        return 0 if BB == 1 else c * cb

    def rope_tables(c):
        ts = slice(c * tl, (c + 1) * tl) if BB == 1 else slice(0, tl)
        return c2_ref[ts, :][None], s2_ref[ts, :][None]

    def rope(z, cos2, sin2):
        return rows(seqs(z) * cos2 + seqs(pltpu.roll(z, D_HEAD // 2, 1)) * sin2)

    tabs = [rope_tables(c) for c in range(nch)]
    ktabs = [(c2 * (D_HEAD ** -0.5), s2 * (D_HEAD ** -0.5)) for c2, s2 in tabs]

    rq, rk = proj(0), proj(1)
    q = [rope(sl(rq, c, h), *tabs[c]) for c, h in units]
    k = [rope(sl(rk, c, h), *ktabs[c]) for c, h in units]
    rv = proj(2)
    v = [sl(rv, c, h).astype(bf16) for c, h in units]
    att = [_dot_nt(q[u].astype(bf16), k[u].astype(bf16)) for u in range(nu)]
    att = [(att[u] * dmask_ref[h]).astype(bf16) for u, (c, h) in enumerate(units)]
    o_ret = [_dot(att[u], v[u]) for u in range(nu)]
    qw = [seqs((q[u] * wq_ref[h]).astype(bf16)) for u, (c, h) in enumerate(units)]
    upd_ret = [ktv(seqs((k[u] * wk_ref[h]).astype(bf16)), seqs(v[u])) for u, (c, h) in enumerate(units)]

    hf = proj(5)
    lbs = [lb[:, h * D_HEAD:(h + 1) * D_HEAD] for h in range(N_HEADS)]
    f = [lbs[h] + (1.0 - lbs[h]) * jax.nn.sigmoid(sl(hf, c, h)) for c, h in units]
    lf = [jnp.log2(z) for z in f]
    kk = [1.0 - z for z in f]
    hq, hi = proj(4), proj(6)
    gq = [sl(hq, c, h) for c, h in units]
    gv = [sl(hi, c, h).astype(bf16) for c, h in units]
    split = []
    for z in lf:
        l1 = z.astype(bf16)
        r1 = z - l1.astype(f32)
        l2 = r1.astype(bf16)
        l3 = (r1 - l2.astype(f32)).astype(bf16)
        split.append(jnp.concatenate([l1, l2, l3], axis=1))
    cs = [_dot(tri_ref[...], z) for z in split]
    b = [z[:, :D_HEAD] + z[:, D_HEAD:2 * D_HEAD] + z[:, 2 * D_HEAD:] for z in cs]
    scores = _hgrn_scores(gq, kk, f, b, lvl_ref, T=T, tl=tl)
    o_hg = [_dot(scores[u], gv[u]) for u in range(nu)]
    qe = [seqs((gq[u] * jnp.exp2(b[u])).astype(bf16)) for u in range(nu)]
    b3 = [seqs(z) for z in b]
    bl = [z[:, tl - 1:tl, :] for z in b3]
    upd_hg = [ktv((seqs(kk[u]) * jnp.exp2(bl[u] - b3[u])).astype(bf16), seqs(gv[u])) for u in range(nu)]
    decay = []
    for z in bl:
        ez = jnp.exp2(z)
        d = [jnp.broadcast_to(ez[j], (D_HEAD, D_HEAD)).T for j in range(cb)]
        decay.append(d[0][None] if cb == 1 else jnp.stack(d))

    s_ret, s_hg = {}, {}
    for u, (c, h) in enumerate(units):
        s0 = seq0(c)
        first = BB > 1 or c == 0
        sp = sret_ref[s0:s0 + cb, h] if first else s_ret[h]
        sg = shg_ref[s0:s0 + cb, h] if first else s_hg[h]
        o_ret[u] = o_ret[u] + rows(qs(qw[u], sp.astype(bf16)))
        o_hg[u] = o_hg[u] + rows(qs(qe[u], sg.astype(bf16)))
        s_ret[h] = math.exp(LOG_GAMMA[h] * tl) * sp + upd_ret[u]
        s_hg[h] = decay[u] * sg + upd_hg[u]
        if BB > 1 or c == nch - 1:
            sret_ref[s0:s0 + cb, h] = s_ret[h]
            shg_ref[s0:s0 + cb, h] = s_hg[h]

    rg, hg = proj(3), proj(7)
    head_out = {}
    for u, (c, h) in enumerate(units):
        o = o_ret[u]
        mu = jnp.mean(o, axis=-1, keepdims=True)
        oc = o - mu
        var = jnp.mean(oc * oc, axis=-1, keepdims=True)
        head_out[(c, h)] = oc * lax.rsqrt(var + LN_EPS) * jax.nn.silu(sl(rg, c, h))
        og = o_hg[u]
        ms = jnp.mean(og * og, axis=-1, keepdims=True)
        head_out[(c, N_HEADS + h)] = og * lax.rsqrt(ms + LN_EPS) * nw * jax.nn.silu(sl(hg, c, h))
    chunk_out = [jnp.concatenate([head_out[(c, j)] for j in range(2 * N_HEADS)], axis=1).astype(bf16)
                 for c in range(nch)]
    mixed = chunk_out[0] if nch == 1 else jnp.concatenate(chunk_out, axis=0)
    m = _dot(mixed, wout_ref[...])
    y = _layernorm_rows(DEEPNORM_ALPHA * x2 + m, g_ref[...], be_ref[...])
    y_ref[...] = y.reshape(BB, TL, D_MODEL)


def _mixer0(x, c2, s2, w_in, w_out, lb_logits, norm_w, ln_g, ln_b, sret0, shg0, *, BB, TL):
    B, L, _ = x.shape
    R = BB * TL
    T = min(R, CHUNK_ROWS)
    if BB == 1:
        cb, tl = 1, T
    else:
        assert T % TL == 0
        cb, tl = T // TL, TL
    per_seq = sret0.shape[0] == B and B > 1
    st_block = (BB, N_HEADS, D_HEAD, D_HEAD)
    kern = functools.partial(_mixer0_kernel, BB=BB, TL=TL, cb=cb, tl=tl)
    return pl.pallas_call(
        kern, grid=(B // BB, L // TL),
        in_specs=[
            pl.BlockSpec((BB, TL, D_MODEL), lambda b, t: (b, t, 0)),
            pl.BlockSpec((TL, D_HEAD), lambda b, t: (t, 0)),
            pl.BlockSpec((TL, D_HEAD), lambda b, t: (t, 0)),
            _const_spec(w_in.shape), _const_spec(w_out.shape), _const_spec(lb_logits.shape),
            _const_spec(norm_w.shape), _const_spec(ln_g.shape), _const_spec(ln_b.shape),
            _state_spec(st_block, per_seq), _state_spec(st_block, per_seq),
        ],
        out_specs=[
            pl.BlockSpec((BB, TL, D_MODEL), lambda b, t: (b, t, 0)),
            _state_spec(st_block, True), _state_spec(st_block, True),
        ],
        out_shape=[
            jax.ShapeDtypeStruct((B, L, D_MODEL), f32),
            jax.ShapeDtypeStruct((B, N_HEADS, D_HEAD, D_HEAD), f32),
            jax.ShapeDtypeStruct((B, N_HEADS, D_HEAD, D_HEAD), f32),
        ],
        scratch_shapes=[
            pltpu.VMEM((N_HEADS, T, T), f32),
            pltpu.VMEM((N_HEADS, T, D_HEAD), f32),
            pltpu.VMEM((N_HEADS, T, D_HEAD), f32),
            pltpu.VMEM((T, T), bf16),
            pltpu.VMEM((T, T), jnp.int32),
        ],
        compiler_params=_params(), name=f"mixer0_b{BB}_t{TL}",
    )(x, c2, s2, w_in, w_out, lb_logits, norm_w, ln_g, ln_b, sret0, shg0)


def _ffn_kernel(x_ref, wup_ref, cw_ref, cb_ref, wdn_ref, g_ref, be_ref, buf0_ref,
                y_ref, buf_ref, u_ref, *, BB, TL):
    R = BB * TL
    K = CONV_W_FFN - 1

    @pl.when(pl.program_id(1) == 0)
    def _init_state():
        buf_ref[...] = buf0_ref[...]

    x2 = x_ref[...].reshape(R, D_MODEL)
    xb = x2.astype(bf16)
    u = _dot(xb, wup_ref[:, :D_FF])
    v = _dot(xb, wup_ref[:, D_FF:])
    u_ref[:, SUBLANES - K:SUBLANES, :] = buf_ref[...]
    u_ref[:, SUBLANES:, :] = u.reshape(BB, TL, D_FF)
    cw = cw_ref[...]
    uc = cb_ref[...] + u_ref[:, SUBLANES - K:SUBLANES - K + TL, :] * cw[0:1]
    for j in range(1, CONV_W_FFN):
        uc = uc + u_ref[:, SUBLANES - K + j:SUBLANES - K + j + TL, :] * cw[j:j + 1]
    buf_ref[...] = u_ref[:, SUBLANES + TL - K:SUBLANES + TL, :]
    hmid = _gelu(uc.reshape(R, D_FF)) * v
    fo = _dot(hmid.astype(bf16), wdn_ref[...])
    y = _layernorm_rows(DEEPNORM_ALPHA * x2 + fo, g_ref[...], be_ref[...])
    y_ref[...] = y.reshape(BB, TL, D_MODEL)


def _ffn(x, w_up, conv_w, conv_b, w_down, ln_g, ln_b, buf0, *, layer, BB, TL):
    B, L, _ = x.shape
    per_seq = buf0.shape[0] == B and B > 1
    K = CONV_W_FFN - 1
    kern = functools.partial(_ffn_kernel, BB=BB, TL=TL)
    layer_spec = lambda rows, cols: pl.BlockSpec((rows, cols), lambda b, t: (layer, 0), pipeline_mode=pl.Buffered(1))
    return pl.pallas_call(
        kern, grid=(B // BB, L // TL),
        in_specs=[
            pl.BlockSpec((BB, TL, D_MODEL), lambda b, t: (b, t, 0)),
            layer_spec(D_MODEL, 2 * D_FF), _const_spec(conv_w.shape), _const_spec(conv_b.shape),
            layer_spec(D_FF, D_MODEL), _const_spec(ln_g.shape), _const_spec(ln_b.shape),
            _state_spec((BB, K, D_FF), per_seq),
        ],
        out_specs=[
            pl.BlockSpec((BB, TL, D_MODEL), lambda b, t: (b, t, 0)),
            _state_spec((BB, K, D_FF), True),
        ],
        out_shape=[
            jax.ShapeDtypeStruct((B, L, D_MODEL), f32),
            jax.ShapeDtypeStruct((B, K, D_FF), f32),
        ],
        scratch_shapes=[pltpu.VMEM((BB, SUBLANES + TL, D_FF), f32)],
        compiler_params=_params(), name=f"ffn_b{BB}_t{TL}",
    )(x, w_up, conv_w, conv_b, w_down, ln_g, ln_b, buf0)


def _rglru_kernel(x_ref, win_ref, cw_ref, cb_ref, wga_ref, bga_ref, wgx_ref, bgx_ref, lam_ref,
                  wout_ref, g_ref, be_ref, h0_ref, buf0_ref,
                  y_ref, h_ref, buf_ref, *, BB, TL):
    R = BB * TL
    K = CONV_W_LRU - 1
    G = TL // SUBLANES

    @pl.when(pl.program_id(1) == 0)
    def _init_state():
        h_ref[...] = h0_ref[...]
        buf_ref[...] = buf0_ref[...]

    x2 = x_ref[...].reshape(R, D_MODEL)
    xb = x2.astype(bf16)
    W = LRU_STEP_COLS
    nchunk = D_RNN // W
    carry = buf_ref[...]
    new_carry = []
    sub = lax.broadcasted_iota(jnp.int32, (R // SUBLANES, SUBLANES, W), 1)
    h0 = h_ref[...]

    def project(ci):
        c0 = ci * W
        return _dot(xb, win_ref[:, c0:c0 + W]), _dot(xb, win_ref[:, D_RNN + c0:D_RNN + c0 + W])

    def recur(ci, gate, rnn):
        c0 = ci * W
        cs = slice(c0, c0 + W)
        rnn3 = rnn.reshape(BB, TL, W)
        xc = _causal_conv(rnn3, carry[:, :, cs], cw_ref, cb_ref, cs).reshape(R, W)
        new_carry.append(rnn3[:, TL - K:, :])
        xcb = xc.astype(bf16)
        blocks = [(j, c0 // LRU_BLOCK + j) for j in range(W // LRU_BLOCK)]
        ga = jnp.concatenate([_dot(xcb[:, j * LRU_BLOCK:(j + 1) * LRU_BLOCK], wga_ref[n]) for j, n in blocks], axis=1)
        gx = jnp.concatenate([_dot(xcb[:, j * LRU_BLOCK:(j + 1) * LRU_BLOCK], wgx_ref[n]) for j, n in blocks], axis=1)
        rgate = jax.nn.sigmoid(ga + bga_ref[:, cs])
        igate = jax.nn.sigmoid(gx + bgx_ref[:, cs])
        nla = rgate * (-LRU_C * jax.nn.log_sigmoid(lam_ref[:, cs]))
        a = jnp.exp(-nla)
        w1 = jnp.tanh(nla) * (1.0 + a * a)
        bv = jnp.where(w1 > 0.0, w1 * lax.rsqrt(w1), 0.0) * (igate * xc)
        a4 = a.reshape(R // SUBLANES, SUBLANES, W)
        b4 = bv.reshape(R // SUBLANES, SUBLANES, W)
        for d in (1, 2, 4):
            keep = sub >= d
            b4 = jnp.where(keep, a4 * pltpu.roll(b4, d, 1) + b4, b4)
            a4 = jnp.where(keep, a4 * pltpu.roll(a4, d, 1), a4)
        a5 = a4.reshape(BB, G, SUBLANES, W)
        b5 = b4.reshape(BB, G, SUBLANES, W)
        hc = h0[:, :, cs]
        tiles = []
        for gi in range(G):
            hg = a5[:, gi] * hc + b5[:, gi]
            hc = hg[:, SUBLANES - 1:SUBLANES, :]
            tiles.append(hg)
        hseq = tiles[0] if G == 1 else jnp.concatenate(tiles, axis=1)
        return (_gelu(gate) * hseq.reshape(R, W)).astype(bf16), hc

    nxt = project(0)
    ys, h_last = [], []
    for ci in range(nchunk):
        cur, nxt = nxt, (project(ci + 1) if ci + 1 < nchunk else None)
        yv, hc = recur(ci, *cur)
        ys.append(yv)
        h_last.append(hc)
    buf_ref[...] = jnp.concatenate(new_carry, axis=2)
    h_ref[...] = jnp.concatenate(h_last, axis=2)
    yv = jnp.concatenate(ys, axis=1)
    groups = LRU_OUT_ROW_GROUPS if R % (LRU_OUT_ROW_GROUPS * 2 * SUBLANES) == 0 else 1
    step = R // groups
    outs = [_layernorm_rows(DEEPNORM_ALPHA * x2[i * step:(i + 1) * step]
                            + _dot(yv[i * step:(i + 1) * step], wout_ref[...]), g_ref[...], be_ref[...])
            for i in range(groups)]
    y = outs[0] if groups == 1 else jnp.concatenate(outs, axis=0)
    y_ref[...] = y.reshape(BB, TL, D_MODEL)


def _rglru(x, w_in, conv_w, conv_b, wga, bga, wgx, bgx, lam, w_out, ln_g, ln_b, h0, buf0, *, BB, TL):
    B, L, _ = x.shape
    per_seq = h0.shape[0] == B and B > 1
    K = CONV_W_LRU - 1
    kern = functools.partial(_rglru_kernel, BB=BB, TL=TL)
    consts = (w_in, conv_w, conv_b, wga, bga, wgx, bgx, lam, w_out, ln_g, ln_b)
    return pl.pallas_call(
        kern, grid=(B // BB, L // TL),
        in_specs=[pl.BlockSpec((BB, TL, D_MODEL), lambda b, t: (b, t, 0))]
        + [_const_spec(c.shape) for c in consts]
        + [_state_spec((BB, 1, D_RNN), per_seq), _state_spec((BB, K, D_RNN), per_seq)],
        out_specs=[
            pl.BlockSpec((BB, TL, D_MODEL), lambda b, t: (b, t, 0)),
            _state_spec((BB, 1, D_RNN), True), _state_spec((BB, K, D_RNN), True),
        ],
        out_shape=[
            jax.ShapeDtypeStruct((B, L, D_MODEL), f32),
            jax.ShapeDtypeStruct((B, 1, D_RNN), f32),
            jax.ShapeDtypeStruct((B, K, D_RNN), f32),
        ],
        compiler_params=_params(), name=f"rglru_b{BB}_t{TL}",
    )(x, *consts, h0, buf0)


def _rglru_tm_kernel(x_ref, win_ref, cw_ref, cb_ref, wga_ref, bga_ref, wgx_ref, bgx_ref, lam_ref,
                     wout_ref, g_ref, be_ref, h0_ref, buf0_ref,
                     y_ref, h_ref, buf_ref, perm_ref, permt_ref, hrun_ref, crun_ref, *, TL):
    B = SUBLANES
    NPART = LRU_TM_PARTS
    TP = TL // NPART
    RP = B * TP
    K = CONV_W_LRU - 1
    W = LRU_STEP_COLS
    nchunk = D_RNN // W
    step = pl.program_id(0)

    @pl.when(step == 0)
    def _init():
        r = lax.broadcasted_iota(jnp.int32, (RP, RP), 0)
        c = lax.broadcasted_iota(jnp.int32, (RP, RP), 1)
        hit = ((r & (B - 1)) * TP + (r >> 3)) == c
        perm_ref[...] = jnp.where(hit, 1.0, 0.0).astype(bf16)
        hit_t = ((c & (B - 1)) * TP + (c >> 3)) == r
        permt_ref[...] = jnp.where(hit_t, 1.0, 0.0).astype(bf16)
        hrun_ref[...] = jnp.broadcast_to(h0_ref[0], (B, D_RNN))
        for k in range(K):
            crun_ref[k] = jnp.broadcast_to(buf0_ref[0, k:k + 1, :], (B, D_RNN))

    def head(p):
        x2 = x_ref[:, p * TP:(p + 1) * TP, :].reshape(RP, D_MODEL)
        return x2, _dot(perm_ref[...], x2.astype(bf16)).astype(bf16)

    def project(xt, ci):
        c0 = ci * W
        return _dot(xt, win_ref[:, c0:c0 + W]), _dot(xt, win_ref[:, D_RNN + c0:D_RNN + c0 + W])

    def body(ci, gate, rnn):
        c0 = ci * W
        cs = slice(c0, c0 + W)
        rnn = rnn.reshape(TP, B, W)
        ext = jnp.concatenate([crun_ref[:, :, cs], rnn], axis=0)
        xc = cb_ref[:, cs] + ext[0:TP] * cw_ref[0:1, cs]
        for j in range(1, CONV_W_LRU):
            xc = xc + ext[j:j + TP] * cw_ref[j:j + 1, cs]
        crun_ref[:, :, cs] = rnn[TP - K:]
        xc = xc.reshape(RP, W)
        xcb = xc.astype(bf16)
        blocks = [(j, c0 // LRU_BLOCK + j) for j in range(W // LRU_BLOCK)]
        ga = jnp.concatenate([_dot(xcb[:, j * LRU_BLOCK:(j + 1) * LRU_BLOCK], wga_ref[n]) for j, n in blocks], axis=1)
        gx = jnp.concatenate([_dot(xcb[:, j * LRU_BLOCK:(j + 1) * LRU_BLOCK], wgx_ref[n]) for j, n in blocks], axis=1)
        rgate = jax.nn.sigmoid(ga + bga_ref[:, cs])
        igate = jax.nn.sigmoid(gx + bgx_ref[:, cs])
        log_a = LRU_C * rgate * jax.nn.log_sigmoid(lam_ref[:, cs])
        a = jnp.exp(log_a).reshape(TP, B, W)
        th = jnp.tanh(log_a)
        bv = (jnp.sqrt(-2.0 * th / (1.0 - th)) * (igate * xc)).reshape(TP, B, W)
        h = hrun_ref[:, cs]
        hs = []
        for t in range(TP):
            h = a[t] * h + bv[t]
            hs.append(h)
        hrun_ref[:, cs] = h
        return (_gelu(gate) * jnp.stack(hs, axis=0).reshape(RP, W)).astype(bf16)

    def tail(p, x2, ys):
        yv = _dot(permt_ref[...], jnp.concatenate(ys, axis=1)).astype(bf16)
        y = _layernorm_rows(DEEPNORM_ALPHA * x2 + _dot(yv, wout_ref[...]), g_ref[...], be_ref[...])
        y_ref[:, p * TP:(p + 1) * TP, :] = y.reshape(B, TP, D_MODEL)

    x2s, xts, nxt = {}, {}, None
    x2s[0], xts[0] = head(0)
    nxt = project(xts[0], 0)
    for p in range(NPART):
        if p + 1 < NPART:
            x2s[p + 1], xts[p + 1] = head(p + 1)
        ys = []
        for ci in range(nchunk):
            cur = nxt
            if ci + 1 < nchunk:
                nxt = project(xts[p], ci + 1)
            elif p + 1 < NPART:
                nxt = project(xts[p + 1], 0)
            ys.append(body(ci, *cur))
        tail(p, x2s[p], ys)

    @pl.when(step == pl.num_programs(0) - 1)
    def _emit_state():
        h_ref[:, 0, :] = hrun_ref[...]
        for k in range(K):
            buf_ref[:, k, :] = crun_ref[k]


def _rglru_tm(x, w_in, conv_w, conv_b, wga, bga, wgx, bgx, lam, w_out, ln_g, ln_b, h0, buf0, *, TL):
    B, L, _ = x.shape
    assert B == SUBLANES and h0.shape[0] == 1 and buf0.shape[0] == 1
    K = CONV_W_LRU - 1
    R = B * TL // LRU_TM_PARTS
    consts = (w_in, conv_w, conv_b, wga, bga, wgx, bgx, lam, w_out, ln_g, ln_b, h0, buf0)
    const_spec = lambda shape: pl.BlockSpec(shape, lambda t: (0,) * len(shape), pipeline_mode=pl.Buffered(1))
    return pl.pallas_call(
        functools.partial(_rglru_tm_kernel, TL=TL), grid=(L // TL,),
        in_specs=[pl.BlockSpec((B, TL, D_MODEL), lambda t: (0, t, 0))] + [const_spec(c.shape) for c in consts],
        out_specs=[
            pl.BlockSpec((B, TL, D_MODEL), lambda t: (0, t, 0)),
            pl.BlockSpec((B, 1, D_RNN), lambda t: (0, 0, 0)), pl.BlockSpec((B, K, D_RNN), lambda t: (0, 0, 0)),
        ],
        out_shape=[
            jax.ShapeDtypeStruct((B, L, D_MODEL), f32),
            jax.ShapeDtypeStruct((B, 1, D_RNN), f32),
            jax.ShapeDtypeStruct((B, K, D_RNN), f32),
        ],
        scratch_shapes=[
            pltpu.VMEM((R, R), bf16), pltpu.VMEM((R, R), bf16),
            pltpu.VMEM((B, D_RNN), f32),
            pltpu.VMEM((K, B, D_RNN), f32),
        ],
        compiler_params=pltpu.CompilerParams(dimension_semantics=("arbitrary",), vmem_limit_bytes=VMEM_LIMIT_BYTES),
        name=f"rglru_tm_t{TL}",
    )(x, *consts)


def _rope_tables(pos0, length):
    half = D_HEAD // 2
    pos = pos0 + jnp.arange(length, dtype=jnp.int32)
    inv = ROPE_BASE ** (-jnp.arange(half, dtype=f32) / half)
    ang = pos.astype(f32)[:, None] * inv[None, :]
    cos, sin = jnp.cos(ang), jnp.sin(ang)
    return jnp.concatenate([cos, cos], axis=1), jnp.concatenate([-sin, sin], axis=1)


def _trunk(x, pos0, states, p, *, BB, TL, BB0, lru_tm_steps=None):
    s_ret, s_hgrn, h_lru, buf_lru, buf_ffn0, buf_ffn1 = states
    c2, s2 = _rope_tables(pos0, x.shape[1])
    x, s_ret, s_hgrn = _mixer0(x, c2, s2, p['w_in_ab'], p['w_out_ab'], p['hgrn_lb_logits'], p['hgrn_norm_w'],
                               p['ln_mix_g'][0], p['ln_mix_b'][0], s_ret, s_hgrn, BB=BB0, TL=TL)
    x, buf_ffn0 = _ffn(x, p['w_ffn_up'], p['ffn_conv_w'][0], p['ffn_conv_b'][0], p['w_ffn_down'],
                       p['ln_ffn_g'][0], p['ln_ffn_b'][0], buf_ffn0, layer=0, BB=BB, TL=TL)
    lru_args = (x, p['w_in_c'], p['conv_w_c'], p['conv_b_c'], p['w_gate_a'], p['b_gate_a'], p['w_gate_x'],
                p['b_gate_x'], p['lru_lambda'], p['w_out_c'], p['ln_mix_g'][1], p['ln_mix_b'][1], h_lru, buf_lru)
    if lru_tm_steps is None:
        x, h_lru, buf_lru = _rglru(*lru_args, BB=BB, TL=TL)
    else:
        x, h_lru, buf_lru = _rglru_tm(*lru_args, TL=lru_tm_steps)
    x, buf_ffn1 = _ffn(x, p['w_ffn_up'], p['ffn_conv_w'][1], p['ffn_conv_b'][1], p['w_ffn_down'],
                       p['ln_ffn_g'][1], p['ln_ffn_b'][1], buf_ffn1, layer=1, BB=BB, TL=TL)
    return x, (s_ret, s_hgrn, h_lru, buf_lru, buf_ffn0, buf_ffn1)


def kernel(x_prompt, x_sample, state_ret, state_hgrn, state_rglru_h, state_rglru_conv, state_ffn_conv, meta_tokens, w_in_ab, w_out_ab, hgrn_lb_logits, hgrn_norm_w, w_in_c, conv_w_c, conv_b_c, w_gate_a, b_gate_a, w_gate_x, b_gate_x, lru_lambda, w_out_c, ln_mix_g, ln_mix_b, ln_ffn_g, ln_ffn_b, w_ffn_up, ffn_conv_w, ffn_conv_b, w_ffn_down):
    row = lambda z: z.reshape(1, -1)
    wb = [w.astype(bf16) for w in (w_in_ab, w_out_ab, w_in_c, w_gate_a, w_gate_x, w_out_c,
                                   w_ffn_up.reshape(DEPTH * D_MODEL, 2 * D_FF),
                                   w_ffn_down.reshape(DEPTH * D_FF, D_MODEL))]
    p = dict(
        w_in_ab=wb[0], w_out_ab=wb[1],
        hgrn_lb_logits=hgrn_lb_logits, hgrn_norm_w=row(hgrn_norm_w),
        w_in_c=wb[2], conv_w_c=conv_w_c, conv_b_c=row(conv_b_c),
        w_gate_a=wb[3], b_gate_a=row(b_gate_a),
        w_gate_x=wb[4], b_gate_x=row(b_gate_x),
        lru_lambda=row(lru_lambda), w_out_c=wb[5],
        ln_mix_g=[row(ln_mix_g[i]) for i in range(DEPTH)], ln_mix_b=[row(ln_mix_b[i]) for i in range(DEPTH)],
        ln_ffn_g=[row(ln_ffn_g[i]) for i in range(DEPTH)], ln_ffn_b=[row(ln_ffn_b[i]) for i in range(DEPTH)],
        w_ffn_up=wb[6], w_ffn_down=wb[7],
        ffn_conv_w=[ffn_conv_w[i] for i in range(DEPTH)],
        ffn_conv_b=[row(ffn_conv_b[i]) for i in range(DEPTH)],
    )
    dt = x_prompt.dtype
    zero_states = (
        jnp.zeros((1, N_HEADS, D_HEAD, D_HEAD), dt), jnp.zeros((1, N_HEADS, D_HEAD, D_HEAD), dt),
        jnp.zeros((1, 1, D_RNN), dt), jnp.zeros((1, CONV_W_LRU - 1, D_RNN), dt),
        jnp.zeros((1, CONV_W_FFN - 1, D_FF), dt), jnp.zeros((1, CONV_W_FFN - 1, D_FF), dt),
    )
    _, meta_states = _trunk(meta_tokens.astype(dt)[None], 0, zero_states, p, BB=1, TL=N_META, BB0=1)
    y_prompt, ps = _trunk(x_prompt, N_META, meta_states, p, BB=1, TL=PROMPT_BLOCK_STEPS, BB0=1,
                          lru_tm_steps=LRU_TM_BLOCK_STEPS)
    sample_states = (state_ret, state_hgrn, state_rglru_h[:, None, :], state_rglru_conv,
                     state_ffn_conv[0], state_ffn_conv[1])
    y_sample, ss = _trunk(x_sample, PAST_LEN, sample_states, p, BB=32, TL=x_sample.shape[1], BB0=16)
    return (y_prompt, y_sample, ps[0], ss[0], ps[1], ss[1], ps[2][:, 0, :], ss[2][:, 0, :], ps[3], ss[3],
            jnp.stack([ps[4], ps[5]]), jnp.stack([ss[4], ss[5]]))
```

```python
import functools
import math

import jax
import jax.numpy as jnp
from jax import lax
from jax.experimental import pallas as pl
from jax.experimental.pallas import tpu as pltpu

f32 = jnp.float32
bf16 = jnp.bfloat16

D_MODEL = 1024
N_META = 16
PAST_LEN = 16384
N_HEADS = 4
D_HEAD = 128
SEG = N_HEADS * D_HEAD
ROPE_BASE = 10000.0
D_RNN = 1024
N_LRU_BLOCKS = 8
LRU_BLOCK = D_RNN // N_LRU_BLOCKS
CONV_W_LRU = 4
LRU_C = 8.0
D_FF = 2816
CONV_W_FFN = 3
LN_EPS = 1e-5
DEPTH = 2
DEEPNORM_ALPHA = (2.0 * DEPTH) ** 0.25
LOG_GAMMA = tuple(math.log1p(-(2.0 ** (-5.0 - h))) for h in range(N_HEADS))

SUBLANES = 8
CHUNK_ROWS = 128
LRU_STEP_COLS = 256
LRU_OUT_ROW_GROUPS = 2
PROMPT_BLOCK_STEPS = 512
LRU_TM_BLOCK_STEPS = 128
LRU_TM_PARTS = 4
VMEM_LIMIT_BYTES = 56 * 1024 * 1024

_GELU_K1 = -2.0 * math.log2(math.e) * math.sqrt(2.0 / math.pi)
_GELU_K3 = _GELU_K1 * 0.044715


def _gelu(x):
    return x / (1.0 + jnp.exp2(x * (_GELU_K1 + _GELU_K3 * (x * x))))


def _dot(a, b):
    return jnp.dot(a, b, preferred_element_type=f32)


def _dot_nt(a, b, t_ref=None):
    if t_ref is None:
        return lax.dot_general(a, b, (((1,), (1,)), ((), ())), preferred_element_type=f32)
    t_ref[...] = b.T
    return jnp.dot(a, t_ref[...], preferred_element_type=f32)


def _layernorm_rows(y, g, b):
    mu = jnp.mean(y, axis=-1, keepdims=True)
    yc = y - mu
    var = jnp.mean(yc * yc, axis=-1, keepdims=True)
    return yc * lax.rsqrt(var + LN_EPS) * g + b


def _causal_conv(x3, carry, w_ref, b_ref, cs):
    BB, TL, W = x3.shape
    K = carry.shape[1]
    w = [w_ref[j:j + 1, cs] for j in range(K + 1)]

    def shift(p, j):
        bnd = w[0] * carry[:, K - j:K - j + 1, :]
        for i in range(1, j):
            bnd = bnd + w[i] * carry[:, K - j + i:K - j + i + 1, :]
        if TL == SUBLANES:
            t = lax.broadcasted_iota(jnp.int32, p.shape, 1)
            return jnp.where(t == 0, bnd, pltpu.roll(p, 1, 1))
        assert BB == 1
        rolled = pltpu.roll(p.reshape(TL, W), 1, 0)
        t = lax.broadcasted_iota(jnp.int32, (SUBLANES, W), 0)
        head = jnp.where(t == 0, bnd.reshape(1, W), rolled[:SUBLANES])
        return jnp.concatenate([head, rolled[SUBLANES:]], axis=0).reshape(BB, TL, W)

    p = w[0] * x3
    for j in range(1, K + 1):
        p = w[j] * x3 + shift(p, j)
    return b_ref[:, cs] + p


def _const_spec(shape):
    nd = len(shape)
    return pl.BlockSpec(shape, lambda b, t: (0,) * nd, pipeline_mode=pl.Buffered(1))


def _state_spec(block, per_seq):
    nd = len(block)
    if per_seq:
        return pl.BlockSpec(block, lambda b, t: (b,) + (0,) * (nd - 1))
    return pl.BlockSpec(block, lambda b, t: (0,) * nd)


def _params():
    return pltpu.CompilerParams(dimension_semantics=("arbitrary", "arbitrary"),
                                vmem_limit_bytes=VMEM_LIMIT_BYTES)


def _mixer0_init_tables(dmask_ref, wq_ref, wk_ref, tri_ref, lvl_ref, *, cb, tl):
    T = cb * tl
    tl_shift = tl.bit_length() - 1
    r = lax.broadcasted_iota(jnp.int32, (T, T), 0)
    c = lax.broadcasted_iota(jnp.int32, (T, T), 1)
    same = (r >> tl_shift) == (c >> tl_shift)
    rel = (r & (tl - 1)) - (c & (tl - 1))
    causal = same & (rel >= 0)
    relf = jnp.maximum(rel, 0).astype(f32)
    tri_ref[...] = jnp.where(causal, 1.0, 0.0).astype(bf16)
    lvl = jnp.where(r == c, 0, -1)
    s, li = 1, 1
    while s < tl:
        blk = (r >> li) == (c >> li)
        hit = blk & ((r & (2 * s - 1)) >= s) & ((c & (2 * s - 1)) < s)
        lvl = jnp.where(hit, li, lvl)
        s, li = 2 * s, li + 1
    lvl_ref[...] = lvl
    tr = (lax.broadcasted_iota(jnp.int32, (T, D_HEAD), 0) & (tl - 1)).astype(f32)
    for h in range(N_HEADS):
        lg = LOG_GAMMA[h]
        dmask_ref[h] = jnp.where(causal, jnp.exp(lg * relf), 0.0)
        wq_ref[h] = jnp.exp(lg * (tr + 1.0))
        wk_ref[h] = jnp.exp(lg * ((tl - 1.0) - tr))


def _hgrn_scores(gqs, kks, fs, bs, lvl_ref, dot_nt, *, T, tl):
    n = len(gqs)
    row = lax.broadcasted_iota(jnp.int32, (T, D_HEAD), 0)
    ntile = T // SUBLANES

    def owned(li, g):
        return lvl_ref[g * SUBLANES:(g + 1) * SUBLANES, :] == li

    full = [dot_nt(gqs[u].astype(bf16), kks[u].astype(bf16)) for u in range(n)]
    tiles = [[jnp.where(owned(0, g), full[u][g * SUBLANES:(g + 1) * SUBLANES], 0.0) for g in range(ntile)]
             for u in range(n)]

    def take(u, li, p, first_tile, n_tiles, p_row0):
        for g in range(n_tiles):
            t = first_tile + g
            tiles[u][t] = jnp.where(owned(li, t), p[p_row0 + g * SUBLANES:p_row0 + (g + 1) * SUBLANES], tiles[u][t])

    s, li = 1, 1
    while s < tl:
        nblk = T // (2 * s)
        if s < SUBLANES:
            up = (row & (2 * s - 1)) >= s
            zs = []
            for u in range(n):
                gq, kk, f, b = gqs[u], kks[u], fs[u], bs[u]
                if s == 1:
                    z = jnp.where(up, gq * f, kk)
                elif s == 2:
                    w = row & 3
                    e = jnp.where(w == 0, pltpu.roll(f, T - 1, 0),
                                  jnp.where(w == 1, 1.0, jnp.where(w == 2, f, f * pltpu.roll(f, 1, 0))))
                    z = jnp.where(up, gq, kk) * e
                else:
                    parts = [jnp.abs(b[m * 2 * s:(m + 1) * 2 * s, :] - b[m * 2 * s + s - 1:m * 2 * s + s, :])
                             for m in range(nblk)]
                    z = jnp.where(up, gq, kk) * jnp.exp2(-jnp.concatenate(parts, axis=0))
                zs.append(z.astype(bf16))
            ps = [dot_nt(zb, zb) for zb in zs]
            for u in range(n):
                take(u, li, ps[u], 0, ntile, 0)
        else:
            zs, qus = [], []
            for u in range(n):
                gq, kk, b = gqs[u], kks[u], bs[u]
                both, upper = [], []
                for m in range(nblk):
                    r0 = m * 2 * s
                    beta = b[r0 + s - 1:r0 + s, :]
                    k_lo = kk[r0:r0 + s] * jnp.exp2(beta - b[r0:r0 + s])
                    q_hi = gq[r0 + s:r0 + 2 * s] * jnp.exp2(b[r0 + s:r0 + 2 * s] - beta)
                    both += [k_lo, q_hi]
                    upper.append(q_hi)
                zs.append(jnp.concatenate(both, axis=0).astype(bf16))
                qus.append((upper[0] if nblk == 1 else jnp.concatenate(upper, axis=0)).astype(bf16))
            ps = [dot_nt(qus[u], zs[u]) for u in range(n)]
            for u in range(n):
                for m in range(nblk):
                    take(u, li, ps[u], (m * 2 * s + s) // SUBLANES, s // SUBLANES, m * s)
        s, li = 2 * s, li + 1
    return [jnp.concatenate(tiles[u], axis=0).astype(bf16) for u in range(n)]


def _mixer0_kernel(x_ref, c2_ref, s2_ref, win_ref, wout_ref, lbl_ref, nw_ref, g_ref, be_ref,
                   sret0_ref, shg0_ref, y_ref, sret_ref, shg_ref,
                   dmask_ref, wq_ref, wk_ref, tri_ref, lvl_ref, kt_ref, *, BB, TL, cb, tl):
    T = cb * tl
    R = BB * TL
    nch = R // T
    slots = iter(range(kt_ref.shape[0]))

    def dot_nt(a, b):
        return _dot_nt(a, b, kt_ref.at[next(slots)] if (cb == 1 and T == CHUNK_ROWS) else None)

    @pl.when((pl.program_id(0) == 0) & (pl.program_id(1) == 0))
    def _init_tables():
        _mixer0_init_tables(dmask_ref, wq_ref, wk_ref, tri_ref, lvl_ref, cb=cb, tl=tl)

    @pl.when(pl.program_id(1) == 0)
    def _init_state():
        sret_ref[...] = sret0_ref[...]
        shg_ref[...] = shg0_ref[...]

    x2 = x_ref[...].reshape(R, D_MODEL)
    xb = x2.astype(bf16)

    def proj(i):
        return _dot(xb, win_ref[:, i * SEG:(i + 1) * SEG])

    lbl = lbl_ref[...]
    le = jnp.exp(lbl - jnp.max(lbl, axis=0, keepdims=True))
    lb = le[0:1] / jnp.sum(le, axis=0, keepdims=True)
    nw = nw_ref[...]

    def seqs(z):
        return z.reshape(cb, tl, D_HEAD)

    def rows(z3):
        return z3.reshape(T, D_HEAD)

    def qs(q3, s3):
        if cb == 1:
            return _dot(q3[0], s3[0])[None]
        return jnp.einsum('bqd,bdv->bqv', q3, s3, preferred_element_type=f32)

    def ktv(k3, v3):
        if cb == 1:
            return lax.dot_general(k3[0], v3[0], (((0,), (0,)), ((), ())),
                                   preferred_element_type=f32)[None]
        return jnp.einsum('btd,btv->bdv', k3, v3, preferred_element_type=f32)

    units = [(c, h) for c in range(nch) for h in range(N_HEADS)]
    nu = len(units)

    def sl(z, c, h):
        return z[c * T:(c + 1) * T, h * D_HEAD:(h + 1) * D_HEAD]

    def seq0(c):
        return 0 if BB == 1 else c * cb

    def rope_tables(c):
        ts = slice(c * tl, (c + 1) * tl) if BB == 1 else slice(0, tl)
        return c2_ref[ts, :][None], s2_ref[ts, :][None]

    def rope(z, cos2, sin2):
        return rows(seqs(z) * cos2 + seqs(pltpu.roll(z, D_HEAD // 2, 1)) * sin2)

    tabs = [rope_tables(c) for c in range(nch)]
    ktabs = [(c2 * (D_HEAD ** -0.5), s2 * (D_HEAD ** -0.5)) for c2, s2 in tabs]

    rq, rk = proj(0), proj(1)
    q = [rope(sl(rq, c, h), *tabs[c]) for c, h in units]
    k = [rope(sl(rk, c, h), *ktabs[c]) for c, h in units]
    rv = proj(2)
    v = [sl(rv, c, h).astype(bf16) for c, h in units]
    att = [dot_nt(q[u].astype(bf16), k[u].astype(bf16)) for u in range(nu)]
    att = [(att[u] * dmask_ref[h]).astype(bf16) for u, (c, h) in enumerate(units)]
    o_ret = [_dot(att[u], v[u]) for u in range(nu)]
    qw = [seqs((q[u] * wq_ref[h]).astype(bf16)) for u, (c, h) in enumerate(units)]
    upd_ret = [ktv(seqs((k[u] * wk_ref[h]).astype(bf16)), seqs(v[u])) for u, (c, h) in enumerate(units)]

    hf = proj(5)
    lbs = [lb[:, h * D_HEAD:(h + 1) * D_HEAD] for h in range(N_HEADS)]
    f = [lbs[h] + (1.0 - lbs[h]) * jax.nn.sigmoid(sl(hf, c, h)) for c, h in units]
    lf = [jnp.log2(z) for z in f]
    kk = [1.0 - z for z in f]
    hq, hi = proj(4), proj(6)
    gq = [sl(hq, c, h) for c, h in units]
    gv = [sl(hi, c, h).astype(bf16) for c, h in units]
    split = []
    for z in lf:
        l1 = z.astype(bf16)
        r1 = z - l1.astype(f32)
        l2 = r1.astype(bf16)
        l3 = (r1 - l2.astype(f32)).astype(bf16)
        split.append(jnp.concatenate([l1, l2, l3], axis=1))
    cs = [_dot(tri_ref[...], z) for z in split]
    b = [z[:, :D_HEAD] + z[:, D_HEAD:2 * D_HEAD] + z[:, 2 * D_HEAD:] for z in cs]
    scores = _hgrn_scores(gq, kk, f, b, lvl_ref, dot_nt, T=T, tl=tl)
    o_hg = [_dot(scores[u], gv[u]) for u in range(nu)]
    qe = [seqs((gq[u] * jnp.exp2(b[u])).astype(bf16)) for u in range(nu)]
    b3 = [seqs(z) for z in b]
    bl = [z[:, tl - 1:tl, :] for z in b3]
    upd_hg = [ktv((seqs(kk[u]) * jnp.exp2(bl[u] - b3[u])).astype(bf16), seqs(gv[u])) for u in range(nu)]
    decay = []
    for z in bl:
        ez = jnp.exp2(z)
        d = [jnp.broadcast_to(ez[j], (D_HEAD, D_HEAD)).T for j in range(cb)]
        decay.append(d[0][None] if cb == 1 else jnp.stack(d))

    s_ret, s_hg = {}, {}
    for u, (c, h) in enumerate(units):
        s0 = seq0(c)
        first = BB > 1 or c == 0
        sp = sret_ref[s0:s0 + cb, h] if first else s_ret[h]
        sg = shg_ref[s0:s0 + cb, h] if first else s_hg[h]
        o_ret[u] = o_ret[u] + rows(qs(qw[u], sp.astype(bf16)))
        o_hg[u] = o_hg[u] + rows(qs(qe[u], sg.astype(bf16)))
        s_ret[h] = math.exp(LOG_GAMMA[h] * tl) * sp + upd_ret[u]
        s_hg[h] = decay[u] * sg + upd_hg[u]
        if BB > 1 or c == nch - 1:
            sret_ref[s0:s0 + cb, h] = s_ret[h]
            shg_ref[s0:s0 + cb, h] = s_hg[h]

    rg, hg = proj(3), proj(7)
    head_out = {}
    for u, (c, h) in enumerate(units):
        o = o_ret[u]
        mu = jnp.mean(o, axis=-1, keepdims=True)
        oc = o - mu
        var = jnp.mean(oc * oc, axis=-1, keepdims=True)
        head_out[(c, h)] = oc * lax.rsqrt(var + LN_EPS) * jax.nn.silu(sl(rg, c, h))
        og = o_hg[u]
        ms = jnp.mean(og * og, axis=-1, keepdims=True)
        head_out[(c, N_HEADS + h)] = og * lax.rsqrt(ms + LN_EPS) * nw * jax.nn.silu(sl(hg, c, h))
    chunk_out = [jnp.concatenate([head_out[(c, j)] for j in range(2 * N_HEADS)], axis=1).astype(bf16)
                 for c in range(nch)]
    mixed = chunk_out[0] if nch == 1 else jnp.concatenate(chunk_out, axis=0)
    m = _dot(mixed, wout_ref[...])
    y = _layernorm_rows(DEEPNORM_ALPHA * x2 + m, g_ref[...], be_ref[...])
    y_ref[...] = y.reshape(BB, TL, D_MODEL)


def _mixer0(x, c2, s2, w_in, w_out, lb_logits, norm_w, ln_g, ln_b, sret0, shg0, *, BB, TL):
    B, L, _ = x.shape
    R = BB * TL
    T = min(R, CHUNK_ROWS)
    if BB == 1:
        cb, tl = 1, T
    else:
        assert T % TL == 0
        cb, tl = T // TL, TL
    per_seq = sret0.shape[0] == B and B > 1
    st_block = (BB, N_HEADS, D_HEAD, D_HEAD)
    n_score_dots = N_HEADS * (R // T) * (2 + (tl - 1).bit_length()) if cb == 1 else 1
    kern = functools.partial(_mixer0_kernel, BB=BB, TL=TL, cb=cb, tl=tl)
    return pl.pallas_call(
        kern, grid=(B // BB, L // TL),
        in_specs=[
            pl.BlockSpec((BB, TL, D_MODEL), lambda b, t: (b, t, 0)),
            pl.BlockSpec((TL, D_HEAD), lambda b, t: (t, 0)),
            pl.BlockSpec((TL, D_HEAD), lambda b, t: (t, 0)),
            _const_spec(w_in.shape), _const_spec(w_out.shape), _const_spec(lb_logits.shape),
            _const_spec(norm_w.shape), _const_spec(ln_g.shape), _const_spec(ln_b.shape),
            _state_spec(st_block, per_seq), _state_spec(st_block, per_seq),
        ],
        out_specs=[
            pl.BlockSpec((BB, TL, D_MODEL), lambda b, t: (b, t, 0)),
            _state_spec(st_block, True), _state_spec(st_block, True),
        ],
        out_shape=[
            jax.ShapeDtypeStruct((B, L, D_MODEL), f32),
            jax.ShapeDtypeStruct((B, N_HEADS, D_HEAD, D_HEAD), f32),
            jax.ShapeDtypeStruct((B, N_HEADS, D_HEAD, D_HEAD), f32),
        ],
        scratch_shapes=[
            pltpu.VMEM((N_HEADS, T, T), f32),
            pltpu.VMEM((N_HEADS, T, D_HEAD), f32),
            pltpu.VMEM((N_HEADS, T, D_HEAD), f32),
            pltpu.VMEM((T, T), bf16),
            pltpu.VMEM((T, T), jnp.int32),
            pltpu.VMEM((n_score_dots, D_HEAD, T), bf16),
        ],
        compiler_params=_params(), name=f"mixer0_b{BB}_t{TL}",
    )(x, c2, s2, w_in, w_out, lb_logits, norm_w, ln_g, ln_b, sret0, shg0)


def _ffn_kernel(x_ref, wup_ref, cw_ref, cb_ref, wdn_ref, g_ref, be_ref, buf0_ref,
                y_ref, buf_ref, u_ref, *, BB, TL):
    R = BB * TL
    K = CONV_W_FFN - 1

    @pl.when(pl.program_id(1) == 0)
    def _init_state():
        buf_ref[...] = buf0_ref[...]

    x2 = x_ref[...].reshape(R, D_MODEL)
    xb = x2.astype(bf16)
    u = _dot(xb, wup_ref[:, :D_FF])
    v = _dot(xb, wup_ref[:, D_FF:])
    u_ref[:, SUBLANES - K:SUBLANES, :] = buf_ref[...]
    u_ref[:, SUBLANES:, :] = u.reshape(BB, TL, D_FF)
    cw = cw_ref[...]
    uc = cb_ref[...] + u_ref[:, SUBLANES - K:SUBLANES - K + TL, :] * cw[0:1]
    for j in range(1, CONV_W_FFN):
        uc = uc + u_ref[:, SUBLANES - K + j:SUBLANES - K + j + TL, :] * cw[j:j + 1]
    buf_ref[...] = u_ref[:, SUBLANES + TL - K:SUBLANES + TL, :]
    hmid = _gelu(uc.reshape(R, D_FF)) * v
    fo = _dot(hmid.astype(bf16), wdn_ref[...])
    y = _layernorm_rows(DEEPNORM_ALPHA * x2 + fo, g_ref[...], be_ref[...])
    y_ref[...] = y.reshape(BB, TL, D_MODEL)


def _ffn(x, w_up, conv_w, conv_b, w_down, ln_g, ln_b, buf0, *, layer, BB, TL):
    B, L, _ = x.shape
    per_seq = buf0.shape[0] == B and B > 1
    K = CONV_W_FFN - 1
    kern = functools.partial(_ffn_kernel, BB=BB, TL=TL)
    layer_spec = lambda rows, cols: pl.BlockSpec((rows, cols), lambda b, t: (layer, 0), pipeline_mode=pl.Buffered(1))
    return pl.pallas_call(
        kern, grid=(B // BB, L // TL),
        in_specs=[
            pl.BlockSpec((BB, TL, D_MODEL), lambda b, t: (b, t, 0)),
            layer_spec(D_MODEL, 2 * D_FF), _const_spec(conv_w.shape), _const_spec(conv_b.shape),
            layer_spec(D_FF, D_MODEL), _const_spec(ln_g.shape), _const_spec(ln_b.shape),
            _state_spec((BB, K, D_FF), per_seq),
        ],
        out_specs=[
            pl.BlockSpec((BB, TL, D_MODEL), lambda b, t: (b, t, 0)),
            _state_spec((BB, K, D_FF), True),
        ],
        out_shape=[
            jax.ShapeDtypeStruct((B, L, D_MODEL), f32),
            jax.ShapeDtypeStruct((B, K, D_FF), f32),
        ],
        scratch_shapes=[pltpu.VMEM((BB, SUBLANES + TL, D_FF), f32)],
        compiler_params=_params(), name=f"ffn_b{BB}_t{TL}",
    )(x, w_up, conv_w, conv_b, w_down, ln_g, ln_b, buf0)


def _rglru_kernel(x_ref, win_ref, cw_ref, cb_ref, wga_ref, bga_ref, wgx_ref, bgx_ref, lam_ref,
                  wout_ref, g_ref, be_ref, h0_ref, buf0_ref,
                  y_ref, h_ref, buf_ref, *, BB, TL):
    R = BB * TL
    K = CONV_W_LRU - 1
    G = TL // SUBLANES

    @pl.when(pl.program_id(1) == 0)
    def _init_state():
        h_ref[...] = h0_ref[...]
        buf_ref[...] = buf0_ref[...]

    x2 = x_ref[...].reshape(R, D_MODEL)
    xb = x2.astype(bf16)
    W = LRU_STEP_COLS
    nchunk = D_RNN // W
    carry = buf_ref[...]
    new_carry = []
    sub = lax.broadcasted_iota(jnp.int32, (R // SUBLANES, SUBLANES, W), 1)
    h0 = h_ref[...]

    def project(ci):
        c0 = ci * W
        return _dot(xb, win_ref[:, c0:c0 + W]), _dot(xb, win_ref[:, D_RNN + c0:D_RNN + c0 + W])

    def recur(ci, gate, rnn):
        c0 = ci * W
        cs = slice(c0, c0 + W)
        rnn3 = rnn.reshape(BB, TL, W)
        xc = _causal_conv(rnn3, carry[:, :, cs], cw_ref, cb_ref, cs).reshape(R, W)
        new_carry.append(rnn3[:, TL - K:, :])
        xcb = xc.astype(bf16)
        blocks = [(j, c0 // LRU_BLOCK + j) for j in range(W // LRU_BLOCK)]
        ga = jnp.concatenate([_dot(xcb[:, j * LRU_BLOCK:(j + 1) * LRU_BLOCK], wga_ref[n]) for j, n in blocks], axis=1)
        gx = jnp.concatenate([_dot(xcb[:, j * LRU_BLOCK:(j + 1) * LRU_BLOCK], wgx_ref[n]) for j, n in blocks], axis=1)
        rgate = jax.nn.sigmoid(ga + bga_ref[:, cs])
        igate = jax.nn.sigmoid(gx + bgx_ref[:, cs])
        nla = rgate * (-LRU_C * jax.nn.log_sigmoid(lam_ref[:, cs]))
        a = jnp.exp(-nla)
        w1 = jnp.tanh(nla) * (1.0 + a * a)
        bv = jnp.where(w1 > 0.0, w1 * lax.rsqrt(w1), 0.0) * (igate * xc)
        a4 = a.reshape(R // SUBLANES, SUBLANES, W)
        b4 = bv.reshape(R // SUBLANES, SUBLANES, W)
        for d in (1, 2, 4):
            keep = sub >= d
            b4 = jnp.where(keep, a4 * pltpu.roll(b4, d, 1) + b4, b4)
            a4 = jnp.where(keep, a4 * pltpu.roll(a4, d, 1), a4)
        a5 = a4.reshape(BB, G, SUBLANES, W)
        b5 = b4.reshape(BB, G, SUBLANES, W)
        hc = h0[:, :, cs]
        tiles = []
        for gi in range(G):
            hg = a5[:, gi] * hc + b5[:, gi]
            hc = hg[:, SUBLANES - 1:SUBLANES, :]
            tiles.append(hg)
        hseq = tiles[0] if G == 1 else jnp.concatenate(tiles, axis=1)
        return (_gelu(gate) * hseq.reshape(R, W)).astype(bf16), hc

    nxt = project(0)
    ys, h_last = [], []
    for ci in range(nchunk):
        cur, nxt = nxt, (project(ci + 1) if ci + 1 < nchunk else None)
        yv, hc = recur(ci, *cur)
        ys.append(yv)
        h_last.append(hc)
    buf_ref[...] = jnp.concatenate(new_carry, axis=2)
    h_ref[...] = jnp.concatenate(h_last, axis=2)
    yv = jnp.concatenate(ys, axis=1)
    groups = LRU_OUT_ROW_GROUPS if R % (LRU_OUT_ROW_GROUPS * 2 * SUBLANES) == 0 else 1
    step = R // groups
    outs = [_layernorm_rows(DEEPNORM_ALPHA * x2[i * step:(i + 1) * step]
                            + _dot(yv[i * step:(i + 1) * step], wout_ref[...]), g_ref[...], be_ref[...])
            for i in range(groups)]
    y = outs[0] if groups == 1 else jnp.concatenate(outs, axis=0)
    y_ref[...] = y.reshape(BB, TL, D_MODEL)


def _rglru(x, w_in, conv_w, conv_b, wga, bga, wgx, bgx, lam, w_out, ln_g, ln_b, h0, buf0, *, BB, TL):
    B, L, _ = x.shape
    per_seq = h0.shape[0] == B and B > 1
    K = CONV_W_LRU - 1
    kern = functools.partial(_rglru_kernel, BB=BB, TL=TL)
    consts = (w_in, conv_w, conv_b, wga, bga, wgx, bgx, lam, w_out, ln_g, ln_b)
    return pl.pallas_call(
        kern, grid=(B // BB, L // TL),
        in_specs=[pl.BlockSpec((BB, TL, D_MODEL), lambda b, t: (b, t, 0))]
        + [_const_spec(c.shape) for c in consts]
        + [_state_spec((BB, 1, D_RNN), per_seq), _state_spec((BB, K, D_RNN), per_seq)],
        out_specs=[
            pl.BlockSpec((BB, TL, D_MODEL), lambda b, t: (b, t, 0)),
            _state_spec((BB, 1, D_RNN), True), _state_spec((BB, K, D_RNN), True),
        ],
        out_shape=[
            jax.ShapeDtypeStruct((B, L, D_MODEL), f32),
            jax.ShapeDtypeStruct((B, 1, D_RNN), f32),
            jax.ShapeDtypeStruct((B, K, D_RNN), f32),
        ],
        compiler_params=_params(), name=f"rglru_b{BB}_t{TL}",
    )(x, *consts, h0, buf0)


def _rglru_tm_kernel(x_ref, win_ref, cw_ref, cb_ref, wga_ref, bga_ref, wgx_ref, bgx_ref, lam_ref,
                     wout_ref, g_ref, be_ref, h0_ref, buf0_ref,
                     y_ref, h_ref, buf_ref, perm_ref, permt_ref, hrun_ref, crun_ref, *, TL):
    B = SUBLANES
    NPART = LRU_TM_PARTS
    TP = TL // NPART
    RP = B * TP
    K = CONV_W_LRU - 1
    W = LRU_STEP_COLS
    nchunk = D_RNN // W
    step = pl.program_id(0)

    @pl.when(step == 0)
    def _init():
        r = lax.broadcasted_iota(jnp.int32, (RP, RP), 0)
        c = lax.broadcasted_iota(jnp.int32, (RP, RP), 1)
        hit = ((r & (B - 1)) * TP + (r >> 3)) == c
        perm_ref[...] = jnp.where(hit, 1.0, 0.0).astype(bf16)
        hit_t = ((c & (B - 1)) * TP + (c >> 3)) == r
        permt_ref[...] = jnp.where(hit_t, 1.0, 0.0).astype(bf16)
        hrun_ref[...] = jnp.broadcast_to(h0_ref[0], (B, D_RNN))
        for k in range(K):
            crun_ref[k] = jnp.broadcast_to(buf0_ref[0, k:k + 1, :], (B, D_RNN))

    def head(p):
        x2 = x_ref[:, p * TP:(p + 1) * TP, :].reshape(RP, D_MODEL)
        return x2, _dot(perm_ref[...], x2.astype(bf16)).astype(bf16)

    def project(xt, ci):
        c0 = ci * W
        return _dot(xt, win_ref[:, c0:c0 + W]), _dot(xt, win_ref[:, D_RNN + c0:D_RNN + c0 + W])

    def body(ci, gate, rnn):
        c0 = ci * W
        cs = slice(c0, c0 + W)
        rnn = rnn.reshape(TP, B, W)
        ext = jnp.concatenate([crun_ref[:, :, cs], rnn], axis=0)
        xc = cb_ref[:, cs] + ext[0:TP] * cw_ref[0:1, cs]
        for j in range(1, CONV_W_LRU):
            xc = xc + ext[j:j + TP] * cw_ref[j:j + 1, cs]
        crun_ref[:, :, cs] = rnn[TP - K:]
        xc = xc.reshape(RP, W)
        xcb = xc.astype(bf16)
        blocks = [(j, c0 // LRU_BLOCK + j) for j in range(W // LRU_BLOCK)]
        ga = jnp.concatenate([_dot(xcb[:, j * LRU_BLOCK:(j + 1) * LRU_BLOCK], wga_ref[n]) for j, n in blocks], axis=1)
        gx = jnp.concatenate([_dot(xcb[:, j * LRU_BLOCK:(j + 1) * LRU_BLOCK], wgx_ref[n]) for j, n in blocks], axis=1)
        rgate = jax.nn.sigmoid(ga + bga_ref[:, cs])
        igate = jax.nn.sigmoid(gx + bgx_ref[:, cs])
        log_a = LRU_C * rgate * jax.nn.log_sigmoid(lam_ref[:, cs])
        a = jnp.exp(log_a).reshape(TP, B, W)
        th = jnp.tanh(log_a)
        bv = (jnp.sqrt(-2.0 * th / (1.0 - th)) * (igate * xc)).reshape(TP, B, W)
        h = hrun_ref[:, cs]
        hs = []
        for t in range(TP):
            h = a[t] * h + bv[t]
            hs.append(h)
        hrun_ref[:, cs] = h
        return (_gelu(gate) * jnp.stack(hs, axis=0).reshape(RP, W)).astype(bf16)

    def tail(p, x2, ys):
        yv = _dot(permt_ref[...], jnp.concatenate(ys, axis=1)).astype(bf16)
        y = _layernorm_rows(DEEPNORM_ALPHA * x2 + _dot(yv, wout_ref[...]), g_ref[...], be_ref[...])
        y_ref[:, p * TP:(p + 1) * TP, :] = y.reshape(B, TP, D_MODEL)

    x2s, xts, nxt = {}, {}, None
    x2s[0], xts[0] = head(0)
    nxt = project(xts[0], 0)
    for p in range(NPART):
        if p + 1 < NPART:
            x2s[p + 1], xts[p + 1] = head(p + 1)
        ys = []
        for ci in range(nchunk):
            cur = nxt
            if ci + 1 < nchunk:
                nxt = project(xts[p], ci + 1)
            elif p + 1 < NPART:
                nxt = project(xts[p + 1], 0)
            ys.append(body(ci, *cur))
        tail(p, x2s[p], ys)

    @pl.when(step == pl.num_programs(0) - 1)
    def _emit_state():
        h_ref[:, 0, :] = hrun_ref[...]
        for k in range(K):
            buf_ref[:, k, :] = crun_ref[k]


def _rglru_tm(x, w_in, conv_w, conv_b, wga, bga, wgx, bgx, lam, w_out, ln_g, ln_b, h0, buf0, *, TL):
    B, L, _ = x.shape
    assert B == SUBLANES and h0.shape[0] == 1 and buf0.shape[0] == 1
    K = CONV_W_LRU - 1
    R = B * TL // LRU_TM_PARTS
    consts = (w_in, conv_w, conv_b, wga, bga, wgx, bgx, lam, w_out, ln_g, ln_b, h0, buf0)
    const_spec = lambda shape: pl.BlockSpec(shape, lambda t: (0,) * len(shape), pipeline_mode=pl.Buffered(1))
    return pl.pallas_call(
        functools.partial(_rglru_tm_kernel, TL=TL), grid=(L // TL,),
        in_specs=[pl.BlockSpec((B, TL, D_MODEL), lambda t: (0, t, 0))] + [const_spec(c.shape) for c in consts],
        out_specs=[
            pl.BlockSpec((B, TL, D_MODEL), lambda t: (0, t, 0)),
            pl.BlockSpec((B, 1, D_RNN), lambda t: (0, 0, 0)), pl.BlockSpec((B, K, D_RNN), lambda t: (0, 0, 0)),
        ],
        out_shape=[
            jax.ShapeDtypeStruct((B, L, D_MODEL), f32),
            jax.ShapeDtypeStruct((B, 1, D_RNN), f32),
            jax.ShapeDtypeStruct((B, K, D_RNN), f32),
        ],
        scratch_shapes=[
            pltpu.VMEM((R, R), bf16), pltpu.VMEM((R, R), bf16),
            pltpu.VMEM((B, D_RNN), f32),
            pltpu.VMEM((K, B, D_RNN), f32),
        ],
        compiler_params=pltpu.CompilerParams(dimension_semantics=("arbitrary",), vmem_limit_bytes=VMEM_LIMIT_BYTES),
        name=f"rglru_tm_t{TL}",
    )(x, *consts)


def _rope_tables(pos0, length):
    half = D_HEAD // 2
    pos = pos0 + jnp.arange(length, dtype=jnp.int32)
    inv = ROPE_BASE ** (-jnp.arange(half, dtype=f32) / half)
    ang = pos.astype(f32)[:, None] * inv[None, :]
    cos, sin = jnp.cos(ang), jnp.sin(ang)
    return jnp.concatenate([cos, cos], axis=1), jnp.concatenate([-sin, sin], axis=1)


def _trunk(x, pos0, states, p, *, BB, TL, BB0, lru_tm_steps=None):
    s_ret, s_hgrn, h_lru, buf_lru, buf_ffn0, buf_ffn1 = states
    c2, s2 = _rope_tables(pos0, x.shape[1])
    x, s_ret, s_hgrn = _mixer0(x, c2, s2, p['w_in_ab'], p['w_out_ab'], p['hgrn_lb_logits'], p['hgrn_norm_w'],
                               p['ln_mix_g'][0], p['ln_mix_b'][0], s_ret, s_hgrn, BB=BB0, TL=TL)
    x, buf_ffn0 = _ffn(x, p['w_ffn_up'], p['ffn_conv_w'][0], p['ffn_conv_b'][0], p['w_ffn_down'],
                       p['ln_ffn_g'][0], p['ln_ffn_b'][0], buf_ffn0, layer=0, BB=BB, TL=TL)
    lru_args = (x, p['w_in_c'], p['conv_w_c'], p['conv_b_c'], p['w_gate_a'], p['b_gate_a'], p['w_gate_x'],
                p['b_gate_x'], p['lru_lambda'], p['w_out_c'], p['ln_mix_g'][1], p['ln_mix_b'][1], h_lru, buf_lru)
    if lru_tm_steps is None:
        x, h_lru, buf_lru = _rglru(*lru_args, BB=BB, TL=TL)
    else:
        x, h_lru, buf_lru = _rglru_tm(*lru_args, TL=lru_tm_steps)
    x, buf_ffn1 = _ffn(x, p['w_ffn_up'], p['ffn_conv_w'][1], p['ffn_conv_b'][1], p['w_ffn_down'],
                       p['ln_ffn_g'][1], p['ln_ffn_b'][1], buf_ffn1, layer=1, BB=BB, TL=TL)
    return x, (s_ret, s_hgrn, h_lru, buf_lru, buf_ffn0, buf_ffn1)


def kernel(x_prompt, x_sample, state_ret, state_hgrn, state_rglru_h, state_rglru_conv, state_ffn_conv, meta_tokens, w_in_ab, w_out_ab, hgrn_lb_logits, hgrn_norm_w, w_in_c, conv_w_c, conv_b_c, w_gate_a, b_gate_a, w_gate_x, b_gate_x, lru_lambda, w_out_c, ln_mix_g, ln_mix_b, ln_ffn_g, ln_ffn_b, w_ffn_up, ffn_conv_w, ffn_conv_b, w_ffn_down):
    row = lambda z: z.reshape(1, -1)
    wb = [w.astype(bf16) for w in (w_in_ab, w_out_ab, w_in_c, w_gate_a, w_gate_x, w_out_c,
                                   w_ffn_up.reshape(DEPTH * D_MODEL, 2 * D_FF),
                                   w_ffn_down.reshape(DEPTH * D_FF, D_MODEL))]
    p = dict(
        w_in_ab=wb[0], w_out_ab=wb[1],
        hgrn_lb_logits=hgrn_lb_logits, hgrn_norm_w=row(hgrn_norm_w),
        w_in_c=wb[2], conv_w_c=conv_w_c, conv_b_c=row(conv_b_c),
        w_gate_a=wb[3], b_gate_a=row(b_gate_a),
        w_gate_x=wb[4], b_gate_x=row(b_gate_x),
        lru_lambda=row(lru_lambda), w_out_c=wb[5],
        ln_mix_g=[row(ln_mix_g[i]) for i in range(DEPTH)], ln_mix_b=[row(ln_mix_b[i]) for i in range(DEPTH)],
        ln_ffn_g=[row(ln_ffn_g[i]) for i in range(DEPTH)], ln_ffn_b=[row(ln_ffn_b[i]) for i in range(DEPTH)],
        w_ffn_up=wb[6], w_ffn_down=wb[7],
        ffn_conv_w=[ffn_conv_w[i] for i in range(DEPTH)],
        ffn_conv_b=[row(ffn_conv_b[i]) for i in range(DEPTH)],
    )
    dt = x_prompt.dtype
    zero_states = (
        jnp.zeros((1, N_HEADS, D_HEAD, D_HEAD), dt), jnp.zeros((1, N_HEADS, D_HEAD, D_HEAD), dt),
        jnp.zeros((1, 1, D_RNN), dt), jnp.zeros((1, CONV_W_LRU - 1, D_RNN), dt),
        jnp.zeros((1, CONV_W_FFN - 1, D_FF), dt), jnp.zeros((1, CONV_W_FFN - 1, D_FF), dt),
    )
    _, meta_states = _trunk(meta_tokens.astype(dt)[None], 0, zero_states, p, BB=1, TL=N_META, BB0=1)
    y_prompt, ps = _trunk(x_prompt, N_META, meta_states, p, BB=1, TL=PROMPT_BLOCK_STEPS, BB0=1,
                          lru_tm_steps=LRU_TM_BLOCK_STEPS)
    sample_states = (state_ret, state_hgrn, state_rglru_h[:, None, :], state_rglru_conv,
                     state_ffn_conv[0], state_ffn_conv[1])
    y_sample, ss = _trunk(x_sample, PAST_LEN, sample_states, p, BB=32, TL=x_sample.shape[1], BB0=16)
    return (y_prompt, y_sample, ps[0], ss[0], ps[1], ss[1], ps[2][:, 0, :], ss[2][:, 0, :], ps[3], ss[3],
            jnp.stack([ps[4], ps[5]]), jnp.stack([ss[4], ss[5]]))
```

```python
import functools
import math

import jax
import jax.numpy as jnp
from jax import lax
from jax.experimental import pallas as pl
from jax.experimental.pallas import tpu as pltpu

f32 = jnp.float32
bf16 = jnp.bfloat16

D_MODEL = 1024
N_META = 16
PAST_LEN = 16384
N_HEADS = 4
D_HEAD = 128
SEG = N_HEADS * D_HEAD
ROPE_BASE = 10000.0
D_RNN = 1024
N_LRU_BLOCKS = 8
LRU_BLOCK = D_RNN // N_LRU_BLOCKS
CONV_W_LRU = 4
LRU_C = 8.0
D_FF = 2816
CONV_W_FFN = 3
LN_EPS = 1e-5
DEPTH = 2
DEEPNORM_ALPHA = (2.0 * DEPTH) ** 0.25
LOG_GAMMA = tuple(math.log1p(-(2.0 ** (-5.0 - h))) for h in range(N_HEADS))

SUBLANES = 8
CHUNK_ROWS = 128
LRU_STEP_COLS = 256
LRU_OUT_ROW_GROUPS = 2
PROMPT_BLOCK_STEPS = 512
PROMPT_MIXER0_BLOCK_STEPS = 512
LRU_TM_BLOCK_STEPS = 128
LRU_TM_PARTS = 4
LOG2_SUBLANES = SUBLANES.bit_length() - 1
FFN_COL_BLOCK = 1408
VMEM_LIMIT_BYTES = 56 * 1024 * 1024

_GELU_K1 = -2.0 * math.log2(math.e) * math.sqrt(2.0 / math.pi)
_GELU_K3 = _GELU_K1 * 0.044715


def _gelu(x):
    return x / (1.0 + jnp.exp2(x * (_GELU_K1 + _GELU_K3 * (x * x))))


def _dot(a, b):
    return jnp.dot(a, b, preferred_element_type=f32)


def _dot_nt(a, b, t_ref=None):
    if t_ref is None:
        return lax.dot_general(a, b, (((1,), (1,)), ((), ())), preferred_element_type=f32)
    t_ref[...] = b.T
    return jnp.dot(a, t_ref[...], preferred_element_type=f32)


def _layernorm_rows(y, g, b):
    mu = jnp.mean(y, axis=-1, keepdims=True)
    yc = y - mu
    var = jnp.mean(yc * yc, axis=-1, keepdims=True)
    return yc * lax.rsqrt(var + LN_EPS) * g + b


def _causal_conv(x3, carry, w_ref, b_ref, cs):
    BB, TL, W = x3.shape
    K = carry.shape[1]
    w = [w_ref[j:j + 1, cs] for j in range(K + 1)]

    def shift(p, j):
        bnd = w[0] * carry[:, K - j:K - j + 1, :]
        for i in range(1, j):
            bnd = bnd + w[i] * carry[:, K - j + i:K - j + i + 1, :]
        if TL == SUBLANES:
            t = lax.broadcasted_iota(jnp.int32, p.shape, 1)
            return jnp.where(t == 0, bnd, pltpu.roll(p, 1, 1))
        assert BB == 1
        rolled = pltpu.roll(p.reshape(TL, W), 1, 0)
        t = lax.broadcasted_iota(jnp.int32, (SUBLANES, W), 0)
        head = jnp.where(t == 0, bnd.reshape(1, W), rolled[:SUBLANES])
        return jnp.concatenate([head, rolled[SUBLANES:]], axis=0).reshape(BB, TL, W)

    p = w[0] * x3
    for j in range(1, K + 1):
        p = w[j] * x3 + shift(p, j)
    return b_ref[:, cs] + p


def _const_spec(shape):
    nd = len(shape)
    return pl.BlockSpec(shape, lambda b, t: (0,) * nd, pipeline_mode=pl.Buffered(1))


def _state_spec(block, per_seq):
    nd = len(block)
    if per_seq:
        return pl.BlockSpec(block, lambda b, t: (b,) + (0,) * (nd - 1))
    return pl.BlockSpec(block, lambda b, t: (0,) * nd)


def _params():
    return pltpu.CompilerParams(dimension_semantics=("arbitrary", "arbitrary"),
                                vmem_limit_bytes=VMEM_LIMIT_BYTES)


def _mixer0_init_tables(dmask_ref, wq_ref, wk_ref, tri_ref, lvl_ref, *, cb, tl):
    T = cb * tl
    tl_shift = tl.bit_length() - 1
    r = lax.broadcasted_iota(jnp.int32, (T, T), 0)
    c = lax.broadcasted_iota(jnp.int32, (T, T), 1)
    same = (r >> tl_shift) == (c >> tl_shift)
    rel = (r & (tl - 1)) - (c & (tl - 1))
    causal = same & (rel >= 0)
    relf = jnp.maximum(rel, 0).astype(f32)
    tri_ref[...] = jnp.where(causal, 1.0, 0.0).astype(bf16)
    lvl = jnp.where(r == c, 0, -1)
    s, li = 1, 1
    while s < tl:
        blk = (r >> li) == (c >> li)
        hit = blk & ((r & (2 * s - 1)) >= s) & ((c & (2 * s - 1)) < s)
        lvl = jnp.where(hit, li, lvl)
        s, li = 2 * s, li + 1
    lvl_ref[...] = lvl
    tr = (lax.broadcasted_iota(jnp.int32, (T, D_HEAD), 0) & (tl - 1)).astype(f32)
    for h in range(N_HEADS):
        lg = LOG_GAMMA[h]
        dmask_ref[h] = jnp.where(causal, jnp.exp(lg * relf), 0.0)
        wq_ref[h] = jnp.exp(lg * (tr + 1.0))
        wk_ref[h] = jnp.exp(lg * ((tl - 1.0) - tr))


def _hgrn_scores(gqs, kks, fs, bs, lvl_ref, dot_nt, *, T, tl):
    n = len(gqs)
    row = lax.broadcasted_iota(jnp.int32, (T, D_HEAD), 0)
    ntile = T // SUBLANES

    def owned(li, g):
        return lvl_ref[g * SUBLANES:(g + 1) * SUBLANES, :] == li

    full = [dot_nt(gqs[u].astype(bf16), kks[u].astype(bf16)) for u in range(n)]
    tiles = [[jnp.where(owned(0, g), full[u][g * SUBLANES:(g + 1) * SUBLANES], 0.0) for g in range(ntile)]
             for u in range(n)]

    def take(u, li, p, first_tile, n_tiles, p_row0):
        for g in range(n_tiles):
            t = first_tile + g
            tiles[u][t] = jnp.where(owned(li, t), p[p_row0 + g * SUBLANES:p_row0 + (g + 1) * SUBLANES], tiles[u][t])

    s, li = 1, 1
    while s < tl:
        nblk = T // (2 * s)
        if s < SUBLANES:
            up = (row & (2 * s - 1)) >= s
            zs = []
            for u in range(n):
                gq, kk, f, b = gqs[u], kks[u], fs[u], bs[u]
                if s == 1:
                    z = jnp.where(up, gq * f, kk)
                elif s == 2:
                    w = row & 3
                    e = jnp.where(w == 0, pltpu.roll(f, T - 1, 0),
                                  jnp.where(w == 1, 1.0, jnp.where(w == 2, f, f * pltpu.roll(f, 1, 0))))
                    z = jnp.where(up, gq, kk) * e
                else:
                    parts = [jnp.abs(b[m * 2 * s:(m + 1) * 2 * s, :] - b[m * 2 * s + s - 1:m * 2 * s + s, :])
                             for m in range(nblk)]
                    z = jnp.where(up, gq, kk) * jnp.exp2(-jnp.concatenate(parts, axis=0))
                zs.append(z.astype(bf16))
            ps = [dot_nt(zb, zb) for zb in zs]
            for u in range(n):
                take(u, li, ps[u], 0, ntile, 0)
        else:
            zs, qus = [], []
            for u in range(n):
                gq, kk, b = gqs[u], kks[u], bs[u]
                both, upper = [], []
                for m in range(nblk):
                    r0 = m * 2 * s
                    beta = b[r0 + s - 1:r0 + s, :]
                    k_lo = kk[r0:r0 + s] * jnp.exp2(beta - b[r0:r0 + s])
                    q_hi = gq[r0 + s:r0 + 2 * s] * jnp.exp2(b[r0 + s:r0 + 2 * s] - beta)
                    both += [k_lo, q_hi]
                    upper.append(q_hi)
                zs.append(jnp.concatenate(both, axis=0).astype(bf16))
                qus.append((upper[0] if nblk == 1 else jnp.concatenate(upper, axis=0)).astype(bf16))
            ps = [dot_nt(qus[u], zs[u]) for u in range(n)]
            for u in range(n):
                for m in range(nblk):
                    take(u, li, ps[u], (m * 2 * s + s) // SUBLANES, s // SUBLANES, m * s)
        s, li = 2 * s, li + 1
    return [jnp.concatenate(tiles[u], axis=0).astype(bf16) for u in range(n)]


def _mixer0_kernel(x_ref, c2_ref, s2_ref, win_ref, wout_ref, lbl_ref, nw_ref, g_ref, be_ref,
                   sret0_ref, shg0_ref, y_ref, sret_ref, shg_ref,
                   dmask_ref, wq_ref, wk_ref, tri_ref, lvl_ref, kt_ref, *, BB, TL, cb, tl):
    T = cb * tl
    R = BB * TL
    nch = R // T
    slots = iter(range(kt_ref.shape[0]))

    def dot_nt(a, b):
        return _dot_nt(a, b, kt_ref.at[next(slots)] if (cb == 1 and T == CHUNK_ROWS) else None)

    @pl.when((pl.program_id(0) == 0) & (pl.program_id(1) == 0))
    def _init_tables():
        _mixer0_init_tables(dmask_ref, wq_ref, wk_ref, tri_ref, lvl_ref, cb=cb, tl=tl)

    @pl.when(pl.program_id(1) == 0)
    def _init_state():
        sret_ref[...] = sret0_ref[...]
        shg_ref[...] = shg0_ref[...]

    x2 = x_ref[...].reshape(R, D_MODEL)
    xb = x2.astype(bf16)

    def proj(i):
        return _dot(xb, win_ref[:, i * SEG:(i + 1) * SEG])

    lbl = lbl_ref[...]
    le = jnp.exp(lbl - jnp.max(lbl, axis=0, keepdims=True))
    lb = le[0:1] / jnp.sum(le, axis=0, keepdims=True)
    nw = nw_ref[...]

    def seqs(z):
        return z.reshape(cb, tl, D_HEAD)

    def rows(z3):
        return z3.reshape(T, D_HEAD)

    def qs(q3, s3):
        if cb == 1:
            return _dot(q3[0], s3[0])[None]
        return jnp.einsum('bqd,bdv->bqv', q3, s3, preferred_element_type=f32)

    def ktv(k3, v3):
        if cb == 1:
            return lax.dot_general(k3[0], v3[0], (((0,), (0,)), ((), ())),
                                   preferred_element_type=f32)[None]
        return jnp.einsum('btd,btv->bdv', k3, v3, preferred_element_type=f32)

    units = [(c, h) for c in range(nch) for h in range(N_HEADS)]
    nu = len(units)

    def sl(z, c, h):
        return z[c * T:(c + 1) * T, h * D_HEAD:(h + 1) * D_HEAD]

    def seq0(c):
        return 0 if BB == 1 else c * cb

    def rope_tables(c):
        ts = slice(c * tl, (c + 1) * tl) if BB == 1 else slice(0, tl)
        return c2_ref[ts, :][None], s2_ref[ts, :][None]

    def rope(z, cos2, sin2):
        return rows(seqs(z) * cos2 + seqs(pltpu.roll(z, D_HEAD // 2, 1)) * sin2)

    tabs = [rope_tables(c) for c in range(nch)]
    ktabs = [(c2 * (D_HEAD ** -0.5), s2 * (D_HEAD ** -0.5)) for c2, s2 in tabs]

    rq, rk = proj(0), proj(1)
    q = [rope(sl(rq, c, h), *tabs[c]) for c, h in units]
    k = [rope(sl(rk, c, h), *ktabs[c]) for c, h in units]
    rv = proj(2)
    v = [sl(rv, c, h).astype(bf16) for c, h in units]
    att = [dot_nt(q[u].astype(bf16), k[u].astype(bf16)) for u in range(nu)]
    att = [(att[u] * dmask_ref[h]).astype(bf16) for u, (c, h) in enumerate(units)]
    o_ret = [_dot(att[u], v[u]) for u in range(nu)]
    qw = [seqs((q[u] * wq_ref[h]).astype(bf16)) for u, (c, h) in enumerate(units)]
    upd_ret = [ktv(seqs((k[u] * wk_ref[h]).astype(bf16)), seqs(v[u])) for u, (c, h) in enumerate(units)]

    hf = proj(5)
    lbs = [lb[:, h * D_HEAD:(h + 1) * D_HEAD] for h in range(N_HEADS)]
    f = [lbs[h] + (1.0 - lbs[h]) * jax.nn.sigmoid(sl(hf, c, h)) for c, h in units]
    lf = [jnp.log2(z) for z in f]
    kk = [1.0 - z for z in f]
    hq, hi = proj(4), proj(6)
    gq = [sl(hq, c, h) for c, h in units]
    gv = [sl(hi, c, h).astype(bf16) for c, h in units]
    split = []
    for z in lf:
        l1 = z.astype(bf16)
        r1 = z - l1.astype(f32)
        l2 = r1.astype(bf16)
        l3 = (r1 - l2.astype(f32)).astype(bf16)
        split.append(jnp.concatenate([l1, l2, l3], axis=1))
    cs = [_dot(tri_ref[...], z) for z in split]
    b = [z[:, :D_HEAD] + z[:, D_HEAD:2 * D_HEAD] + z[:, 2 * D_HEAD:] for z in cs]
    scores = _hgrn_scores(gq, kk, f, b, lvl_ref, dot_nt, T=T, tl=tl)
    o_hg = [_dot(scores[u], gv[u]) for u in range(nu)]
    qe = [seqs((gq[u] * jnp.exp2(b[u])).astype(bf16)) for u in range(nu)]
    b3 = [seqs(z) for z in b]
    bl = [z[:, tl - 1:tl, :] for z in b3]
    upd_hg = [ktv((seqs(kk[u]) * jnp.exp2(bl[u] - b3[u])).astype(bf16), seqs(gv[u])) for u in range(nu)]
    decay = []
    for z in bl:
        ez = jnp.exp2(z)
        d = [jnp.broadcast_to(ez[j], (D_HEAD, D_HEAD)).T for j in range(cb)]
        decay.append(d[0][None] if cb == 1 else jnp.stack(d))

    s_ret, s_hg = {}, {}
    for u, (c, h) in enumerate(units):
        s0 = seq0(c)
        first = BB > 1 or c == 0
        sp = sret_ref[s0:s0 + cb, h] if first else s_ret[h]
        sg = shg_ref[s0:s0 + cb, h] if first else s_hg[h]
        o_ret[u] = o_ret[u] + rows(qs(qw[u], sp.astype(bf16)))
        o_hg[u] = o_hg[u] + rows(qs(qe[u], sg.astype(bf16)))
        s_ret[h] = math.exp(LOG_GAMMA[h] * tl) * sp + upd_ret[u]
        s_hg[h] = decay[u] * sg + upd_hg[u]
        if BB > 1 or c == nch - 1:
            sret_ref[s0:s0 + cb, h] = s_ret[h]
            shg_ref[s0:s0 + cb, h] = s_hg[h]

    rg, hg = proj(3), proj(7)
    head_out = {}
    for u, (c, h) in enumerate(units):
        o = o_ret[u]
        mu = jnp.mean(o, axis=-1, keepdims=True)
        oc = o - mu
        var = jnp.mean(oc * oc, axis=-1, keepdims=True)
        head_out[(c, h)] = oc * lax.rsqrt(var + LN_EPS) * jax.nn.silu(sl(rg, c, h))
        og = o_hg[u]
        ms = jnp.mean(og * og, axis=-1, keepdims=True)
        head_out[(c, N_HEADS + h)] = og * lax.rsqrt(ms + LN_EPS) * nw * jax.nn.silu(sl(hg, c, h))
    chunk_out = [jnp.concatenate([head_out[(c, j)] for j in range(2 * N_HEADS)], axis=1).astype(bf16)
                 for c in range(nch)]
    mixed = chunk_out[0] if nch == 1 else jnp.concatenate(chunk_out, axis=0)
    m = _dot(mixed, wout_ref[...])
    y = _layernorm_rows(DEEPNORM_ALPHA * x2 + m, g_ref[...], be_ref[...])
    y_ref[...] = y.reshape(BB, TL, D_MODEL)


def _mixer0(x, c2, s2, w_in, w_out, lb_logits, norm_w, ln_g, ln_b, sret0, shg0, *, BB, TL):
    B, L, _ = x.shape
    R = BB * TL
    T = min(R, CHUNK_ROWS)
    if BB == 1:
        cb, tl = 1, T
    else:
        assert T % TL == 0
        cb, tl = T // TL, TL
    per_seq = sret0.shape[0] == B and B > 1
    st_block = (BB, N_HEADS, D_HEAD, D_HEAD)
    n_score_dots = N_HEADS * (R // T) * (2 + (tl - 1).bit_length()) if cb == 1 else 1
    kern = functools.partial(_mixer0_kernel, BB=BB, TL=TL, cb=cb, tl=tl)
    return pl.pallas_call(
        kern, grid=(B // BB, L // TL),
        in_specs=[
            pl.BlockSpec((BB, TL, D_MODEL), lambda b, t: (b, t, 0)),
            pl.BlockSpec((TL, D_HEAD), lambda b, t: (t, 0)),
            pl.BlockSpec((TL, D_HEAD), lambda b, t: (t, 0)),
            _const_spec(w_in.shape), _const_spec(w_out.shape), _const_spec(lb_logits.shape),
            _const_spec(norm_w.shape), _const_spec(ln_g.shape), _const_spec(ln_b.shape),
            _state_spec(st_block, per_seq), _state_spec(st_block, per_seq),
        ],
        out_specs=[
            pl.BlockSpec((BB, TL, D_MODEL), lambda b, t: (b, t, 0)),
            _state_spec(st_block, True), _state_spec(st_block, True),
        ],
        out_shape=[
            jax.ShapeDtypeStruct((B, L, D_MODEL), f32),
            jax.ShapeDtypeStruct((B, N_HEADS, D_HEAD, D_HEAD), f32),
            jax.ShapeDtypeStruct((B, N_HEADS, D_HEAD, D_HEAD), f32),
        ],
        scratch_shapes=[
            pltpu.VMEM((N_HEADS, T, T), f32),
            pltpu.VMEM((N_HEADS, T, D_HEAD), f32),
            pltpu.VMEM((N_HEADS, T, D_HEAD), f32),
            pltpu.VMEM((T, T), bf16),
            pltpu.VMEM((T, T), jnp.int32),
            pltpu.VMEM((n_score_dots, D_HEAD, T), bf16),
        ],
        compiler_params=_params(), name=f"mixer0_b{BB}_t{TL}",
    )(x, c2, s2, w_in, w_out, lb_logits, norm_w, ln_g, ln_b, sret0, shg0)


def _ffn_kernel(x_ref, wup_ref, cw_ref, cb_ref, wdn_ref, g_ref, be_ref, buf0_ref,
                y_ref, buf_ref, u_ref, *, BB, TL):
    R = BB * TL
    K = CONV_W_FFN - 1

    @pl.when(pl.program_id(1) == 0)
    def _init_state():
        buf_ref[...] = buf0_ref[...]

    x2 = x_ref[...].reshape(R, D_MODEL)
    xb = x2.astype(bf16)
    u = _dot(xb, wup_ref[:, :D_FF])
    v = _dot(xb, wup_ref[:, D_FF:])
    u_ref[:, SUBLANES - K:SUBLANES, :] = buf_ref[...]
    u_ref[:, SUBLANES:, :] = u.reshape(BB, TL, D_FF)
    cw = cw_ref[...]
    uc = cb_ref[...] + u_ref[:, SUBLANES - K:SUBLANES - K + TL, :] * cw[0:1]
    for j in range(1, CONV_W_FFN):
        uc = uc + u_ref[:, SUBLANES - K + j:SUBLANES - K + j + TL, :] * cw[j:j + 1]
    buf_ref[...] = u_ref[:, SUBLANES + TL - K:SUBLANES + TL, :]
    hmid = _gelu(uc.reshape(R, D_FF)) * v
    fo = _dot(hmid.astype(bf16), wdn_ref[...])
    y = _layernorm_rows(DEEPNORM_ALPHA * x2 + fo, g_ref[...], be_ref[...])
    y_ref[...] = y.reshape(BB, TL, D_MODEL)


def _ffn(x, w_up, conv_w, conv_b, w_down, ln_g, ln_b, buf0, *, layer, BB, TL):
    B, L, _ = x.shape
    per_seq = buf0.shape[0] == B and B > 1
    K = CONV_W_FFN - 1
    kern = functools.partial(_ffn_kernel, BB=BB, TL=TL)
    layer_spec = lambda rows, cols: pl.BlockSpec((rows, cols), lambda b, t: (layer, 0), pipeline_mode=pl.Buffered(1))
    return pl.pallas_call(
        kern, grid=(B // BB, L // TL),
        in_specs=[
            pl.BlockSpec((BB, TL, D_MODEL), lambda b, t: (b, t, 0)),
            layer_spec(D_MODEL, 2 * D_FF), _const_spec(conv_w.shape), _const_spec(conv_b.shape),
            layer_spec(D_FF, D_MODEL), _const_spec(ln_g.shape), _const_spec(ln_b.shape),
            _state_spec((BB, K, D_FF), per_seq),
        ],
        out_specs=[
            pl.BlockSpec((BB, TL, D_MODEL), lambda b, t: (b, t, 0)),
            _state_spec((BB, K, D_FF), True),
        ],
        out_shape=[
            jax.ShapeDtypeStruct((B, L, D_MODEL), f32),
            jax.ShapeDtypeStruct((B, K, D_FF), f32),
        ],
        scratch_shapes=[pltpu.VMEM((BB, SUBLANES + TL, D_FF), f32)],
        compiler_params=_params(), name=f"ffn_b{BB}_t{TL}",
    )(x, w_up, conv_w, conv_b, w_down, ln_g, ln_b, buf0)


def _ffn_cols_kernel(x_ref, wu_ref, wv_ref, cw_ref, cb_ref, wd_ref, g_ref, be_ref, buf0_ref,
                     y_ref, buf_ref, xb_ref, acc_ref, *, B, TL):
    c = pl.program_id(0)
    R = B * TL
    K = CONV_W_FFN - 1
    W = FFN_COL_BLOCK

    @pl.when(c == 0)
    def _first_step():
        xb_ref[...] = x_ref[...].reshape(R, D_MODEL).astype(bf16)
        acc_ref[...] = jnp.zeros_like(acc_ref)

    xb = xb_ref[...]
    u3 = _dot(xb, wu_ref[...]).reshape(B, TL, W)
    v = _dot(xb, wv_ref[...])
    uc = _causal_conv(u3, buf0_ref[...], cw_ref, cb_ref, slice(None))
    buf_ref[...] = u3[:, TL - K:, :]
    hmid = (_gelu(uc.reshape(R, W)) * v).astype(bf16)
    acc_ref[...] += _dot(hmid, wd_ref[...])

    @pl.when(c == pl.num_programs(0) - 1)
    def _last_step():
        x2 = x_ref[...].reshape(R, D_MODEL)
        y = _layernorm_rows(DEEPNORM_ALPHA * x2 + acc_ref[...], g_ref[...], be_ref[...])
        y_ref[...] = y.reshape(B, TL, D_MODEL)


def _ffn_cols(x, w_up, conv_w, conv_b, w_down, ln_g, ln_b, buf0, *, layer):
    B, TL, _ = x.shape
    assert buf0.shape[0] == B
    K = CONV_W_FFN - 1
    W = FFN_COL_BLOCK
    nc = D_FF // W
    whole = lambda shape: pl.BlockSpec(shape, lambda c: (0,) * len(shape), pipeline_mode=pl.Buffered(1))
    return pl.pallas_call(
        functools.partial(_ffn_cols_kernel, B=B, TL=TL), grid=(nc,),
        in_specs=[
            whole(x.shape),
            pl.BlockSpec((D_MODEL, W), lambda c: (layer, c)),
            pl.BlockSpec((D_MODEL, W), lambda c: (layer, nc + c)),
            pl.BlockSpec((CONV_W_FFN, W), lambda c: (0, c)), pl.BlockSpec((1, W), lambda c: (0, c)),
            pl.BlockSpec((W, D_MODEL), lambda c: (layer * nc + c, 0)),
            whole(ln_g.shape), whole(ln_b.shape),
            pl.BlockSpec((B, K, W), lambda c: (0, 0, c)),
        ],
        out_specs=[
            pl.BlockSpec((B, TL, D_MODEL), lambda c: (0, 0, 0)),
            pl.BlockSpec((B, K, W), lambda c: (0, 0, c)),
        ],
        out_shape=[
            jax.ShapeDtypeStruct((B, TL, D_MODEL), f32),
            jax.ShapeDtypeStruct((B, K, D_FF), f32),
        ],
        scratch_shapes=[
            pltpu.VMEM((B * TL, D_MODEL), bf16),
            pltpu.VMEM((B * TL, D_MODEL), f32),
        ],
        compiler_params=pltpu.CompilerParams(dimension_semantics=("arbitrary",), vmem_limit_bytes=VMEM_LIMIT_BYTES),
        name=f"ffn_cols_b{B}_t{TL}",
    )(x, w_up, w_up, conv_w, conv_b, w_down, ln_g, ln_b, buf0)


def _rglru_kernel(x_ref, win_ref, cw_ref, cb_ref, wga_ref, bga_ref, wgx_ref, bgx_ref, lam_ref,
                  wout_ref, g_ref, be_ref, h0_ref, buf0_ref,
                  y_ref, h_ref, buf_ref, *, BB, TL):
    R = BB * TL
    K = CONV_W_LRU - 1
    G = TL // SUBLANES

    @pl.when(pl.program_id(1) == 0)
    def _init_state():
        h_ref[...] = h0_ref[...]
        buf_ref[...] = buf0_ref[...]

    x2 = x_ref[...].reshape(R, D_MODEL)
    xb = x2.astype(bf16)
    W = LRU_STEP_COLS
    nchunk = D_RNN // W
    carry = buf_ref[...]
    new_carry = []
    sub = lax.broadcasted_iota(jnp.int32, (R // SUBLANES, SUBLANES, W), 1)
    h0 = h_ref[...]

    def project(ci):
        c0 = ci * W
        return _dot(xb, win_ref[:, c0:c0 + W]), _dot(xb, win_ref[:, D_RNN + c0:D_RNN + c0 + W])

    def recur(ci, gate, rnn):
        c0 = ci * W
        cs = slice(c0, c0 + W)
        rnn3 = rnn.reshape(BB, TL, W)
        xc = _causal_conv(rnn3, carry[:, :, cs], cw_ref, cb_ref, cs).reshape(R, W)
        new_carry.append(rnn3[:, TL - K:, :])
        xcb = xc.astype(bf16)
        blocks = [(j, c0 // LRU_BLOCK + j) for j in range(W // LRU_BLOCK)]
        ga = jnp.concatenate([_dot(xcb[:, j * LRU_BLOCK:(j + 1) * LRU_BLOCK], wga_ref[n]) for j, n in blocks], axis=1)
        gx = jnp.concatenate([_dot(xcb[:, j * LRU_BLOCK:(j + 1) * LRU_BLOCK], wgx_ref[n]) for j, n in blocks], axis=1)
        rgate = jax.nn.sigmoid(ga + bga_ref[:, cs])
        igate = jax.nn.sigmoid(gx + bgx_ref[:, cs])
        nla = rgate * (-LRU_C * jax.nn.log_sigmoid(lam_ref[:, cs]))
        a = jnp.exp(-nla)
        w1 = jnp.tanh(nla) * (1.0 + a * a)
        bv = jnp.where(w1 > 0.0, w1 * lax.rsqrt(w1), 0.0) * (igate * xc)
        a4 = a.reshape(R // SUBLANES, SUBLANES, W)
        b4 = bv.reshape(R // SUBLANES, SUBLANES, W)
        for d in (1, 2, 4):
            keep = sub >= d
            b4 = jnp.where(keep, a4 * pltpu.roll(b4, d, 1) + b4, b4)
            a4 = jnp.where(keep, a4 * pltpu.roll(a4, d, 1), a4)
        a5 = a4.reshape(BB, G, SUBLANES, W)
        b5 = b4.reshape(BB, G, SUBLANES, W)
        hc = h0[:, :, cs]
        tiles = []
        for gi in range(G):
            hg = a5[:, gi] * hc + b5[:, gi]
            hc = hg[:, SUBLANES - 1:SUBLANES, :]
            tiles.append(hg)
        hseq = tiles[0] if G == 1 else jnp.concatenate(tiles, axis=1)
        return (_gelu(gate) * hseq.reshape(R, W)).astype(bf16), hc

    nxt = project(0)
    ys, h_last = [], []
    for ci in range(nchunk):
        cur, nxt = nxt, (project(ci + 1) if ci + 1 < nchunk else None)
        yv, hc = recur(ci, *cur)
        ys.append(yv)
        h_last.append(hc)
    buf_ref[...] = jnp.concatenate(new_carry, axis=2)
    h_ref[...] = jnp.concatenate(h_last, axis=2)
    yv = jnp.concatenate(ys, axis=1)
    groups = LRU_OUT_ROW_GROUPS if R % (LRU_OUT_ROW_GROUPS * 2 * SUBLANES) == 0 else 1
    step = R // groups
    outs = [_layernorm_rows(DEEPNORM_ALPHA * x2[i * step:(i + 1) * step]
                            + _dot(yv[i * step:(i + 1) * step], wout_ref[...]), g_ref[...], be_ref[...])
            for i in range(groups)]
    y = outs[0] if groups == 1 else jnp.concatenate(outs, axis=0)
    y_ref[...] = y.reshape(BB, TL, D_MODEL)


def _rglru(x, w_in, conv_w, conv_b, wga, bga, wgx, bgx, lam, w_out, ln_g, ln_b, h0, buf0, *, BB, TL):
    B, L, _ = x.shape
    per_seq = h0.shape[0] == B and B > 1
    K = CONV_W_LRU - 1
    kern = functools.partial(_rglru_kernel, BB=BB, TL=TL)
    consts = (w_in, conv_w, conv_b, wga, bga, wgx, bgx, lam, w_out, ln_g, ln_b)
    return pl.pallas_call(
        kern, grid=(B // BB, L // TL),
        in_specs=[pl.BlockSpec((BB, TL, D_MODEL), lambda b, t: (b, t, 0))]
        + [_const_spec(c.shape) for c in consts]
        + [_state_spec((BB, 1, D_RNN), per_seq), _state_spec((BB, K, D_RNN), per_seq)],
        out_specs=[
            pl.BlockSpec((BB, TL, D_MODEL), lambda b, t: (b, t, 0)),
            _state_spec((BB, 1, D_RNN), True), _state_spec((BB, K, D_RNN), True),
        ],
        out_shape=[
            jax.ShapeDtypeStruct((B, L, D_MODEL), f32),
            jax.ShapeDtypeStruct((B, 1, D_RNN), f32),
            jax.ShapeDtypeStruct((B, K, D_RNN), f32),
        ],
        compiler_params=_params(), name=f"rglru_b{BB}_t{TL}",
    )(x, *consts, h0, buf0)


def _rglru_tm_kernel(x_ref, win_ref, cw_ref, cb_ref, wga_ref, bga_ref, wgx_ref, bgx_ref, lam_ref,
                     wout_ref, g_ref, be_ref, h0_ref, buf0_ref,
                     y_ref, h_ref, buf_ref, perm_ref, permt_ref, hrun_ref, crun_ref, *, TL):
    B = SUBLANES
    NPART = LRU_TM_PARTS
    TP = TL // NPART
    RP = B * TP
    K = CONV_W_LRU - 1
    W = LRU_STEP_COLS
    nchunk = D_RNN // W
    step = pl.program_id(0)

    @pl.when(step == 0)
    def _init():
        r = lax.broadcasted_iota(jnp.int32, (RP, RP), 0)
        c = lax.broadcasted_iota(jnp.int32, (RP, RP), 1)
        hit = ((r & (B - 1)) * TP + (r >> LOG2_SUBLANES)) == c
        perm_ref[...] = jnp.where(hit, 1.0, 0.0).astype(bf16)
        hit_t = ((c & (B - 1)) * TP + (c >> LOG2_SUBLANES)) == r
        permt_ref[...] = jnp.where(hit_t, 1.0, 0.0).astype(bf16)
        hrun_ref[...] = jnp.broadcast_to(h0_ref[0], (B, D_RNN))
        for k in range(K):
            crun_ref[k] = jnp.broadcast_to(buf0_ref[0, k:k + 1, :], (B, D_RNN))

    def head(p):
        x2 = x_ref[:, p * TP:(p + 1) * TP, :].reshape(RP, D_MODEL)
        return x2, _dot(perm_ref[...], x2.astype(bf16)).astype(bf16)

    def project(xt, ci):
        c0 = ci * W
        return _dot(xt, win_ref[:, c0:c0 + W]), _dot(xt, win_ref[:, D_RNN + c0:D_RNN + c0 + W])

    def body(ci, gate, rnn):
        c0 = ci * W
        cs = slice(c0, c0 + W)
        rnn = rnn.reshape(TP, B, W)
        ext = jnp.concatenate([crun_ref[:, :, cs], rnn], axis=0)
        xc = cb_ref[:, cs] + ext[0:TP] * cw_ref[0:1, cs]
        for j in range(1, CONV_W_LRU):
            xc = xc + ext[j:j + TP] * cw_ref[j:j + 1, cs]
        crun_ref[:, :, cs] = rnn[TP - K:]
        xc = xc.reshape(RP, W)
        xcb = xc.astype(bf16)
        blocks = [(j, c0 // LRU_BLOCK + j) for j in range(W // LRU_BLOCK)]
        ga = jnp.concatenate([_dot(xcb[:, j * LRU_BLOCK:(j + 1) * LRU_BLOCK], wga_ref[n]) for j, n in blocks], axis=1)
        gx = jnp.concatenate([_dot(xcb[:, j * LRU_BLOCK:(j + 1) * LRU_BLOCK], wgx_ref[n]) for j, n in blocks], axis=1)
        rgate = jax.nn.sigmoid(ga + bga_ref[:, cs])
        igate = jax.nn.sigmoid(gx + bgx_ref[:, cs])
        log_a = LRU_C * rgate * jax.nn.log_sigmoid(lam_ref[:, cs])
        a = jnp.exp(log_a).reshape(TP, B, W)
        th = jnp.tanh(log_a)
        bv = (jnp.sqrt(-2.0 * th / (1.0 - th)) * (igate * xc)).reshape(TP, B, W)
        h = hrun_ref[:, cs]
        hs = []
        for t in range(TP):
            h = a[t] * h + bv[t]
            hs.append(h)
        hrun_ref[:, cs] = h
        return (_gelu(gate) * jnp.stack(hs, axis=0).reshape(RP, W)).astype(bf16)

    def tail(p, x2, ys):
        yv = _dot(permt_ref[...], jnp.concatenate(ys, axis=1)).astype(bf16)
        y = _layernorm_rows(DEEPNORM_ALPHA * x2 + _dot(yv, wout_ref[...]), g_ref[...], be_ref[...])
        y_ref[:, p * TP:(p + 1) * TP, :] = y.reshape(B, TP, D_MODEL)

    x2s, xts, nxt = {}, {}, None
    x2s[0], xts[0] = head(0)
    nxt = project(xts[0], 0)
    for p in range(NPART):
        if p + 1 < NPART:
            x2s[p + 1], xts[p + 1] = head(p + 1)
        ys = []
        for ci in range(nchunk):
            cur = nxt
            if ci + 1 < nchunk:
                nxt = project(xts[p], ci + 1)
            elif p + 1 < NPART:
                nxt = project(xts[p + 1], 0)
            ys.append(body(ci, *cur))
        tail(p, x2s[p], ys)

    @pl.when(step == pl.num_programs(0) - 1)
    def _emit_state():
        h_ref[:, 0, :] = hrun_ref[...]
        for k in range(K):
            buf_ref[:, k, :] = crun_ref[k]


def _rglru_tm(x, w_in, conv_w, conv_b, wga, bga, wgx, bgx, lam, w_out, ln_g, ln_b, h0, buf0, *, TL):
    B, L, _ = x.shape
    assert B == SUBLANES and h0.shape[0] == 1 and buf0.shape[0] == 1
    K = CONV_W_LRU - 1
    R = B * TL // LRU_TM_PARTS
    consts = (w_in, conv_w, conv_b, wga, bga, wgx, bgx, lam, w_out, ln_g, ln_b, h0, buf0)
    const_spec = lambda shape: pl.BlockSpec(shape, lambda t: (0,) * len(shape), pipeline_mode=pl.Buffered(1))
    return pl.pallas_call(
        functools.partial(_rglru_tm_kernel, TL=TL), grid=(L // TL,),
        in_specs=[pl.BlockSpec((B, TL, D_MODEL), lambda t: (0, t, 0))] + [const_spec(c.shape) for c in consts],
        out_specs=[
            pl.BlockSpec((B, TL, D_MODEL), lambda t: (0, t, 0)),
            pl.BlockSpec((B, 1, D_RNN), lambda t: (0, 0, 0)), pl.BlockSpec((B, K, D_RNN), lambda t: (0, 0, 0)),
        ],
        out_shape=[
            jax.ShapeDtypeStruct((B, L, D_MODEL), f32),
            jax.ShapeDtypeStruct((B, 1, D_RNN), f32),
            jax.ShapeDtypeStruct((B, K, D_RNN), f32),
        ],
        scratch_shapes=[
            pltpu.VMEM((R, R), bf16), pltpu.VMEM((R, R), bf16),
            pltpu.VMEM((B, D_RNN), f32),
            pltpu.VMEM((K, B, D_RNN), f32),
        ],
        compiler_params=pltpu.CompilerParams(dimension_semantics=("arbitrary",), vmem_limit_bytes=VMEM_LIMIT_BYTES),
        name=f"rglru_tm_t{TL}",
    )(x, *consts)


def _rope_tables(pos0, length):
    half = D_HEAD // 2
    pos = pos0 + jnp.arange(length, dtype=jnp.int32)
    inv = ROPE_BASE ** (-jnp.arange(half, dtype=f32) / half)
    ang = pos.astype(f32)[:, None] * inv[None, :]
    cos, sin = jnp.cos(ang), jnp.sin(ang)
    return jnp.concatenate([cos, cos], axis=1), jnp.concatenate([-sin, sin], axis=1)


def _trunk(x, pos0, states, p, *, BB, TL, BB0, TL0=None, lru_tm_steps=None):
    s_ret, s_hgrn, h_lru, buf_lru, buf_ffn0, buf_ffn1 = states
    c2, s2 = _rope_tables(pos0, x.shape[1])
    x, s_ret, s_hgrn = _mixer0(x, c2, s2, p['w_in_ab'], p['w_out_ab'], p['hgrn_lb_logits'], p['hgrn_norm_w'],
                               p['ln_mix_g'][0], p['ln_mix_b'][0], s_ret, s_hgrn, BB=BB0, TL=TL0 or TL)
    short = x.shape[1] == TL and buf_ffn0.shape[0] == x.shape[0]
    ffn_args0 = (x, p['w_ffn_up'], p['ffn_conv_w'][0], p['ffn_conv_b'][0], p['w_ffn_down'],
                 p['ln_ffn_g'][0], p['ln_ffn_b'][0], buf_ffn0)
    x, buf_ffn0 = _ffn_cols(*ffn_args0, layer=0) if short else _ffn(*ffn_args0, layer=0, BB=BB, TL=TL)
    lru_args = (x, p['w_in_c'], p['conv_w_c'], p['conv_b_c'], p['w_gate_a'], p['b_gate_a'], p['w_gate_x'],
                p['b_gate_x'], p['lru_lambda'], p['w_out_c'], p['ln_mix_g'][1], p['ln_mix_b'][1], h_lru, buf_lru)
    if lru_tm_steps is None:
        x, h_lru, buf_lru = _rglru(*lru_args, BB=BB, TL=TL)
    else:
        x, h_lru, buf_lru = _rglru_tm(*lru_args, TL=lru_tm_steps)
    ffn_args1 = (x, p['w_ffn_up'], p['ffn_conv_w'][1], p['ffn_conv_b'][1], p['w_ffn_down'],
                 p['ln_ffn_g'][1], p['ln_ffn_b'][1], buf_ffn1)
    x, buf_ffn1 = _ffn_cols(*ffn_args1, layer=1) if short else _ffn(*ffn_args1, layer=1, BB=BB, TL=TL)
    return x, (s_ret, s_hgrn, h_lru, buf_lru, buf_ffn0, buf_ffn1)


def kernel(x_prompt, x_sample, state_ret, state_hgrn, state_rglru_h, state_rglru_conv, state_ffn_conv, meta_tokens, w_in_ab, w_out_ab, hgrn_lb_logits, hgrn_norm_w, w_in_c, conv_w_c, conv_b_c, w_gate_a, b_gate_a, w_gate_x, b_gate_x, lru_lambda, w_out_c, ln_mix_g, ln_mix_b, ln_ffn_g, ln_ffn_b, w_ffn_up, ffn_conv_w, ffn_conv_b, w_ffn_down):
    row = lambda z: z.reshape(1, -1)
    wb = [w.astype(bf16) for w in (w_in_ab, w_out_ab, w_in_c, w_gate_a, w_gate_x, w_out_c,
                                   w_ffn_up.reshape(DEPTH * D_MODEL, 2 * D_FF),
                                   w_ffn_down.reshape(DEPTH * D_FF, D_MODEL))]
    p = dict(
        w_in_ab=wb[0], w_out_ab=wb[1],
        hgrn_lb_logits=hgrn_lb_logits, hgrn_norm_w=row(hgrn_norm_w),
        w_in_c=wb[2], conv_w_c=conv_w_c, conv_b_c=row(conv_b_c),
        w_gate_a=wb[3], b_gate_a=row(b_gate_a),
        w_gate_x=wb[4], b_gate_x=row(b_gate_x),
        lru_lambda=row(lru_lambda), w_out_c=wb[5],
        ln_mix_g=[row(ln_mix_g[i]) for i in range(DEPTH)], ln_mix_b=[row(ln_mix_b[i]) for i in range(DEPTH)],
        ln_ffn_g=[row(ln_ffn_g[i]) for i in range(DEPTH)], ln_ffn_b=[row(ln_ffn_b[i]) for i in range(DEPTH)],
        w_ffn_up=wb[6], w_ffn_down=wb[7],
        ffn_conv_w=[ffn_conv_w[i] for i in range(DEPTH)],
        ffn_conv_b=[row(ffn_conv_b[i]) for i in range(DEPTH)],
    )
    dt = x_prompt.dtype
    zero_states = (
        jnp.zeros((1, N_HEADS, D_HEAD, D_HEAD), dt), jnp.zeros((1, N_HEADS, D_HEAD, D_HEAD), dt),
        jnp.zeros((1, 1, D_RNN), dt), jnp.zeros((1, CONV_W_LRU - 1, D_RNN), dt),
        jnp.zeros((1, CONV_W_FFN - 1, D_FF), dt), jnp.zeros((1, CONV_W_FFN - 1, D_FF), dt),
    )
    _, meta_states = _trunk(meta_tokens.astype(dt)[None], 0, zero_states, p, BB=1, TL=N_META, BB0=1)
    y_prompt, ps = _trunk(x_prompt, N_META, meta_states, p, BB=1, TL=PROMPT_BLOCK_STEPS, BB0=1,
                          TL0=PROMPT_MIXER0_BLOCK_STEPS, lru_tm_steps=LRU_TM_BLOCK_STEPS)
    sample_states = (state_ret, state_hgrn, state_rglru_h[:, None, :], state_rglru_conv,
                     state_ffn_conv[0], state_ffn_conv[1])
    y_sample, ss = _trunk(x_sample, PAST_LEN, sample_states, p, BB=32, TL=x_sample.shape[1], BB0=16)
    return (y_prompt, y_sample, ps[0], ss[0], ps[1], ss[1], ps[2][:, 0, :], ss[2][:, 0, :], ps[3], ss[3],
            jnp.stack([ps[4], ps[5]]), jnp.stack([ss[4], ss[5]]))
```

```python
import functools
import math

import jax
import jax.numpy as jnp
from jax import lax
from jax.experimental import pallas as pl
from jax.experimental.pallas import tpu as pltpu

f32 = jnp.float32
bf16 = jnp.bfloat16

D_MODEL = 1024
N_META = 16
PAST_LEN = 16384
N_HEADS = 4
D_HEAD = 128
SEG = N_HEADS * D_HEAD
ROPE_BASE = 10000.0
D_RNN = 1024
N_LRU_BLOCKS = 8
LRU_BLOCK = D_RNN // N_LRU_BLOCKS
CONV_W_LRU = 4
LRU_C = 8.0
D_FF = 2816
CONV_W_FFN = 3
LN_EPS = 1e-5
DEPTH = 2
DEEPNORM_ALPHA = (2.0 * DEPTH) ** 0.25
LOG_GAMMA = tuple(math.log1p(-(2.0 ** (-5.0 - h))) for h in range(N_HEADS))

SUBLANES = 8
CHUNK_ROWS = 128
LRU_STEP_COLS = 256
LRU_OUT_ROW_GROUPS = 2
PROMPT_BLOCK_STEPS = 1024
PROMPT_MIXER0_BLOCK_STEPS = 1024
LRU_TM_BLOCK_STEPS = 128
LRU_TM_PARTS = 4
LOG2_SUBLANES = SUBLANES.bit_length() - 1
VMEM_LIMIT_BYTES = 56 * 1024 * 1024
BIG_BLOCK_VMEM_LIMIT_BYTES = 62 * 1024 * 1024

_GELU_K1 = -2.0 * math.log2(math.e) * math.sqrt(2.0 / math.pi)
_GELU_K3 = _GELU_K1 * 0.044715


def _gelu(x):
    return x / (1.0 + jnp.exp2(x * (_GELU_K1 + _GELU_K3 * (x * x))))


def _dot(a, b):
    return jnp.dot(a, b, preferred_element_type=f32)


def _dot_nt(a, b, t_ref=None):
    if t_ref is None:
        return lax.dot_general(a, b, (((1,), (1,)), ((), ())), preferred_element_type=f32)
    t_ref[...] = b.T
    return jnp.dot(a, t_ref[...], preferred_element_type=f32)


def _layernorm_rows(y, g, b):
    mu = jnp.mean(y, axis=-1, keepdims=True)
    yc = y - mu
    var = jnp.mean(yc * yc, axis=-1, keepdims=True)
    return yc * lax.rsqrt(var + LN_EPS) * g + b


def _causal_conv(x3, carry, w_ref, b_ref, cs):
    BB, TL, W = x3.shape
    K = carry.shape[1]
    w = [w_ref[j:j + 1, cs] for j in range(K + 1)]

    def shift(p, j):
        bnd = w[0] * carry[:, K - j:K - j + 1, :]
        for i in range(1, j):
            bnd = bnd + w[i] * carry[:, K - j + i:K - j + i + 1, :]
        if TL == SUBLANES:
            t = lax.broadcasted_iota(jnp.int32, p.shape, 1)
            return jnp.where(t == 0, bnd, pltpu.roll(p, 1, 1))
        assert BB == 1
        rolled = pltpu.roll(p.reshape(TL, W), 1, 0)
        t = lax.broadcasted_iota(jnp.int32, (SUBLANES, W), 0)
        head = jnp.where(t == 0, bnd.reshape(1, W), rolled[:SUBLANES])
        return jnp.concatenate([head, rolled[SUBLANES:]], axis=0).reshape(BB, TL, W)

    p = w[0] * x3
    for j in range(1, K + 1):
        p = w[j] * x3 + shift(p, j)
    return b_ref[:, cs] + p


def _const_spec(shape):
    nd = len(shape)
    return pl.BlockSpec(shape, lambda b, t: (0,) * nd, pipeline_mode=pl.Buffered(1))


def _state_spec(block, per_seq):
    nd = len(block)
    if per_seq:
        return pl.BlockSpec(block, lambda b, t: (b,) + (0,) * (nd - 1))
    return pl.BlockSpec(block, lambda b, t: (0,) * nd)


def _params(vmem_limit_bytes=VMEM_LIMIT_BYTES):
    return pltpu.CompilerParams(dimension_semantics=("arbitrary", "arbitrary"),
                                vmem_limit_bytes=vmem_limit_bytes)


def _mixer0_init_tables(dmask_ref, wq_ref, wk_ref, tri_ref, lvl_ref, *, cb, tl):
    T = cb * tl
    tl_shift = tl.bit_length() - 1
    r = lax.broadcasted_iota(jnp.int32, (T, T), 0)
    c = lax.broadcasted_iota(jnp.int32, (T, T), 1)
    same = (r >> tl_shift) == (c >> tl_shift)
    rel = (r & (tl - 1)) - (c & (tl - 1))
    causal = same & (rel >= 0)
    relf = jnp.maximum(rel, 0).astype(f32)
    tri_ref[...] = jnp.where(causal, 1.0, 0.0).astype(bf16)
    lvl = jnp.where(r == c, 0, -1)
    s, li = 1, 1
    while s < tl:
        blk = (r >> li) == (c >> li)
        hit = blk & ((r & (2 * s - 1)) >= s) & ((c & (2 * s - 1)) < s)
        lvl = jnp.where(hit, li, lvl)
        s, li = 2 * s, li + 1
    lvl_ref[...] = lvl
    tr = (lax.broadcasted_iota(jnp.int32, (T, D_HEAD), 0) & (tl - 1)).astype(f32)
    for h in range(N_HEADS):
        lg = LOG_GAMMA[h]
        dmask_ref[h] = jnp.where(causal, jnp.exp(lg * relf), 0.0)
        wq_ref[h] = jnp.exp(lg * (tr + 1.0))
        wk_ref[h] = jnp.exp(lg * ((tl - 1.0) - tr))


def _hgrn_scores(gqs, kks, fs, bs, lvl_ref, dot_nt, *, T, tl):
    n = len(gqs)
    row = lax.broadcasted_iota(jnp.int32, (T, D_HEAD), 0)
    ntile = T // SUBLANES

    def owned(li, g):
        return lvl_ref[g * SUBLANES:(g + 1) * SUBLANES, :] == li

    full = [dot_nt(gqs[u].astype(bf16), kks[u].astype(bf16)) for u in range(n)]
    tiles = [[jnp.where(owned(0, g), full[u][g * SUBLANES:(g + 1) * SUBLANES], 0.0) for g in range(ntile)]
             for u in range(n)]

    def take(u, li, p, first_tile, n_tiles, p_row0):
        for g in range(n_tiles):
            t = first_tile + g
            tiles[u][t] = jnp.where(owned(li, t), p[p_row0 + g * SUBLANES:p_row0 + (g + 1) * SUBLANES], tiles[u][t])

    s, li = 1, 1
    while s < tl:
        nblk = T // (2 * s)
        if s < SUBLANES:
            up = (row & (2 * s - 1)) >= s
            zs = []
            for u in range(n):
                gq, kk, f, b = gqs[u], kks[u], fs[u], bs[u]
                if s == 1:
                    z = jnp.where(up, gq * f, kk)
                elif s == 2:
                    w = row & 3
                    e = jnp.where(w == 0, pltpu.roll(f, T - 1, 0),
                                  jnp.where(w == 1, 1.0, jnp.where(w == 2, f, f * pltpu.roll(f, 1, 0))))
                    z = jnp.where(up, gq, kk) * e
                else:
                    parts = [jnp.abs(b[m * 2 * s:(m + 1) * 2 * s, :] - b[m * 2 * s + s - 1:m * 2 * s + s, :])
                             for m in range(nblk)]
                    z = jnp.where(up, gq, kk) * jnp.exp2(-jnp.concatenate(parts, axis=0))
                zs.append(z.astype(bf16))
            ps = [dot_nt(zb, zb) for zb in zs]
            for u in range(n):
                take(u, li, ps[u], 0, ntile, 0)
        else:
            zs, qus = [], []
            for u in range(n):
                gq, kk, b = gqs[u], kks[u], bs[u]
                both, upper = [], []
                for m in range(nblk):
                    r0 = m * 2 * s
                    beta = b[r0 + s - 1:r0 + s, :]
                    k_lo = kk[r0:r0 + s] * jnp.exp2(beta - b[r0:r0 + s])
                    q_hi = gq[r0 + s:r0 + 2 * s] * jnp.exp2(b[r0 + s:r0 + 2 * s] - beta)
                    both += [k_lo, q_hi]
                    upper.append(q_hi)
                zs.append(jnp.concatenate(both, axis=0).astype(bf16))
                qus.append((upper[0] if nblk == 1 else jnp.concatenate(upper, axis=0)).astype(bf16))
            ps = [dot_nt(qus[u], zs[u]) for u in range(n)]
            for u in range(n):
                for m in range(nblk):
                    take(u, li, ps[u], (m * 2 * s + s) // SUBLANES, s // SUBLANES, m * s)
        s, li = 2 * s, li + 1
    return [jnp.concatenate(tiles[u], axis=0).astype(bf16) for u in range(n)]


def _mixer0_kernel(x_ref, c2_ref, s2_ref, win_ref, wout_ref, lbl_ref, nw_ref, g_ref, be_ref,
                   sret0_ref, shg0_ref, y_ref, sret_ref, shg_ref,
                   dmask_ref, wq_ref, wk_ref, tri_ref, lvl_ref, kt_ref, *, BB, TL, cb, tl):
    T = cb * tl
    R = BB * TL
    nch = R // T
    slots = iter(range(kt_ref.shape[0]))

    def dot_nt(a, b):
        return _dot_nt(a, b, kt_ref.at[next(slots)] if (cb == 1 and T == CHUNK_ROWS) else None)

    @pl.when((pl.program_id(0) == 0) & (pl.program_id(1) == 0))
    def _init_tables():
        _mixer0_init_tables(dmask_ref, wq_ref, wk_ref, tri_ref, lvl_ref, cb=cb, tl=tl)

    @pl.when(pl.program_id(1) == 0)
    def _init_state():
        sret_ref[...] = sret0_ref[...]
        shg_ref[...] = shg0_ref[...]

    x2 = x_ref[...].reshape(R, D_MODEL)
    xb = x2.astype(bf16)

    def proj(i):
        return _dot(xb, win_ref[:, i * SEG:(i + 1) * SEG])

    lbl = lbl_ref[...]
    le = jnp.exp(lbl - jnp.max(lbl, axis=0, keepdims=True))
    lb = le[0:1] / jnp.sum(le, axis=0, keepdims=True)
    nw = nw_ref[...]

    def seqs(z):
        return z.reshape(cb, tl, D_HEAD)

    def rows(z3):
        return z3.reshape(T, D_HEAD)

    def qs(q3, s3):
        if cb == 1:
            return _dot(q3[0], s3[0])[None]
        return jnp.einsum('bqd,bdv->bqv', q3, s3, preferred_element_type=f32)

    def ktv(k3, v3):
        if cb == 1:
            return lax.dot_general(k3[0], v3[0], (((0,), (0,)), ((), ())),
                                   preferred_element_type=f32)[None]
        return jnp.einsum('btd,btv->bdv', k3, v3, preferred_element_type=f32)

    units = [(c, h) for c in range(nch) for h in range(N_HEADS)]
    nu = len(units)

    def sl(z, c, h):
        return z[c * T:(c + 1) * T, h * D_HEAD:(h + 1) * D_HEAD]

    def seq0(c):
        return 0 if BB == 1 else c * cb

    def rope_tables(c):
        ts = slice(c * tl, (c + 1) * tl) if BB == 1 else slice(0, tl)
        return c2_ref[ts, :][None], s2_ref[ts, :][None]

    def rope(z, cos2, sin2):
        return rows(seqs(z) * cos2 + seqs(pltpu.roll(z, D_HEAD // 2, 1)) * sin2)

    tabs = [rope_tables(c) for c in range(nch)]
    ktabs = [(c2 * (D_HEAD ** -0.5), s2 * (D_HEAD ** -0.5)) for c2, s2 in tabs]

    rq, rk = proj(0), proj(1)
    q = [rope(sl(rq, c, h), *tabs[c]) for c, h in units]
    k = [rope(sl(rk, c, h), *ktabs[c]) for c, h in units]
    rv = proj(2)
    v = [sl(rv, c, h).astype(bf16) for c, h in units]
    att = [dot_nt(q[u].astype(bf16), k[u].astype(bf16)) for u in range(nu)]
    att = [(att[u] * dmask_ref[h]).astype(bf16) for u, (c, h) in enumerate(units)]
    o_ret = [_dot(att[u], v[u]) for u in range(nu)]
    qw = [seqs((q[u] * wq_ref[h]).astype(bf16)) for u, (c, h) in enumerate(units)]
    upd_ret = [ktv(seqs((k[u] * wk_ref[h]).astype(bf16)), seqs(v[u])) for u, (c, h) in enumerate(units)]

    hf = proj(5)
    lbs = [lb[:, h * D_HEAD:(h + 1) * D_HEAD] for h in range(N_HEADS)]
    f = [lbs[h] + (1.0 - lbs[h]) * jax.nn.sigmoid(sl(hf, c, h)) for c, h in units]
    lf = [jnp.log2(z) for z in f]
    kk = [1.0 - z for z in f]
    hq, hi = proj(4), proj(6)
    gq = [sl(hq, c, h) for c, h in units]
    gv = [sl(hi, c, h).astype(bf16) for c, h in units]
    split = []
    for z in lf:
        l1 = z.astype(bf16)
        r1 = z - l1.astype(f32)
        l2 = r1.astype(bf16)
        l3 = (r1 - l2.astype(f32)).astype(bf16)
        split.append(jnp.concatenate([l1, l2, l3], axis=1))
    cs = [_dot(tri_ref[...], z) for z in split]
    b = [z[:, :D_HEAD] + z[:, D_HEAD:2 * D_HEAD] + z[:, 2 * D_HEAD:] for z in cs]
    scores = _hgrn_scores(gq, kk, f, b, lvl_ref, dot_nt, T=T, tl=tl)
    o_hg = [_dot(scores[u], gv[u]) for u in range(nu)]
    qe = [seqs((gq[u] * jnp.exp2(b[u])).astype(bf16)) for u in range(nu)]
    b3 = [seqs(z) for z in b]
    bl = [z[:, tl - 1:tl, :] for z in b3]
    upd_hg = [ktv((seqs(kk[u]) * jnp.exp2(bl[u] - b3[u])).astype(bf16), seqs(gv[u])) for u in range(nu)]
    decay = []
    for z in bl:
        ez = jnp.exp2(z)
        d = [jnp.broadcast_to(ez[j], (D_HEAD, D_HEAD)).T for j in range(cb)]
        decay.append(d[0][None] if cb == 1 else jnp.stack(d))

    s_ret, s_hg = {}, {}
    for u, (c, h) in enumerate(units):
        s0 = seq0(c)
        first = BB > 1 or c == 0
        sp = sret_ref[s0:s0 + cb, h] if first else s_ret[h]
        sg = shg_ref[s0:s0 + cb, h] if first else s_hg[h]
        o_ret[u] = o_ret[u] + rows(qs(qw[u], sp.astype(bf16)))
        o_hg[u] = o_hg[u] + rows(qs(qe[u], sg.astype(bf16)))
        s_ret[h] = math.exp(LOG_GAMMA[h] * tl) * sp + upd_ret[u]
        s_hg[h] = decay[u] * sg + upd_hg[u]
        if BB > 1 or c == nch - 1:
            sret_ref[s0:s0 + cb, h] = s_ret[h]
            shg_ref[s0:s0 + cb, h] = s_hg[h]

    rg, hg = proj(3), proj(7)
    head_out = {}
    for u, (c, h) in enumerate(units):
        o = o_ret[u]
        mu = jnp.mean(o, axis=-1, keepdims=True)
        oc = o - mu
        var = jnp.mean(oc * oc, axis=-1, keepdims=True)
        head_out[(c, h)] = oc * lax.rsqrt(var + LN_EPS) * jax.nn.silu(sl(rg, c, h))
        og = o_hg[u]
        ms = jnp.mean(og * og, axis=-1, keepdims=True)
        head_out[(c, N_HEADS + h)] = og * lax.rsqrt(ms + LN_EPS) * nw * jax.nn.silu(sl(hg, c, h))
    chunk_out = [jnp.concatenate([head_out[(c, j)] for j in range(2 * N_HEADS)], axis=1).astype(bf16)
                 for c in range(nch)]
    mixed = chunk_out[0] if nch == 1 else jnp.concatenate(chunk_out, axis=0)
    m = _dot(mixed, wout_ref[...])
    y = _layernorm_rows(DEEPNORM_ALPHA * x2 + m, g_ref[...], be_ref[...])
    y_ref[...] = y.reshape(BB, TL, D_MODEL)


def _mixer0(x, c2, s2, w_in, w_out, lb_logits, norm_w, ln_g, ln_b, sret0, shg0, *, BB, TL):
    B, L, _ = x.shape
    R = BB * TL
    T = min(R, CHUNK_ROWS)
    if BB == 1:
        cb, tl = 1, T
    else:
        assert T % TL == 0
        cb, tl = T // TL, TL
    per_seq = sret0.shape[0] == B and B > 1
    st_block = (BB, N_HEADS, D_HEAD, D_HEAD)
    n_score_dots = N_HEADS * (R // T) * (2 + (tl - 1).bit_length()) if cb == 1 else 1
    kern = functools.partial(_mixer0_kernel, BB=BB, TL=TL, cb=cb, tl=tl)
    return pl.pallas_call(
        kern, grid=(B // BB, L // TL),
        in_specs=[
            pl.BlockSpec((BB, TL, D_MODEL), lambda b, t: (b, t, 0)),
            pl.BlockSpec((TL, D_HEAD), lambda b, t: (t, 0)),
            pl.BlockSpec((TL, D_HEAD), lambda b, t: (t, 0)),
            _const_spec(w_in.shape), _const_spec(w_out.shape), _const_spec(lb_logits.shape),
            _const_spec(norm_w.shape), _const_spec(ln_g.shape), _const_spec(ln_b.shape),
            _state_spec(st_block, per_seq), _state_spec(st_block, per_seq),
        ],
        out_specs=[
            pl.BlockSpec((BB, TL, D_MODEL), lambda b, t: (b, t, 0)),
            _state_spec(st_block, True), _state_spec(st_block, True),
        ],
        out_shape=[
            jax.ShapeDtypeStruct((B, L, D_MODEL), f32),
            jax.ShapeDtypeStruct((B, N_HEADS, D_HEAD, D_HEAD), f32),
            jax.ShapeDtypeStruct((B, N_HEADS, D_HEAD, D_HEAD), f32),
        ],
        scratch_shapes=[
            pltpu.VMEM((N_HEADS, T, T), f32),
            pltpu.VMEM((N_HEADS, T, D_HEAD), f32),
            pltpu.VMEM((N_HEADS, T, D_HEAD), f32),
            pltpu.VMEM((T, T), bf16),
            pltpu.VMEM((T, T), jnp.int32),
            pltpu.VMEM((n_score_dots, D_HEAD, T), bf16),
        ],
        compiler_params=_params(BIG_BLOCK_VMEM_LIMIT_BYTES), name=f"mixer0_b{BB}_t{TL}",
    )(x, c2, s2, w_in, w_out, lb_logits, norm_w, ln_g, ln_b, sret0, shg0)


def _ffn_kernel(x_ref, wup_ref, cw_ref, cb_ref, wdn_ref, g_ref, be_ref, buf0_ref, *rest, BB, TL, stacked):
    R = BB * TL
    K = CONV_W_FFN - 1
    if stacked:
        prev_ref, y_ref, out_ref, u_ref = rest
        buf_ref = out_ref.at[1]
    else:
        y_ref, buf_ref, u_ref = rest

    @pl.when(pl.program_id(1) == 0)
    def _init_state():
        buf_ref[...] = buf0_ref[...]
        if stacked:
            out_ref[0] = prev_ref[...]

    x2 = x_ref[...].reshape(R, D_MODEL)
    xb = x2.astype(bf16)
    u = _dot(xb, wup_ref[:, :D_FF])
    v = _dot(xb, wup_ref[:, D_FF:])
    u_ref[:, SUBLANES - K:SUBLANES, :] = buf_ref[...]
    u_ref[:, SUBLANES:, :] = u.reshape(BB, TL, D_FF)
    cw = cw_ref[...]
    uc = cb_ref[...] + u_ref[:, SUBLANES - K:SUBLANES - K + TL, :] * cw[0:1]
    for j in range(1, CONV_W_FFN):
        uc = uc + u_ref[:, SUBLANES - K + j:SUBLANES - K + j + TL, :] * cw[j:j + 1]
    buf_ref[...] = u_ref[:, SUBLANES + TL - K:SUBLANES + TL, :]
    hmid = _gelu(uc.reshape(R, D_FF)) * v
    fo = _dot(hmid.astype(bf16), wdn_ref[...])
    y = _layernorm_rows(DEEPNORM_ALPHA * x2 + fo, g_ref[...], be_ref[...])
    y_ref[...] = y.reshape(BB, TL, D_MODEL)


def _ffn(x, w_up, conv_w, conv_b, w_down, ln_g, ln_b, buf0, *, layer, BB, TL, prev_buf=None):
    B, L, _ = x.shape
    per_seq = buf0.shape[-3] == B and B > 1
    K = CONV_W_FFN - 1
    stacked = prev_buf is not None
    kern = functools.partial(_ffn_kernel, BB=BB, TL=TL, stacked=stacked)
    buf_spec = (pl.BlockSpec((2, BB, K, D_FF), lambda b, t: (0, b, 0, 0)) if stacked
                else _state_spec((BB, K, D_FF), True))
    buf_shape = (2, B, K, D_FF) if stacked else (B, K, D_FF)
    buf0_spec = (_state_spec((BB, K, D_FF), per_seq) if buf0.ndim == 3
                 else pl.BlockSpec((None, BB, K, D_FF), lambda b, t: (layer, b, 0, 0)))
    layer_spec = lambda rows, cols: pl.BlockSpec((rows, cols), lambda b, t: (layer, 0), pipeline_mode=pl.Buffered(1))
    return pl.pallas_call(
        kern, grid=(B // BB, L // TL),
        in_specs=[
            pl.BlockSpec((BB, TL, D_MODEL), lambda b, t: (b, t, 0)),
            layer_spec(D_MODEL, 2 * D_FF), _const_spec(conv_w.shape), _const_spec(conv_b.shape),
            layer_spec(D_FF, D_MODEL), _const_spec(ln_g.shape), _const_spec(ln_b.shape),
            buf0_spec,
        ] + ([_state_spec((BB, K, D_FF), True)] if stacked else []),
        out_specs=[pl.BlockSpec((BB, TL, D_MODEL), lambda b, t: (b, t, 0)), buf_spec],
        out_shape=[
            jax.ShapeDtypeStruct((B, L, D_MODEL), f32),
            jax.ShapeDtypeStruct(buf_shape, f32),
        ],
        scratch_shapes=[pltpu.VMEM((BB, SUBLANES + TL, D_FF), f32)],
        compiler_params=_params(BIG_BLOCK_VMEM_LIMIT_BYTES), name=f"ffn_b{BB}_t{TL}",
    )(x, w_up, conv_w, conv_b, w_down, ln_g, ln_b, buf0, *([prev_buf] if stacked else []))


def _rglru_kernel(x_ref, win_ref, cw_ref, cb_ref, wga_ref, bga_ref, wgx_ref, bgx_ref, lam_ref,
                  wout_ref, g_ref, be_ref, h0_ref, buf0_ref,
                  y_ref, h_ref, buf_ref, *, BB, TL):
    R = BB * TL
    K = CONV_W_LRU - 1
    G = TL // SUBLANES

    @pl.when(pl.program_id(1) == 0)
    def _init_state():
        h_ref[...] = h0_ref[...]
        buf_ref[...] = buf0_ref[...]

    x2 = x_ref[...].reshape(R, D_MODEL)
    xb = x2.astype(bf16)
    W = LRU_STEP_COLS
    nchunk = D_RNN // W
    carry = buf_ref[...]
    new_carry = []
    sub = lax.broadcasted_iota(jnp.int32, (R // SUBLANES, SUBLANES, W), 1)
    h0 = h_ref[...]

    def project(ci):
        c0 = ci * W
        return _dot(xb, win_ref[:, c0:c0 + W]), _dot(xb, win_ref[:, D_RNN + c0:D_RNN + c0 + W])

    def recur(ci, gate, rnn):
        c0 = ci * W
        cs = slice(c0, c0 + W)
        rnn3 = rnn.reshape(BB, TL, W)
        xc = _causal_conv(rnn3, carry[:, :, cs], cw_ref, cb_ref, cs).reshape(R, W)
        new_carry.append(rnn3[:, TL - K:, :])
        xcb = xc.astype(bf16)
        blocks = [(j, c0 // LRU_BLOCK + j) for j in range(W // LRU_BLOCK)]
        ga = jnp.concatenate([_dot(xcb[:, j * LRU_BLOCK:(j + 1) * LRU_BLOCK], wga_ref[n]) for j, n in blocks], axis=1)
        gx = jnp.concatenate([_dot(xcb[:, j * LRU_BLOCK:(j + 1) * LRU_BLOCK], wgx_ref[n]) for j, n in blocks], axis=1)
        rgate = jax.nn.sigmoid(ga + bga_ref[:, cs])
        igate = jax.nn.sigmoid(gx + bgx_ref[:, cs])
        nla = rgate * (-LRU_C * jax.nn.log_sigmoid(lam_ref[:, cs]))
        a = jnp.exp(-nla)
        w1 = jnp.tanh(nla) * (1.0 + a * a)
        bv = jnp.where(w1 > 0.0, w1 * lax.rsqrt(w1), 0.0) * (igate * xc)
        a4 = a.reshape(R // SUBLANES, SUBLANES, W)
        b4 = bv.reshape(R // SUBLANES, SUBLANES, W)
        for d in (1, 2, 4):
            keep = sub >= d
            b4 = jnp.where(keep, a4 * pltpu.roll(b4, d, 1) + b4, b4)
            a4 = jnp.where(keep, a4 * pltpu.roll(a4, d, 1), a4)
        a5 = a4.reshape(BB, G, SUBLANES, W)
        b5 = b4.reshape(BB, G, SUBLANES, W)
        hc = h0[:, :, cs]
        tiles = []
        for gi in range(G):
            hg = a5[:, gi] * hc + b5[:, gi]
            hc = hg[:, SUBLANES - 1:SUBLANES, :]
            tiles.append(hg)
        hseq = tiles[0] if G == 1 else jnp.concatenate(tiles, axis=1)
        return (_gelu(gate) * hseq.reshape(R, W)).astype(bf16), hc

    nxt = project(0)
    ys, h_last = [], []
    for ci in range(nchunk):
        cur, nxt = nxt, (project(ci + 1) if ci + 1 < nchunk else None)
        yv, hc = recur(ci, *cur)
        ys.append(yv)
        h_last.append(hc)
    buf_ref[...] = jnp.concatenate(new_carry, axis=2)
    h_ref[...] = jnp.concatenate(h_last, axis=2)
    yv = jnp.concatenate(ys, axis=1)
    groups = LRU_OUT_ROW_GROUPS if R % (LRU_OUT_ROW_GROUPS * 2 * SUBLANES) == 0 else 1
    step = R // groups
    outs = [_layernorm_rows(DEEPNORM_ALPHA * x2[i * step:(i + 1) * step]
                            + _dot(yv[i * step:(i + 1) * step], wout_ref[...]), g_ref[...], be_ref[...])
            for i in range(groups)]
    y = outs[0] if groups == 1 else jnp.concatenate(outs, axis=0)
    y_ref[...] = y.reshape(BB, TL, D_MODEL)


def _rglru(x, w_in, conv_w, conv_b, wga, bga, wgx, bgx, lam, w_out, ln_g, ln_b, h0, buf0, *, BB, TL):
    B, L, _ = x.shape
    per_seq = h0.shape[0] == B and B > 1
    K = CONV_W_LRU - 1
    kern = functools.partial(_rglru_kernel, BB=BB, TL=TL)
    consts = (w_in, conv_w, conv_b, wga, bga, wgx, bgx, lam, w_out, ln_g, ln_b)
    return pl.pallas_call(
        kern, grid=(B // BB, L // TL),
        in_specs=[pl.BlockSpec((BB, TL, D_MODEL), lambda b, t: (b, t, 0))]
        + [_const_spec(c.shape) for c in consts]
        + [_state_spec((BB, 1, D_RNN), per_seq), _state_spec((BB, K, D_RNN), per_seq)],
        out_specs=[
            pl.BlockSpec((BB, TL, D_MODEL), lambda b, t: (b, t, 0)),
            _state_spec((BB, 1, D_RNN), True), _state_spec((BB, K, D_RNN), True),
        ],
        out_shape=[
            jax.ShapeDtypeStruct((B, L, D_MODEL), f32),
            jax.ShapeDtypeStruct((B, 1, D_RNN), f32),
            jax.ShapeDtypeStruct((B, K, D_RNN), f32),
        ],
        compiler_params=_params(), name=f"rglru_b{BB}_t{TL}",
    )(x, *consts, h0, buf0)


def _rglru_tm_kernel(x_ref, win_ref, cw_ref, cb_ref, wga_ref, bga_ref, wgx_ref, bgx_ref, lam_ref,
                     wout_ref, g_ref, be_ref, h0_ref, buf0_ref,
                     y_ref, h_ref, buf_ref, perm_ref, permt_ref, hrun_ref, crun_ref, *, TL):
    B = SUBLANES
    NPART = LRU_TM_PARTS
    TP = TL // NPART
    RP = B * TP
    K = CONV_W_LRU - 1
    W = LRU_STEP_COLS
    nchunk = D_RNN // W
    step = pl.program_id(0)

    @pl.when(step == 0)
    def _init():
        r = lax.broadcasted_iota(jnp.int32, (RP, RP), 0)
        c = lax.broadcasted_iota(jnp.int32, (RP, RP), 1)
        hit = ((r & (B - 1)) * TP + (r >> LOG2_SUBLANES)) == c
        perm_ref[...] = jnp.where(hit, 1.0, 0.0).astype(bf16)
        hit_t = ((c & (B - 1)) * TP + (c >> LOG2_SUBLANES)) == r
        permt_ref[...] = jnp.where(hit_t, 1.0, 0.0).astype(bf16)
        hrun_ref[...] = jnp.broadcast_to(h0_ref[0], (B, D_RNN))
        for k in range(K):
            crun_ref[k] = jnp.broadcast_to(buf0_ref[0, k:k + 1, :], (B, D_RNN))

    def head(p):
        x2 = x_ref[:, p * TP:(p + 1) * TP, :].reshape(RP, D_MODEL)
        return x2, _dot(perm_ref[...], x2.astype(bf16)).astype(bf16)

    def project(xt, ci):
        c0 = ci * W
        return _dot(xt, win_ref[:, c0:c0 + W]), _dot(xt, win_ref[:, D_RNN + c0:D_RNN + c0 + W])

    def body(ci, gate, rnn):
        c0 = ci * W
        cs = slice(c0, c0 + W)
        rnn = rnn.reshape(TP, B, W)
        ext = jnp.concatenate([crun_ref[:, :, cs], rnn], axis=0)
        xc = cb_ref[:, cs] + ext[0:TP] * cw_ref[0:1, cs]
        for j in range(1, CONV_W_LRU):
            xc = xc + ext[j:j + TP] * cw_ref[j:j + 1, cs]
        crun_ref[:, :, cs] = rnn[TP - K:]
        xc = xc.reshape(RP, W)
        xcb = xc.astype(bf16)
        blocks = [(j, c0 // LRU_BLOCK + j) for j in range(W // LRU_BLOCK)]
        ga = jnp.concatenate([_dot(xcb[:, j * LRU_BLOCK:(j + 1) * LRU_BLOCK], wga_ref[n]) for j, n in blocks], axis=1)
        gx = jnp.concatenate([_dot(xcb[:, j * LRU_BLOCK:(j + 1) * LRU_BLOCK], wgx_ref[n]) for j, n in blocks], axis=1)
        rgate = jax.nn.sigmoid(ga + bga_ref[:, cs])
        igate = jax.nn.sigmoid(gx + bgx_ref[:, cs])
        log_a = LRU_C * rgate * jax.nn.log_sigmoid(lam_ref[:, cs])
        a = jnp.exp(log_a).reshape(TP, B, W)
        th = jnp.tanh(log_a)
        bv = (jnp.sqrt(-2.0 * th / (1.0 - th)) * (igate * xc)).reshape(TP, B, W)
        h = hrun_ref[:, cs]
        hs = []
        for t in range(TP):
            h = a[t] * h + bv[t]
            hs.append(h)
        hrun_ref[:, cs] = h
        return (_gelu(gate) * jnp.stack(hs, axis=0).reshape(RP, W)).astype(bf16)

    def tail(p, x2, ys):
        yv = _dot(permt_ref[...], jnp.concatenate(ys, axis=1)).astype(bf16)
        y = _layernorm_rows(DEEPNORM_ALPHA * x2 + _dot(yv, wout_ref[...]), g_ref[...], be_ref[...])
        y_ref[:, p * TP:(p + 1) * TP, :] = y.reshape(B, TP, D_MODEL)

    x2s, xts, nxt = {}, {}, None
    x2s[0], xts[0] = head(0)
    nxt = project(xts[0], 0)
    for p in range(NPART):
        if p + 1 < NPART:
            x2s[p + 1], xts[p + 1] = head(p + 1)
        ys = []
        for ci in range(nchunk):
            cur = nxt
            if ci + 1 < nchunk:
                nxt = project(xts[p], ci + 1)
            elif p + 1 < NPART:
                nxt = project(xts[p + 1], 0)
            ys.append(body(ci, *cur))
        tail(p, x2s[p], ys)

    @pl.when(step == pl.num_programs(0) - 1)
    def _emit_state():
        h_ref[:, 0, :] = hrun_ref[...]
        for k in range(K):
            buf_ref[:, k, :] = crun_ref[k]


def _rglru_tm(x, w_in, conv_w, conv_b, wga, bga, wgx, bgx, lam, w_out, ln_g, ln_b, h0, buf0, *, TL):
    B, L, _ = x.shape
    assert B == SUBLANES and h0.shape[0] == 1 and buf0.shape[0] == 1
    K = CONV_W_LRU - 1
    R = B * TL // LRU_TM_PARTS
    consts = (w_in, conv_w, conv_b, wga, bga, wgx, bgx, lam, w_out, ln_g, ln_b, h0, buf0)
    const_spec = lambda shape: pl.BlockSpec(shape, lambda t: (0,) * len(shape), pipeline_mode=pl.Buffered(1))
    return pl.pallas_call(
        functools.partial(_rglru_tm_kernel, TL=TL), grid=(L // TL,),
        in_specs=[pl.BlockSpec((B, TL, D_MODEL), lambda t: (0, t, 0))] + [const_spec(c.shape) for c in consts],
        out_specs=[
            pl.BlockSpec((B, TL, D_MODEL), lambda t: (0, t, 0)),
            pl.BlockSpec((B, 1, D_RNN), lambda t: (0, 0, 0)), pl.BlockSpec((B, K, D_RNN), lambda t: (0, 0, 0)),
        ],
        out_shape=[
            jax.ShapeDtypeStruct((B, L, D_MODEL), f32),
            jax.ShapeDtypeStruct((B, 1, D_RNN), f32),
            jax.ShapeDtypeStruct((B, K, D_RNN), f32),
        ],
        scratch_shapes=[
            pltpu.VMEM((R, R), bf16), pltpu.VMEM((R, R), bf16),
            pltpu.VMEM((B, D_RNN), f32),
            pltpu.VMEM((K, B, D_RNN), f32),
        ],
        compiler_params=pltpu.CompilerParams(dimension_semantics=("arbitrary",), vmem_limit_bytes=VMEM_LIMIT_BYTES),
        name=f"rglru_tm_t{TL}",
    )(x, *consts)


def _rope_tables(pos0, length):
    half = D_HEAD // 2
    pos = pos0 + jnp.arange(length, dtype=jnp.int32)
    inv = ROPE_BASE ** (-jnp.arange(half, dtype=f32) / half)
    ang = pos.astype(f32)[:, None] * inv[None, :]
    cos, sin = jnp.cos(ang), jnp.sin(ang)
    return jnp.concatenate([cos, cos], axis=1), jnp.concatenate([-sin, sin], axis=1)


def _trunk(x, pos0, states, p, *, BB, TL, BB0, TL0=None, lru_tm_steps=None, stack_ffn_bufs=False):
    s_ret, s_hgrn, h_lru, buf_lru, buf_ffn0, buf_ffn1 = states
    c2, s2 = _rope_tables(pos0, x.shape[1])
    x, s_ret, s_hgrn = _mixer0(x, c2, s2, p['w_in_ab'], p['w_out_ab'], p['hgrn_lb_logits'], p['hgrn_norm_w'],
                               p['ln_mix_g'][0], p['ln_mix_b'][0], s_ret, s_hgrn, BB=BB0, TL=TL0 or TL)
    x, buf_ffn0 = _ffn(x, p['w_ffn_up'], p['ffn_conv_w'][0], p['ffn_conv_b'][0], p['w_ffn_down'],
                       p['ln_ffn_g'][0], p['ln_ffn_b'][0], buf_ffn0, layer=0, BB=BB, TL=TL)
    lru_args = (x, p['w_in_c'], p['conv_w_c'], p['conv_b_c'], p['w_gate_a'], p['b_gate_a'], p['w_gate_x'],
                p['b_gate_x'], p['lru_lambda'], p['w_out_c'], p['ln_mix_g'][1], p['ln_mix_b'][1], h_lru, buf_lru)
    if lru_tm_steps is None:
        x, h_lru, buf_lru = _rglru(*lru_args, BB=BB, TL=TL)
    else:
        x, h_lru, buf_lru = _rglru_tm(*lru_args, TL=lru_tm_steps)
    x, buf_ffn1 = _ffn(x, p['w_ffn_up'], p['ffn_conv_w'][1], p['ffn_conv_b'][1], p['w_ffn_down'],
                       p['ln_ffn_g'][1], p['ln_ffn_b'][1], buf_ffn1, layer=1, BB=BB, TL=TL,
                       prev_buf=buf_ffn0 if stack_ffn_bufs else None)
    return x, (s_ret, s_hgrn, h_lru, buf_lru, buf_ffn0, buf_ffn1)


def kernel(x_prompt, x_sample, state_ret, state_hgrn, state_rglru_h, state_rglru_conv, state_ffn_conv, meta_tokens, w_in_ab, w_out_ab, hgrn_lb_logits, hgrn_norm_w, w_in_c, conv_w_c, conv_b_c, w_gate_a, b_gate_a, w_gate_x, b_gate_x, lru_lambda, w_out_c, ln_mix_g, ln_mix_b, ln_ffn_g, ln_ffn_b, w_ffn_up, ffn_conv_w, ffn_conv_b, w_ffn_down):
    row = lambda z: z.reshape(1, -1)
    wb = [w.astype(bf16) for w in (w_in_ab, w_out_ab, w_in_c, w_gate_a, w_gate_x, w_out_c,
                                   w_ffn_up.reshape(DEPTH * D_MODEL, 2 * D_FF),
                                   w_ffn_down.reshape(DEPTH * D_FF, D_MODEL))]
    p = dict(
        w_in_ab=wb[0], w_out_ab=wb[1],
        hgrn_lb_logits=hgrn_lb_logits, hgrn_norm_w=row(hgrn_norm_w),
        w_in_c=wb[2], conv_w_c=conv_w_c, conv_b_c=row(conv_b_c),
        w_gate_a=wb[3], b_gate_a=row(b_gate_a),
        w_gate_x=wb[4], b_gate_x=row(b_gate_x),
        lru_lambda=row(lru_lambda), w_out_c=wb[5],
        ln_mix_g=[row(ln_mix_g[i]) for i in range(DEPTH)], ln_mix_b=[row(ln_mix_b[i]) for i in range(DEPTH)],
        ln_ffn_g=[row(ln_ffn_g[i]) for i in range(DEPTH)], ln_ffn_b=[row(ln_ffn_b[i]) for i in range(DEPTH)],
        w_ffn_up=wb[6], w_ffn_down=wb[7],
        ffn_conv_w=[ffn_conv_w[i] for i in range(DEPTH)],
        ffn_conv_b=[row(ffn_conv_b[i]) for i in range(DEPTH)],
    )
    dt = x_prompt.dtype
    zero_states = (
        jnp.zeros((1, N_HEADS, D_HEAD, D_HEAD), dt), jnp.zeros((1, N_HEADS, D_HEAD, D_HEAD), dt),
        jnp.zeros((1, 1, D_RNN), dt), jnp.zeros((1, CONV_W_LRU - 1, D_RNN), dt),
        jnp.zeros((1, CONV_W_FFN - 1, D_FF), dt), jnp.zeros((1, CONV_W_FFN - 1, D_FF), dt),
    )
    _, meta_states = _trunk(meta_tokens.astype(dt)[None], 0, zero_states, p, BB=1, TL=N_META, BB0=1)
    y_prompt, ps = _trunk(x_prompt, N_META, meta_states, p, BB=1, TL=PROMPT_BLOCK_STEPS, BB0=1,
                          TL0=PROMPT_MIXER0_BLOCK_STEPS, lru_tm_steps=LRU_TM_BLOCK_STEPS, stack_ffn_bufs=True)
    sample_states = (state_ret, state_hgrn, state_rglru_h[:, None, :], state_rglru_conv,
                     state_ffn_conv, state_ffn_conv)
    y_sample, ss = _trunk(x_sample, PAST_LEN, sample_states, p, BB=32, TL=x_sample.shape[1], BB0=16,
                          stack_ffn_bufs=True)
    return (y_prompt, y_sample, ps[0], ss[0], ps[1], ss[1], ps[2][:, 0, :], ss[2][:, 0, :], ps[3], ss[3],
            ps[5], ss[5])
```

```python
import functools
import math

import jax
import jax.numpy as jnp
from jax import lax
from jax.experimental import pallas as pl
from jax.experimental.pallas import tpu as pltpu

f32 = jnp.float32
bf16 = jnp.bfloat16

D_MODEL = 1024
N_META = 16
PAST_LEN = 16384
N_HEADS = 4
D_HEAD = 128
SEG = N_HEADS * D_HEAD
ROPE_BASE = 10000.0
D_RNN = 1024
N_LRU_BLOCKS = 8
LRU_BLOCK = D_RNN // N_LRU_BLOCKS
CONV_W_LRU = 4
LRU_C = 8.0
D_FF = 2816
CONV_W_FFN = 3
LN_EPS = 1e-5
DEPTH = 2
DEEPNORM_ALPHA = (2.0 * DEPTH) ** 0.25
LOG_GAMMA = tuple(math.log1p(-(2.0 ** (-5.0 - h))) for h in range(N_HEADS))

SUBLANES = 8
CHUNK_ROWS = 128
LRU_STEP_COLS = 256
LRU_OUT_ROW_GROUPS = 2
PROMPT_BLOCK_STEPS = 1024
PROMPT_MIXER0_BLOCK_STEPS = 1024
LRU_TM_BLOCK_STEPS = 128
LRU_TM_PARTS = 4
LOG2_SUBLANES = SUBLANES.bit_length() - 1
VMEM_LIMIT_BYTES = 56 * 1024 * 1024
BIG_BLOCK_VMEM_LIMIT_BYTES = 62 * 1024 * 1024

_GELU_K1 = -2.0 * math.log2(math.e) * math.sqrt(2.0 / math.pi)
_GELU_K3 = _GELU_K1 * 0.044715


def _gelu(x):
    return x / (1.0 + jnp.exp2(x * (_GELU_K1 + _GELU_K3 * (x * x))))


def _dot(a, b):
    return jnp.dot(a, b, preferred_element_type=f32)


def _dot_nt(a, b, t_ref=None):
    if t_ref is None:
        return lax.dot_general(a, b, (((1,), (1,)), ((), ())), preferred_element_type=f32)
    t_ref[...] = b.T
    return jnp.dot(a, t_ref[...], preferred_element_type=f32)


def _layernorm_rows(y, g, b):
    mu = jnp.mean(y, axis=-1, keepdims=True)
    yc = y - mu
    var = jnp.mean(yc * yc, axis=-1, keepdims=True)
    return yc * lax.rsqrt(var + LN_EPS) * g + b


def _causal_conv(x3, carry, w_ref, b_ref, cs):
    BB, TL, W = x3.shape
    K = carry.shape[1]
    w = [w_ref[j:j + 1, cs] for j in range(K + 1)]

    def shift(p, j):
        bnd = w[0] * carry[:, K - j:K - j + 1, :]
        for i in range(1, j):
            bnd = bnd + w[i] * carry[:, K - j + i:K - j + i + 1, :]
        if TL == SUBLANES:
            t = lax.broadcasted_iota(jnp.int32, p.shape, 1)
            return jnp.where(t == 0, bnd, pltpu.roll(p, 1, 1))
        assert BB == 1
        rolled = pltpu.roll(p.reshape(TL, W), 1, 0)
        t = lax.broadcasted_iota(jnp.int32, (SUBLANES, W), 0)
        head = jnp.where(t == 0, bnd.reshape(1, W), rolled[:SUBLANES])
        return jnp.concatenate([head, rolled[SUBLANES:]], axis=0).reshape(BB, TL, W)

    p = w[0] * x3
    for j in range(1, K + 1):
        p = w[j] * x3 + shift(p, j)
    return b_ref[:, cs] + p


def _const_spec(shape):
    nd = len(shape)
    return pl.BlockSpec(shape, lambda b, t: (0,) * nd, pipeline_mode=pl.Buffered(1))


def _state_spec(block, per_seq):
    nd = len(block)
    if per_seq:
        return pl.BlockSpec(block, lambda b, t: (b,) + (0,) * (nd - 1))
    return pl.BlockSpec(block, lambda b, t: (0,) * nd)


def _params(vmem_limit_bytes=VMEM_LIMIT_BYTES):
    return pltpu.CompilerParams(dimension_semantics=("arbitrary", "arbitrary"),
                                vmem_limit_bytes=vmem_limit_bytes)


def _mixer0_init_tables(dmask_ref, wq_ref, wk_ref, tri_ref, lvl_ref, *, cb, tl):
    T = cb * tl
    tl_shift = tl.bit_length() - 1
    r = lax.broadcasted_iota(jnp.int32, (T, T), 0)
    c = lax.broadcasted_iota(jnp.int32, (T, T), 1)
    same = (r >> tl_shift) == (c >> tl_shift)
    rel = (r & (tl - 1)) - (c & (tl - 1))
    causal = same & (rel >= 0)
    relf = jnp.maximum(rel, 0).astype(f32)
    tri_ref[...] = jnp.where(causal, 1.0, 0.0).astype(bf16)
    lvl = jnp.where(r == c, 0, -1)
    s, li = 1, 1
    while s < tl:
        blk = (r >> li) == (c >> li)
        hit = blk & ((r & (2 * s - 1)) >= s) & ((c & (2 * s - 1)) < s)
        lvl = jnp.where(hit, li, lvl)
        s, li = 2 * s, li + 1
    lvl_ref[...] = lvl
    tr = (lax.broadcasted_iota(jnp.int32, (T, D_HEAD), 0) & (tl - 1)).astype(f32)
    for h in range(N_HEADS):
        lg = LOG_GAMMA[h]
        dmask_ref[h] = jnp.where(causal, jnp.exp(lg * relf), 0.0)
        wq_ref[h] = jnp.exp(lg * (tr + 1.0))
        wk_ref[h] = jnp.exp(lg * ((tl - 1.0) - tr))


def _hgrn_scores(gqs, kks, fs, bs, lvl_ref, dot_nt, *, T, tl):
    n = len(gqs)
    row = lax.broadcasted_iota(jnp.int32, (T, D_HEAD), 0)
    ntile = T // SUBLANES

    def owned(li, g):
        return lvl_ref[g * SUBLANES:(g + 1) * SUBLANES, :] == li

    full = [dot_nt(gqs[u].astype(bf16), kks[u].astype(bf16)) for u in range(n)]
    tiles = [[jnp.where(owned(0, g), full[u][g * SUBLANES:(g + 1) * SUBLANES], 0.0) for g in range(ntile)]
             for u in range(n)]

    def take(u, li, p, first_tile, n_tiles, p_row0):
        for g in range(n_tiles):
            t = first_tile + g
            tiles[u][t] = jnp.where(owned(li, t), p[p_row0 + g * SUBLANES:p_row0 + (g + 1) * SUBLANES], tiles[u][t])

    s, li = 1, 1
    while s < tl:
        nblk = T // (2 * s)
        if s < SUBLANES:
            up = (row & (2 * s - 1)) >= s
            zs = []
            for u in range(n):
                gq, kk, f, b = gqs[u], kks[u], fs[u], bs[u]
                if s == 1:
                    z = jnp.where(up, gq * f, kk)
                elif s == 2:
                    w = row & 3
                    e = jnp.where(w == 0, pltpu.roll(f, T - 1, 0),
                                  jnp.where(w == 1, 1.0, jnp.where(w == 2, f, f * pltpu.roll(f, 1, 0))))
                    z = jnp.where(up, gq, kk) * e
                else:
                    parts = [jnp.abs(b[m * 2 * s:(m + 1) * 2 * s, :] - b[m * 2 * s + s - 1:m * 2 * s + s, :])
                             for m in range(nblk)]
                    z = jnp.where(up, gq, kk) * jnp.exp2(-jnp.concatenate(parts, axis=0))
                zs.append(z.astype(bf16))
            ps = [dot_nt(zb, zb) for zb in zs]
            for u in range(n):
                take(u, li, ps[u], 0, ntile, 0)
        else:
            zs, qus = [], []
            for u in range(n):
                gq, kk, b = gqs[u], kks[u], bs[u]
                both, upper = [], []
                for m in range(nblk):
                    r0 = m * 2 * s
                    beta = b[r0 + s - 1:r0 + s, :]
                    k_lo = kk[r0:r0 + s] * jnp.exp2(beta - b[r0:r0 + s])
                    q_hi = gq[r0 + s:r0 + 2 * s] * jnp.exp2(b[r0 + s:r0 + 2 * s] - beta)
                    both += [k_lo, q_hi]
                    upper.append(q_hi)
                zs.append(jnp.concatenate(both, axis=0).astype(bf16))
                qus.append((upper[0] if nblk == 1 else jnp.concatenate(upper, axis=0)).astype(bf16))
            ps = [dot_nt(qus[u], zs[u]) for u in range(n)]
            for u in range(n):
                for m in range(nblk):
                    take(u, li, ps[u], (m * 2 * s + s) // SUBLANES, s // SUBLANES, m * s)
        s, li = 2 * s, li + 1
    return [jnp.concatenate(tiles[u], axis=0).astype(bf16) for u in range(n)]


def _mixer0_kernel(x_ref, c2_ref, s2_ref, win_ref, wout_ref, lbl_ref, nw_ref, g_ref, be_ref,
                   sret0_ref, shg0_ref, y_ref, sret_ref, shg_ref,
                   dmask_ref, wq_ref, wk_ref, tri_ref, lvl_ref, kt_ref, *, BB, TL, cb, tl):
    T = cb * tl
    R = BB * TL
    nch = R // T
    slots = iter(range(kt_ref.shape[0]))

    def dot_nt(a, b):
        return _dot_nt(a, b, kt_ref.at[next(slots)] if (cb == 1 and T == CHUNK_ROWS) else None)

    @pl.when((pl.program_id(0) == 0) & (pl.program_id(1) == 0))
    def _init_tables():
        _mixer0_init_tables(dmask_ref, wq_ref, wk_ref, tri_ref, lvl_ref, cb=cb, tl=tl)

    @pl.when(pl.program_id(1) == 0)
    def _init_state():
        sret_ref[...] = sret0_ref[...]
        shg_ref[...] = shg0_ref[...]

    x2 = x_ref[...].reshape(R, D_MODEL)
    xb = x2.astype(bf16)

    def proj(i):
        return _dot(xb, win_ref[:, i * SEG:(i + 1) * SEG])

    lbl = lbl_ref[...]
    le = jnp.exp(lbl - jnp.max(lbl, axis=0, keepdims=True))
    lb = le[0:1] / jnp.sum(le, axis=0, keepdims=True)
    nw = nw_ref[...]

    def seqs(z):
        return z.reshape(cb, tl, D_HEAD)

    def rows(z3):
        return z3.reshape(T, D_HEAD)

    def qs(q3, s3):
        if cb == 1:
            return _dot(q3[0], s3[0])[None]
        return jnp.einsum('bqd,bdv->bqv', q3, s3, preferred_element_type=f32)

    def ktv(k3, v3):
        if cb == 1:
            return lax.dot_general(k3[0], v3[0], (((0,), (0,)), ((), ())),
                                   preferred_element_type=f32)[None]
        return jnp.einsum('btd,btv->bdv', k3, v3, preferred_element_type=f32)

    units = [(c, h) for c in range(nch) for h in range(N_HEADS)]
    nu = len(units)

    def sl(z, c, h):
        return z[c * T:(c + 1) * T, h * D_HEAD:(h + 1) * D_HEAD]

    def seq0(c):
        return 0 if BB == 1 else c * cb

    def rope_tables(c):
        ts = slice(c * tl, (c + 1) * tl) if BB == 1 else slice(0, tl)
        return c2_ref[ts, :][None], s2_ref[ts, :][None]

    def rope(z, cos2, sin2):
        return rows(seqs(z) * cos2 + seqs(pltpu.roll(z, D_HEAD // 2, 1)) * sin2)

    tabs = [rope_tables(c) for c in range(nch)]
    ktabs = [(c2 * (D_HEAD ** -0.5), s2 * (D_HEAD ** -0.5)) for c2, s2 in tabs]

    rq, rk = proj(0), proj(1)
    q = [rope(sl(rq, c, h), *tabs[c]) for c, h in units]
    k = [rope(sl(rk, c, h), *ktabs[c]) for c, h in units]
    rv = proj(2)
    v = [sl(rv, c, h).astype(bf16) for c, h in units]
    att = [dot_nt(q[u].astype(bf16), k[u].astype(bf16)) for u in range(nu)]
    att = [(att[u] * dmask_ref[h]).astype(bf16) for u, (c, h) in enumerate(units)]
    o_ret = [_dot(att[u], v[u]) for u in range(nu)]
    qw = [seqs((q[u] * wq_ref[h]).astype(bf16)) for u, (c, h) in enumerate(units)]
    upd_ret = [ktv(seqs((k[u] * wk_ref[h]).astype(bf16)), seqs(v[u])) for u, (c, h) in enumerate(units)]

    hf = proj(5)
    lbs = [lb[:, h * D_HEAD:(h + 1) * D_HEAD] for h in range(N_HEADS)]
    f = [lbs[h] + (1.0 - lbs[h]) * jax.nn.sigmoid(sl(hf, c, h)) for c, h in units]
    lf = [jnp.log2(z) for z in f]
    kk = [1.0 - z for z in f]
    hq, hi = proj(4), proj(6)
    gq = [sl(hq, c, h) for c, h in units]
    gv = [sl(hi, c, h).astype(bf16) for c, h in units]
    split = []
    for z in lf:
        l1 = z.astype(bf16)
        r1 = z - l1.astype(f32)
        l2 = r1.astype(bf16)
        l3 = (r1 - l2.astype(f32)).astype(bf16)
        split.append(jnp.concatenate([l1, l2, l3], axis=1))
    cs = [_dot(tri_ref[...], z) for z in split]
    b = [z[:, :D_HEAD] + z[:, D_HEAD:2 * D_HEAD] + z[:, 2 * D_HEAD:] for z in cs]
    scores = _hgrn_scores(gq, kk, f, b, lvl_ref, dot_nt, T=T, tl=tl)
    o_hg = [_dot(scores[u], gv[u]) for u in range(nu)]
    qe = [seqs((gq[u] * jnp.exp2(b[u])).astype(bf16)) for u in range(nu)]
    b3 = [seqs(z) for z in b]
    bl = [z[:, tl - 1:tl, :] for z in b3]
    upd_hg = [ktv((seqs(kk[u]) * jnp.exp2(bl[u] - b3[u])).astype(bf16), seqs(gv[u])) for u in range(nu)]
    decay = []
    for z in bl:
        ez = jnp.exp2(z)
        d = [jnp.broadcast_to(ez[j], (D_HEAD, D_HEAD)).T for j in range(cb)]
        decay.append(d[0][None] if cb == 1 else jnp.stack(d))

    s_ret, s_hg = {}, {}
    for u, (c, h) in enumerate(units):
        s0 = seq0(c)
        first = BB > 1 or c == 0
        sp = sret_ref[s0:s0 + cb, h] if first else s_ret[h]
        sg = shg_ref[s0:s0 + cb, h] if first else s_hg[h]
        o_ret[u] = o_ret[u] + rows(qs(qw[u], sp.astype(bf16)))
        o_hg[u] = o_hg[u] + rows(qs(qe[u], sg.astype(bf16)))
        s_ret[h] = math.exp(LOG_GAMMA[h] * tl) * sp + upd_ret[u]
        s_hg[h] = decay[u] * sg + upd_hg[u]
        if BB > 1 or c == nch - 1:
            sret_ref[s0:s0 + cb, h] = s_ret[h]
            shg_ref[s0:s0 + cb, h] = s_hg[h]

    rg, hg = proj(3), proj(7)
    head_out = {}
    for u, (c, h) in enumerate(units):
        o = o_ret[u]
        mu = jnp.mean(o, axis=-1, keepdims=True)
        oc = o - mu
        var = jnp.mean(oc * oc, axis=-1, keepdims=True)
        head_out[(c, h)] = oc * lax.rsqrt(var + LN_EPS) * jax.nn.silu(sl(rg, c, h))
        og = o_hg[u]
        ms = jnp.mean(og * og, axis=-1, keepdims=True)
        head_out[(c, N_HEADS + h)] = og * lax.rsqrt(ms + LN_EPS) * nw * jax.nn.silu(sl(hg, c, h))
    chunk_out = [jnp.concatenate([head_out[(c, j)] for j in range(2 * N_HEADS)], axis=1).astype(bf16)
                 for c in range(nch)]
    mixed = chunk_out[0] if nch == 1 else jnp.concatenate(chunk_out, axis=0)
    m = _dot(mixed, wout_ref[...])
    y = _layernorm_rows(DEEPNORM_ALPHA * x2 + m, g_ref[...], be_ref[...])
    y_ref[...] = y.reshape(BB, TL, D_MODEL)


def _mixer0(x, c2, s2, w_in, w_out, lb_logits, norm_w, ln_g, ln_b, sret0, shg0, *, BB, TL):
    B, L, _ = x.shape
    R = BB * TL
    T = min(R, CHUNK_ROWS)
    if BB == 1:
        cb, tl = 1, T
    else:
        assert T % TL == 0
        cb, tl = T // TL, TL
    per_seq = sret0.shape[0] == B and B > 1
    st_block = (BB, N_HEADS, D_HEAD, D_HEAD)
    n_score_dots = N_HEADS * (R // T) * (2 + (tl - 1).bit_length()) if cb == 1 else 1
    kern = functools.partial(_mixer0_kernel, BB=BB, TL=TL, cb=cb, tl=tl)
    return pl.pallas_call(
        kern, grid=(B // BB, L // TL),
        in_specs=[
            pl.BlockSpec((BB, TL, D_MODEL), lambda b, t: (b, t, 0)),
            pl.BlockSpec((TL, D_HEAD), lambda b, t: (t, 0)),
            pl.BlockSpec((TL, D_HEAD), lambda b, t: (t, 0)),
            _const_spec(w_in.shape), _const_spec(w_out.shape), _const_spec(lb_logits.shape),
            _const_spec(norm_w.shape), _const_spec(ln_g.shape), _const_spec(ln_b.shape),
            _state_spec(st_block, per_seq), _state_spec(st_block, per_seq),
        ],
        out_specs=[
            pl.BlockSpec((BB, TL, D_MODEL), lambda b, t: (b, t, 0)),
            _state_spec(st_block, True), _state_spec(st_block, True),
        ],
        out_shape=[
            jax.ShapeDtypeStruct((B, L, D_MODEL), f32),
            jax.ShapeDtypeStruct((B, N_HEADS, D_HEAD, D_HEAD), f32),
            jax.ShapeDtypeStruct((B, N_HEADS, D_HEAD, D_HEAD), f32),
        ],
        scratch_shapes=[
            pltpu.VMEM((N_HEADS, T, T), f32),
            pltpu.VMEM((N_HEADS, T, D_HEAD), f32),
            pltpu.VMEM((N_HEADS, T, D_HEAD), f32),
            pltpu.VMEM((T, T), bf16),
            pltpu.VMEM((T, T), jnp.int32),
            pltpu.VMEM((n_score_dots, D_HEAD, T), bf16),
        ],
        compiler_params=_params(BIG_BLOCK_VMEM_LIMIT_BYTES), name=f"mixer0_b{BB}_t{TL}",
    )(x, c2, s2, w_in, w_out, lb_logits, norm_w, ln_g, ln_b, sret0, shg0)


def _ffn_kernel(x_ref, wup_ref, cw_ref, cb_ref, wdn_ref, g_ref, be_ref, buf0_ref, *rest, BB, TL, stacked):
    R = BB * TL
    K = CONV_W_FFN - 1
    if stacked:
        prev_ref, y_ref, out_ref, u_ref = rest
        buf_ref = out_ref.at[1]
    else:
        y_ref, buf_ref, u_ref = rest

    @pl.when(pl.program_id(1) == 0)
    def _init_state():
        buf_ref[...] = buf0_ref[...]
        if stacked:
            out_ref[0] = prev_ref[...]

    x2 = x_ref[...].reshape(R, D_MODEL)
    xb = x2.astype(bf16)
    u = _dot(xb, wup_ref[:, :D_FF])
    v = _dot(xb, wup_ref[:, D_FF:])
    u_ref[:, SUBLANES - K:SUBLANES, :] = buf_ref[...]
    u_ref[:, SUBLANES:, :] = u.reshape(BB, TL, D_FF)
    cw = cw_ref[...]
    uc = cb_ref[...] + u_ref[:, SUBLANES - K:SUBLANES - K + TL, :] * cw[0:1]
    for j in range(1, CONV_W_FFN):
        uc = uc + u_ref[:, SUBLANES - K + j:SUBLANES - K + j + TL, :] * cw[j:j + 1]
    buf_ref[...] = u_ref[:, SUBLANES + TL - K:SUBLANES + TL, :]
    hmid = _gelu(uc.reshape(R, D_FF)) * v
    fo = _dot(hmid.astype(bf16), wdn_ref[...])
    y = _layernorm_rows(DEEPNORM_ALPHA * x2 + fo, g_ref[...], be_ref[...])
    y_ref[...] = y.reshape(BB, TL, D_MODEL)


def _ffn(x, w_up, conv_w, conv_b, w_down, ln_g, ln_b, buf0, *, layer, BB, TL, prev_buf=None):
    B, L, _ = x.shape
    per_seq = buf0.shape[-3] == B and B > 1
    K = CONV_W_FFN - 1
    stacked = prev_buf is not None
    kern = functools.partial(_ffn_kernel, BB=BB, TL=TL, stacked=stacked)
    buf_spec = (pl.BlockSpec((2, BB, K, D_FF), lambda b, t: (0, b, 0, 0)) if stacked
                else _state_spec((BB, K, D_FF), True))
    buf_shape = (2, B, K, D_FF) if stacked else (B, K, D_FF)
    buf0_spec = (_state_spec((BB, K, D_FF), per_seq) if buf0.ndim == 3
                 else pl.BlockSpec((None, BB, K, D_FF), lambda b, t: (layer, b, 0, 0)))
    layer_spec = lambda rows, cols: pl.BlockSpec((rows, cols), lambda b, t: (layer, 0), pipeline_mode=pl.Buffered(1))
    return pl.pallas_call(
        kern, grid=(B // BB, L // TL),
        in_specs=[
            pl.BlockSpec((BB, TL, D_MODEL), lambda b, t: (b, t, 0)),
            layer_spec(D_MODEL, 2 * D_FF), _const_spec(conv_w.shape), _const_spec(conv_b.shape),
            layer_spec(D_FF, D_MODEL), _const_spec(ln_g.shape), _const_spec(ln_b.shape),
            buf0_spec,
        ] + ([_state_spec((BB, K, D_FF), True)] if stacked else []),
        out_specs=[pl.BlockSpec((BB, TL, D_MODEL), lambda b, t: (b, t, 0)), buf_spec],
        out_shape=[
            jax.ShapeDtypeStruct((B, L, D_MODEL), f32),
            jax.ShapeDtypeStruct(buf_shape, f32),
        ],
        scratch_shapes=[pltpu.VMEM((BB, SUBLANES + TL, D_FF), f32)],
        compiler_params=_params(BIG_BLOCK_VMEM_LIMIT_BYTES), name=f"ffn_b{BB}_t{TL}",
    )(x, w_up, conv_w, conv_b, w_down, ln_g, ln_b, buf0, *([prev_buf] if stacked else []))


def _rglru_kernel(x_ref, win_ref, cw_ref, cb_ref, wga_ref, bga_ref, wgx_ref, bgx_ref, lam_ref,
                  wout_ref, g_ref, be_ref, h0_ref, buf0_ref,
                  y_ref, h_ref, buf_ref, *, BB, TL):
    R = BB * TL
    K = CONV_W_LRU - 1
    G = TL // SUBLANES

    @pl.when(pl.program_id(1) == 0)
    def _init_state():
        h_ref[...] = h0_ref[...]
        buf_ref[...] = buf0_ref[...]

    x2 = x_ref[...].reshape(R, D_MODEL)
    xb = x2.astype(bf16)
    W = LRU_STEP_COLS
    nchunk = D_RNN // W
    carry = buf_ref[...]
    new_carry = []
    sub = lax.broadcasted_iota(jnp.int32, (R // SUBLANES, SUBLANES, W), 1)
    h0 = h_ref[...]

    def project(ci):
        c0 = ci * W
        return _dot(xb, win_ref[:, c0:c0 + W]), _dot(xb, win_ref[:, D_RNN + c0:D_RNN + c0 + W])

    def recur(ci, gate, rnn):
        c0 = ci * W
        cs = slice(c0, c0 + W)
        rnn3 = rnn.reshape(BB, TL, W)
        xc = _causal_conv(rnn3, carry[:, :, cs], cw_ref, cb_ref, cs).reshape(R, W)
        new_carry.append(rnn3[:, TL - K:, :])
        xcb = xc.astype(bf16)
        blocks = [(j, c0 // LRU_BLOCK + j) for j in range(W // LRU_BLOCK)]
        ga = jnp.concatenate([_dot(xcb[:, j * LRU_BLOCK:(j + 1) * LRU_BLOCK], wga_ref[n]) for j, n in blocks], axis=1)
        gx = jnp.concatenate([_dot(xcb[:, j * LRU_BLOCK:(j + 1) * LRU_BLOCK], wgx_ref[n]) for j, n in blocks], axis=1)
        rgate = jax.nn.sigmoid(ga + bga_ref[:, cs])
        igate = jax.nn.sigmoid(gx + bgx_ref[:, cs])
        nla = rgate * (-LRU_C * jax.nn.log_sigmoid(lam_ref[:, cs]))
        a = jnp.exp(-nla)
        w1 = jnp.tanh(nla) * (1.0 + a * a)
        bv = jnp.where(w1 > 0.0, w1 * lax.rsqrt(w1), 0.0) * (igate * xc)
        a4 = a.reshape(R // SUBLANES, SUBLANES, W)
        b4 = bv.reshape(R // SUBLANES, SUBLANES, W)
        for d in (1, 2, 4):
            keep = sub >= d
            b4 = jnp.where(keep, a4 * pltpu.roll(b4, d, 1) + b4, b4)
            a4 = jnp.where(keep, a4 * pltpu.roll(a4, d, 1), a4)
        a5 = a4.reshape(BB, G, SUBLANES, W)
        b5 = b4.reshape(BB, G, SUBLANES, W)
        hc = h0[:, :, cs]
        tiles = []
        for gi in range(G):
            hg = a5[:, gi] * hc + b5[:, gi]
            hc = hg[:, SUBLANES - 1:SUBLANES, :]
            tiles.append(hg)
        hseq = tiles[0] if G == 1 else jnp.concatenate(tiles, axis=1)
        return (_gelu(gate) * hseq.reshape(R, W)).astype(bf16), hc

    nxt = project(0)
    ys, h_last = [], []
    for ci in range(nchunk):
        cur, nxt = nxt, (project(ci + 1) if ci + 1 < nchunk else None)
        yv, hc = recur(ci, *cur)
        ys.append(yv)
        h_last.append(hc)
    buf_ref[...] = jnp.concatenate(new_carry, axis=2)
    h_ref[...] = jnp.concatenate(h_last, axis=2)
    yv = jnp.concatenate(ys, axis=1)
    groups = LRU_OUT_ROW_GROUPS if R % (LRU_OUT_ROW_GROUPS * 2 * SUBLANES) == 0 else 1
    step = R // groups
    outs = [_layernorm_rows(DEEPNORM_ALPHA * x2[i * step:(i + 1) * step]
                            + _dot(yv[i * step:(i + 1) * step], wout_ref[...]), g_ref[...], be_ref[...])
            for i in range(groups)]
    y = outs[0] if groups == 1 else jnp.concatenate(outs, axis=0)
    y_ref[...] = y.reshape(BB, TL, D_MODEL)


def _rglru(x, w_in, conv_w, conv_b, wga, bga, wgx, bgx, lam, w_out, ln_g, ln_b, h0, buf0, *, BB, TL):
    B, L, _ = x.shape
    per_seq = h0.shape[0] == B and B > 1
    K = CONV_W_LRU - 1
    kern = functools.partial(_rglru_kernel, BB=BB, TL=TL)
    consts = (w_in, conv_w, conv_b, wga, bga, wgx, bgx, lam, w_out, ln_g, ln_b)
    return pl.pallas_call(
        kern, grid=(B // BB, L // TL),
        in_specs=[pl.BlockSpec((BB, TL, D_MODEL), lambda b, t: (b, t, 0))]
        + [_const_spec(c.shape) for c in consts]
        + [_state_spec((BB, 1, D_RNN), per_seq), _state_spec((BB, K, D_RNN), per_seq)],
        out_specs=[
            pl.BlockSpec((BB, TL, D_MODEL), lambda b, t: (b, t, 0)),
            _state_spec((BB, 1, D_RNN), True), _state_spec((BB, K, D_RNN), True),
        ],
        out_shape=[
            jax.ShapeDtypeStruct((B, L, D_MODEL), f32),
            jax.ShapeDtypeStruct((B, 1, D_RNN), f32),
            jax.ShapeDtypeStruct((B, K, D_RNN), f32),
        ],
        compiler_params=_params(), name=f"rglru_b{BB}_t{TL}",
    )(x, *consts, h0, buf0)


def _rglru_tm_kernel(x_ref, win_ref, cw_ref, cb_ref, wga_ref, bga_ref, wgx_ref, bgx_ref, lam_ref,
                     wout_ref, g_ref, be_ref, h0_ref, buf0_ref,
                     y_ref, h_ref, buf_ref, perm_ref, permt_ref, hrun_ref, crun_ref, *, TL):
    B = SUBLANES
    NPART = LRU_TM_PARTS
    TP = TL // NPART
    RP = B * TP
    K = CONV_W_LRU - 1
    W = LRU_STEP_COLS
    nchunk = D_RNN // W
    step = pl.program_id(0)

    @pl.when(step == 0)
    def _init():
        r = lax.broadcasted_iota(jnp.int32, (RP, RP), 0)
        c = lax.broadcasted_iota(jnp.int32, (RP, RP), 1)
        hit = ((r & (B - 1)) * TP + (r >> LOG2_SUBLANES)) == c
        perm_ref[...] = jnp.where(hit, 1.0, 0.0).astype(bf16)
        hit_t = ((c & (B - 1)) * TP + (c >> LOG2_SUBLANES)) == r
        permt_ref[...] = jnp.where(hit_t, 1.0, 0.0).astype(bf16)
        hrun_ref[...] = jnp.broadcast_to(h0_ref[0], (B, D_RNN))
        for k in range(K):
            crun_ref[k] = jnp.broadcast_to(buf0_ref[0, k:k + 1, :], (B, D_RNN))

    def head(p):
        x2 = x_ref[:, p * TP:(p + 1) * TP, :].reshape(RP, D_MODEL)
        return x2, _dot(perm_ref[...], x2.astype(bf16)).astype(bf16)

    def project(xt, ci):
        c0 = ci * W
        return _dot(xt, win_ref[:, c0:c0 + W]), _dot(xt, win_ref[:, D_RNN + c0:D_RNN + c0 + W])

    def body(ci, gate, rnn):
        c0 = ci * W
        cs = slice(c0, c0 + W)
        rnn = rnn.reshape(TP, B, W)
        ext = jnp.concatenate([crun_ref[:, :, cs], rnn], axis=0)
        xc = cb_ref[:, cs] + ext[0:TP] * cw_ref[0:1, cs]
        for j in range(1, CONV_W_LRU):
            xc = xc + ext[j:j + TP] * cw_ref[j:j + 1, cs]
        crun_ref[:, :, cs] = rnn[TP - K:]
        xc = xc.reshape(RP, W)
        xcb = xc.astype(bf16)
        blocks = [(j, c0 // LRU_BLOCK + j) for j in range(W // LRU_BLOCK)]
        ga = jnp.concatenate([_dot(xcb[:, j * LRU_BLOCK:(j + 1) * LRU_BLOCK], wga_ref[n]) for j, n in blocks], axis=1)
        gx = jnp.concatenate([_dot(xcb[:, j * LRU_BLOCK:(j + 1) * LRU_BLOCK], wgx_ref[n]) for j, n in blocks], axis=1)
        rgate = jax.nn.sigmoid(ga + bga_ref[:, cs])
        igate = jax.nn.sigmoid(gx + bgx_ref[:, cs])
        nla = rgate * (-LRU_C * jax.nn.log_sigmoid(lam_ref[:, cs]))
        a2 = jnp.exp(-nla)
        a = a2.reshape(TP, B, W)
        w1 = jnp.tanh(nla) * (1.0 + a2 * a2)
        bv = (jnp.where(w1 > 0.0, w1 * lax.rsqrt(w1), 0.0) * (igate * xc)).reshape(TP, B, W)
        h = hrun_ref[:, cs]
        hs = []
        for t in range(TP):
            h = a[t] * h + bv[t]
            hs.append(h)
        hrun_ref[:, cs] = h
        return (_gelu(gate) * jnp.stack(hs, axis=0).reshape(RP, W)).astype(bf16)

    def tail(p, x2, ys):
        yv = _dot(permt_ref[...], jnp.concatenate(ys, axis=1)).astype(bf16)
        y = _layernorm_rows(DEEPNORM_ALPHA * x2 + _dot(yv, wout_ref[...]), g_ref[...], be_ref[...])
        y_ref[:, p * TP:(p + 1) * TP, :] = y.reshape(B, TP, D_MODEL)

    x2s, xts, nxt = {}, {}, None
    x2s[0], xts[0] = head(0)
    nxt = project(xts[0], 0)
    for p in range(NPART):
        if p + 1 < NPART:
            x2s[p + 1], xts[p + 1] = head(p + 1)
        ys = []
        for ci in range(nchunk):
            cur = nxt
            if ci + 1 < nchunk:
                nxt = project(xts[p], ci + 1)
            elif p + 1 < NPART:
                nxt = project(xts[p + 1], 0)
            ys.append(body(ci, *cur))
        tail(p, x2s[p], ys)

    @pl.when(step == pl.num_programs(0) - 1)
    def _emit_state():
        h_ref[:, 0, :] = hrun_ref[...]
        for k in range(K):
            buf_ref[:, k, :] = crun_ref[k]


def _rglru_tm(x, w_in, conv_w, conv_b, wga, bga, wgx, bgx, lam, w_out, ln_g, ln_b, h0, buf0, *, TL):
    B, L, _ = x.shape
    assert B == SUBLANES and h0.shape[0] == 1 and buf0.shape[0] == 1
    K = CONV_W_LRU - 1
    R = B * TL // LRU_TM_PARTS
    consts = (w_in, conv_w, conv_b, wga, bga, wgx, bgx, lam, w_out, ln_g, ln_b, h0, buf0)
    const_spec = lambda shape: pl.BlockSpec(shape, lambda t: (0,) * len(shape), pipeline_mode=pl.Buffered(1))
    return pl.pallas_call(
        functools.partial(_rglru_tm_kernel, TL=TL), grid=(L // TL,),
        in_specs=[pl.BlockSpec((B, TL, D_MODEL), lambda t: (0, t, 0))] + [const_spec(c.shape) for c in consts],
        out_specs=[
            pl.BlockSpec((B, TL, D_MODEL), lambda t: (0, t, 0)),
            pl.BlockSpec((B, 1, D_RNN), lambda t: (0, 0, 0)), pl.BlockSpec((B, K, D_RNN), lambda t: (0, 0, 0)),
        ],
        out_shape=[
            jax.ShapeDtypeStruct((B, L, D_MODEL), f32),
            jax.ShapeDtypeStruct((B, 1, D_RNN), f32),
            jax.ShapeDtypeStruct((B, K, D_RNN), f32),
        ],
        scratch_shapes=[
            pltpu.VMEM((R, R), bf16), pltpu.VMEM((R, R), bf16),
            pltpu.VMEM((B, D_RNN), f32),
            pltpu.VMEM((K, B, D_RNN), f32),
        ],
        compiler_params=pltpu.CompilerParams(dimension_semantics=("arbitrary",), vmem_limit_bytes=VMEM_LIMIT_BYTES),
        name=f"rglru_tm_t{TL}",
    )(x, *consts)


def _rope_tables(pos0, length):
    half = D_HEAD // 2
    pos = pos0 + jnp.arange(length, dtype=jnp.int32)
    inv = ROPE_BASE ** (-jnp.arange(half, dtype=f32) / half)
    ang = pos.astype(f32)[:, None] * inv[None, :]
    cos, sin = jnp.cos(ang), jnp.sin(ang)
    return jnp.concatenate([cos, cos], axis=1), jnp.concatenate([-sin, sin], axis=1)


def _trunk(x, pos0, states, p, *, BB, TL, BB0, TL0=None, lru_tm_steps=None, stack_ffn_bufs=False):
    s_ret, s_hgrn, h_lru, buf_lru, buf_ffn0, buf_ffn1 = states
    c2, s2 = _rope_tables(pos0, x.shape[1])
    x, s_ret, s_hgrn = _mixer0(x, c2, s2, p['w_in_ab'], p['w_out_ab'], p['hgrn_lb_logits'], p['hgrn_norm_w'],
                               p['ln_mix_g'][0], p['ln_mix_b'][0], s_ret, s_hgrn, BB=BB0, TL=TL0 or TL)
    x, buf_ffn0 = _ffn(x, p['w_ffn_up'], p['ffn_conv_w'][0], p['ffn_conv_b'][0], p['w_ffn_down'],
                       p['ln_ffn_g'][0], p['ln_ffn_b'][0], buf_ffn0, layer=0, BB=BB, TL=TL)
    lru_args = (x, p['w_in_c'], p['conv_w_c'], p['conv_b_c'], p['w_gate_a'], p['b_gate_a'], p['w_gate_x'],
                p['b_gate_x'], p['lru_lambda'], p['w_out_c'], p['ln_mix_g'][1], p['ln_mix_b'][1], h_lru, buf_lru)
    if lru_tm_steps is None:
        x, h_lru, buf_lru = _rglru(*lru_args, BB=BB, TL=TL)
    else:
        x, h_lru, buf_lru = _rglru_tm(*lru_args, TL=lru_tm_steps)
    x, buf_ffn1 = _ffn(x, p['w_ffn_up'], p['ffn_conv_w'][1], p['ffn_conv_b'][1], p['w_ffn_down'],
                       p['ln_ffn_g'][1], p['ln_ffn_b'][1], buf_ffn1, layer=1, BB=BB, TL=TL,
                       prev_buf=buf_ffn0 if stack_ffn_bufs else None)
    return x, (s_ret, s_hgrn, h_lru, buf_lru, buf_ffn0, buf_ffn1)


def kernel(x_prompt, x_sample, state_ret, state_hgrn, state_rglru_h, state_rglru_conv, state_ffn_conv, meta_tokens, w_in_ab, w_out_ab, hgrn_lb_logits, hgrn_norm_w, w_in_c, conv_w_c, conv_b_c, w_gate_a, b_gate_a, w_gate_x, b_gate_x, lru_lambda, w_out_c, ln_mix_g, ln_mix_b, ln_ffn_g, ln_ffn_b, w_ffn_up, ffn_conv_w, ffn_conv_b, w_ffn_down):
    row = lambda z: z.reshape(1, -1)
    wb = [w.astype(bf16) for w in (w_in_ab, w_out_ab, w_in_c, w_gate_a, w_gate_x, w_out_c,
                                   w_ffn_up.reshape(DEPTH * D_MODEL, 2 * D_FF),
                                   w_ffn_down.reshape(DEPTH * D_FF, D_MODEL))]
    p = dict(
        w_in_ab=wb[0], w_out_ab=wb[1],
        hgrn_lb_logits=hgrn_lb_logits, hgrn_norm_w=row(hgrn_norm_w),
        w_in_c=wb[2], conv_w_c=conv_w_c, conv_b_c=row(conv_b_c),
        w_gate_a=wb[3], b_gate_a=row(b_gate_a),
        w_gate_x=wb[4], b_gate_x=row(b_gate_x),
        lru_lambda=row(lru_lambda), w_out_c=wb[5],
        ln_mix_g=[row(ln_mix_g[i]) for i in range(DEPTH)], ln_mix_b=[row(ln_mix_b[i]) for i in range(DEPTH)],
        ln_ffn_g=[row(ln_ffn_g[i]) for i in range(DEPTH)], ln_ffn_b=[row(ln_ffn_b[i]) for i in range(DEPTH)],
        w_ffn_up=wb[6], w_ffn_down=wb[7],
        ffn_conv_w=[ffn_conv_w[i] for i in range(DEPTH)],
        ffn_conv_b=[row(ffn_conv_b[i]) for i in range(DEPTH)],
    )
    dt = x_prompt.dtype
    zero_states = (
        jnp.zeros((1, N_HEADS, D_HEAD, D_HEAD), dt), jnp.zeros((1, N_HEADS, D_HEAD, D_HEAD), dt),
        jnp.zeros((1, 1, D_RNN), dt), jnp.zeros((1, CONV_W_LRU - 1, D_RNN), dt),
        jnp.zeros((1, CONV_W_FFN - 1, D_FF), dt), jnp.zeros((1, CONV_W_FFN - 1, D_FF), dt),
    )
    _, meta_states = _trunk(meta_tokens.astype(dt)[None], 0, zero_states, p, BB=1, TL=N_META, BB0=1)
    y_prompt, ps = _trunk(x_prompt, N_META, meta_states, p, BB=1, TL=PROMPT_BLOCK_STEPS, BB0=1,
                          TL0=PROMPT_MIXER0_BLOCK_STEPS, lru_tm_steps=LRU_TM_BLOCK_STEPS, stack_ffn_bufs=True)
    sample_states = (state_ret, state_hgrn, state_rglru_h[:, None, :], state_rglru_conv,
                     state_ffn_conv, state_ffn_conv)
    y_sample, ss = _trunk(x_sample, PAST_LEN, sample_states, p, BB=32, TL=x_sample.shape[1], BB0=16,
                          stack_ffn_bufs=True)
    return (y_prompt, y_sample, ps[0], ss[0], ps[1], ss[1], ps[2][:, 0, :], ss[2][:, 0, :], ps[3], ss[3],
            ps[5], ss[5])
```

```python
import functools
import math

import jax
import jax.numpy as jnp
from jax import lax
from jax.experimental import pallas as pl
from jax.experimental.pallas import tpu as pltpu

f32 = jnp.float32
bf16 = jnp.bfloat16

D_MODEL = 1024
N_META = 16
PAST_LEN = 16384
N_HEADS = 4
D_HEAD = 128
SEG = N_HEADS * D_HEAD
ROPE_BASE = 10000.0
D_RNN = 1024
N_LRU_BLOCKS = 8
LRU_BLOCK = D_RNN // N_LRU_BLOCKS
CONV_W_LRU = 4
LRU_C = 8.0
D_FF = 2816
CONV_W_FFN = 3
LN_EPS = 1e-5
DEPTH = 2
DEEPNORM_ALPHA = (2.0 * DEPTH) ** 0.25
LOG_GAMMA = tuple(math.log1p(-(2.0 ** (-5.0 - h))) for h in range(N_HEADS))

SUBLANES = 8
CHUNK_ROWS = 128
LRU_STEP_COLS = 256
LRU_OUT_ROW_GROUPS = 2
PROMPT_BLOCK_STEPS = 1024
PROMPT_MIXER0_BLOCK_STEPS = 1024
LRU_TM_BLOCK_STEPS = 128
LRU_TM_PARTS = 4
LOG2_SUBLANES = SUBLANES.bit_length() - 1
VMEM_LIMIT_BYTES = 56 * 1024 * 1024
BIG_BLOCK_VMEM_LIMIT_BYTES = 62 * 1024 * 1024

_GELU_K1 = -2.0 * math.log2(math.e) * math.sqrt(2.0 / math.pi)
_GELU_K3 = _GELU_K1 * 0.044715


def _gelu(x):
    return x / (1.0 + jnp.exp2(x * (_GELU_K1 + _GELU_K3 * (x * x))))


def _dot(a, b):
    return jnp.dot(a, b, preferred_element_type=f32)


def _dot_nt(a, b, t_ref=None):
    if t_ref is None:
        return lax.dot_general(a, b, (((1,), (1,)), ((), ())), preferred_element_type=f32)
    t_ref[...] = b.T
    return jnp.dot(a, t_ref[...], preferred_element_type=f32)


def _layernorm_rows(y, g, b):
    mu = jnp.mean(y, axis=-1, keepdims=True)
    yc = y - mu
    var = jnp.mean(yc * yc, axis=-1, keepdims=True)
    return yc * lax.rsqrt(var + LN_EPS) * g + b


def _causal_conv(x3, carry, w_ref, b_ref, cs):
    BB, TL, W = x3.shape
    K = len(carry)
    w = [w_ref[j:j + 1, cs] for j in range(K + 1)]

    def shift(p, j):
        bnd = w[0] * carry[K - j]
        for i in range(1, j):
            bnd = bnd + w[i] * carry[K - j + i]
        if TL == SUBLANES:
            t = lax.broadcasted_iota(jnp.int32, p.shape, 1)
            return jnp.where(t == 0, bnd, pltpu.roll(p, 1, 1))
        assert BB == 1
        rolled = pltpu.roll(p.reshape(TL, W), 1, 0)
        t = lax.broadcasted_iota(jnp.int32, (SUBLANES, W), 0)
        head = jnp.where(t == 0, bnd.reshape(1, W), rolled[:SUBLANES])
        return jnp.concatenate([head, rolled[SUBLANES:]], axis=0).reshape(BB, TL, W)

    p = w[0] * x3
    for j in range(1, K + 1):
        p = w[j] * x3 + shift(p, j)
    return b_ref[:, cs] + p


def _const_spec(shape):
    nd = len(shape)
    return pl.BlockSpec(shape, lambda b, t: (0,) * nd, pipeline_mode=pl.Buffered(1))


def _state_spec(block, per_seq):
    nd = len(block)
    if per_seq:
        return pl.BlockSpec(block, lambda b, t: (b,) + (0,) * (nd - 1))
    return pl.BlockSpec(block, lambda b, t: (0,) * nd)


def _params(vmem_limit_bytes=VMEM_LIMIT_BYTES):
    return pltpu.CompilerParams(dimension_semantics=("arbitrary", "arbitrary"),
                                vmem_limit_bytes=vmem_limit_bytes)


def _mixer0_init_tables(dmask_ref, wq_ref, wk_ref, tri_ref, lvl_ref, *, cb, tl):
    T = cb * tl
    tl_shift = tl.bit_length() - 1
    r = lax.broadcasted_iota(jnp.int32, (T, T), 0)
    c = lax.broadcasted_iota(jnp.int32, (T, T), 1)
    same = (r >> tl_shift) == (c >> tl_shift)
    rel = (r & (tl - 1)) - (c & (tl - 1))
    causal = same & (rel >= 0)
    relf = jnp.maximum(rel, 0).astype(f32)
    tri_ref[...] = jnp.where(causal, 1.0, 0.0).astype(bf16)
    lvl = jnp.where(r == c, 0, -1)
    s, li = 1, 1
    while s < tl:
        blk = (r >> li) == (c >> li)
        hit = blk & ((r & (2 * s - 1)) >= s) & ((c & (2 * s - 1)) < s)
        lvl = jnp.where(hit, li, lvl)
        s, li = 2 * s, li + 1
    lvl_ref[...] = lvl
    tr = (lax.broadcasted_iota(jnp.int32, (T, D_HEAD), 0) & (tl - 1)).astype(f32)
    for h in range(N_HEADS):
        lg = LOG_GAMMA[h]
        dmask_ref[h] = jnp.where(causal, jnp.exp(lg * relf), 0.0)
        wq_ref[h] = jnp.exp(lg * (tr + 1.0))
        wk_ref[h] = jnp.exp(lg * ((tl - 1.0) - tr))


def _hgrn_scores(gqs, kks, fs, bs, lvl_ref, dot_nt, *, T, tl):
    n = len(gqs)
    row = lax.broadcasted_iota(jnp.int32, (T, D_HEAD), 0)
    ntile = T // SUBLANES

    def owned(li, g):
        return lvl_ref[g * SUBLANES:(g + 1) * SUBLANES, :] == li

    full = [dot_nt(gqs[u].astype(bf16), kks[u].astype(bf16)) for u in range(n)]
    tiles = [[jnp.where(owned(0, g), full[u][g * SUBLANES:(g + 1) * SUBLANES], 0.0) for g in range(ntile)]
             for u in range(n)]

    def take(u, li, p, first_tile, n_tiles, p_row0):
        for g in range(n_tiles):
            t = first_tile + g
            tiles[u][t] = jnp.where(owned(li, t), p[p_row0 + g * SUBLANES:p_row0 + (g + 1) * SUBLANES], tiles[u][t])

    s, li = 1, 1
    while s < tl:
        nblk = T // (2 * s)
        if s < SUBLANES:
            up = (row & (2 * s - 1)) >= s
            zs = []
            for u in range(n):
                gq, kk, f, b = gqs[u], kks[u], fs[u], bs[u]
                if s == 1:
                    z = jnp.where(up, gq * f, kk)
                elif s == 2:
                    w = row & 3
                    e = jnp.where(w == 0, pltpu.roll(f, T - 1, 0),
                                  jnp.where(w == 1, 1.0, jnp.where(w == 2, f, f * pltpu.roll(f, 1, 0))))
                    z = jnp.where(up, gq, kk) * e
                else:
                    parts = [jnp.abs(b[m * 2 * s:(m + 1) * 2 * s, :] - b[m * 2 * s + s - 1:m * 2 * s + s, :])
                             for m in range(nblk)]
                    z = jnp.where(up, gq, kk) * jnp.exp2(-jnp.concatenate(parts, axis=0))
                zs.append(z.astype(bf16))
            ps = [dot_nt(zb, zb) for zb in zs]
            for u in range(n):
                take(u, li, ps[u], 0, ntile, 0)
        else:
            zs, qus = [], []
            for u in range(n):
                gq, kk, b = gqs[u], kks[u], bs[u]
                both, upper = [], []
                for m in range(nblk):
                    r0 = m * 2 * s
                    beta = b[r0 + s - 1:r0 + s, :]
                    k_lo = kk[r0:r0 + s] * jnp.exp2(beta - b[r0:r0 + s])
                    q_hi = gq[r0 + s:r0 + 2 * s] * jnp.exp2(b[r0 + s:r0 + 2 * s] - beta)
                    both += [k_lo, q_hi]
                    upper.append(q_hi)
                zs.append(jnp.concatenate(both, axis=0).astype(bf16))
                qus.append((upper[0] if nblk == 1 else jnp.concatenate(upper, axis=0)).astype(bf16))
            ps = [dot_nt(qus[u], zs[u]) for u in range(n)]
            for u in range(n):
                for m in range(nblk):
                    take(u, li, ps[u], (m * 2 * s + s) // SUBLANES, s // SUBLANES, m * s)
        s, li = 2 * s, li + 1
    return [jnp.concatenate(tiles[u], axis=0).astype(bf16) for u in range(n)]


def _mixer0_kernel(x_ref, c2_ref, s2_ref, win_ref, wout_ref, lbl_ref, nw_ref, g_ref, be_ref,
                   sret0_ref, shg0_ref, y_ref, sret_ref, shg_ref,
                   dmask_ref, wq_ref, wk_ref, tri_ref, lvl_ref, kt_ref, *, BB, TL, cb, tl):
    T = cb * tl
    R = BB * TL
    nch = R // T
    slots = iter(range(kt_ref.shape[0]))

    def dot_nt(a, b):
        return _dot_nt(a, b, kt_ref.at[next(slots)] if (cb == 1 and T == CHUNK_ROWS) else None)

    @pl.when((pl.program_id(0) == 0) & (pl.program_id(1) == 0))
    def _init_tables():
        _mixer0_init_tables(dmask_ref, wq_ref, wk_ref, tri_ref, lvl_ref, cb=cb, tl=tl)

    @pl.when(pl.program_id(1) == 0)
    def _init_state():
        sret_ref[...] = sret0_ref[...]
        shg_ref[...] = shg0_ref[...]

    x2 = x_ref[...].reshape(R, D_MODEL)
    xb = x2.astype(bf16)

    def proj(i):
        return _dot(xb, win_ref[:, i * SEG:(i + 1) * SEG])

    lbl = lbl_ref[...]
    le = jnp.exp(lbl - jnp.max(lbl, axis=0, keepdims=True))
    lb = le[0:1] / jnp.sum(le, axis=0, keepdims=True)
    nw = nw_ref[...]

    def seqs(z):
        return z.reshape(cb, tl, D_HEAD)

    def rows(z3):
        return z3.reshape(T, D_HEAD)

    def qs(q3, s3):
        if cb == 1:
            return _dot(q3[0], s3[0])[None]
        return jnp.einsum('bqd,bdv->bqv', q3, s3, preferred_element_type=f32)

    def ktv(k3, v3):
        if cb == 1:
            return lax.dot_general(k3[0], v3[0], (((0,), (0,)), ((), ())),
                                   preferred_element_type=f32)[None]
        return jnp.einsum('btd,btv->bdv', k3, v3, preferred_element_type=f32)

    units = [(c, h) for c in range(nch) for h in range(N_HEADS)]
    nu = len(units)

    def sl(z, c, h):
        return z[c * T:(c + 1) * T, h * D_HEAD:(h + 1) * D_HEAD]

    def seq0(c):
        return 0 if BB == 1 else c * cb

    def rope_tables(c):
        ts = slice(c * tl, (c + 1) * tl) if BB == 1 else slice(0, tl)
        return c2_ref[ts, :][None], s2_ref[ts, :][None]

    def rope(z, cos2, sin2):
        return rows(seqs(z) * cos2 + seqs(pltpu.roll(z, D_HEAD // 2, 1)) * sin2)

    tabs = [rope_tables(c) for c in range(nch)]
    ktabs = [(c2 * (D_HEAD ** -0.5), s2 * (D_HEAD ** -0.5)) for c2, s2 in tabs]

    rq, rk = proj(0), proj(1)
    q = [rope(sl(rq, c, h), *tabs[c]) for c, h in units]
    k = [rope(sl(rk, c, h), *ktabs[c]) for c, h in units]
    rv = proj(2)
    v = [sl(rv, c, h).astype(bf16) for c, h in units]
    att = [dot_nt(q[u].astype(bf16), k[u].astype(bf16)) for u in range(nu)]
    att = [(att[u] * dmask_ref[h]).astype(bf16) for u, (c, h) in enumerate(units)]
    o_ret = [_dot(att[u], v[u]) for u in range(nu)]
    qw = [seqs((q[u] * wq_ref[h]).astype(bf16)) for u, (c, h) in enumerate(units)]
    upd_ret = [ktv(seqs((k[u] * wk_ref[h]).astype(bf16)), seqs(v[u])) for u, (c, h) in enumerate(units)]

    hf = proj(5)
    lbs = [lb[:, h * D_HEAD:(h + 1) * D_HEAD] for h in range(N_HEADS)]
    f = [lbs[h] + (1.0 - lbs[h]) * jax.nn.sigmoid(sl(hf, c, h)) for c, h in units]
    lf = [jnp.log2(z) for z in f]
    kk = [1.0 - z for z in f]
    hq, hi = proj(4), proj(6)
    gq = [sl(hq, c, h) for c, h in units]
    gv = [sl(hi, c, h).astype(bf16) for c, h in units]
    split = []
    for z in lf:
        l1 = z.astype(bf16)
        r1 = z - l1.astype(f32)
        l2 = r1.astype(bf16)
        l3 = (r1 - l2.astype(f32)).astype(bf16)
        split.append(jnp.concatenate([l1, l2, l3], axis=1))
    cs = [_dot(tri_ref[...], z) for z in split]
    b = [z[:, :D_HEAD] + z[:, D_HEAD:2 * D_HEAD] + z[:, 2 * D_HEAD:] for z in cs]
    scores = _hgrn_scores(gq, kk, f, b, lvl_ref, dot_nt, T=T, tl=tl)
    o_hg = [_dot(scores[u], gv[u]) for u in range(nu)]
    qe = [seqs((gq[u] * jnp.exp2(b[u])).astype(bf16)) for u in range(nu)]
    b3 = [seqs(z) for z in b]
    bl = [z[:, tl - 1:tl, :] for z in b3]
    upd_hg = [ktv((seqs(kk[u]) * jnp.exp2(bl[u] - b3[u])).astype(bf16), seqs(gv[u])) for u in range(nu)]
    decay = []
    for z in bl:
        ez = jnp.exp2(z)
        d = [jnp.broadcast_to(ez[j], (D_HEAD, D_HEAD)).T for j in range(cb)]
        decay.append(d[0][None] if cb == 1 else jnp.stack(d))

    s_ret, s_hg = {}, {}
    for u, (c, h) in enumerate(units):
        s0 = seq0(c)
        first = BB > 1 or c == 0
        sp = sret_ref[s0:s0 + cb, h] if first else s_ret[h]
        sg = shg_ref[s0:s0 + cb, h] if first else s_hg[h]
        o_ret[u] = o_ret[u] + rows(qs(qw[u], sp.astype(bf16)))
        o_hg[u] = o_hg[u] + rows(qs(qe[u], sg.astype(bf16)))
        s_ret[h] = math.exp(LOG_GAMMA[h] * tl) * sp + upd_ret[u]
        s_hg[h] = decay[u] * sg + upd_hg[u]
        if BB > 1 or c == nch - 1:
            sret_ref[s0:s0 + cb, h] = s_ret[h]
            shg_ref[s0:s0 + cb, h] = s_hg[h]

    rg, hg = proj(3), proj(7)
    head_out = {}
    for u, (c, h) in enumerate(units):
        o = o_ret[u]
        mu = jnp.mean(o, axis=-1, keepdims=True)
        oc = o - mu
        var = jnp.mean(oc * oc, axis=-1, keepdims=True)
        head_out[(c, h)] = oc * lax.rsqrt(var + LN_EPS) * jax.nn.silu(sl(rg, c, h))
        og = o_hg[u]
        ms = jnp.mean(og * og, axis=-1, keepdims=True)
        head_out[(c, N_HEADS + h)] = og * lax.rsqrt(ms + LN_EPS) * nw * jax.nn.silu(sl(hg, c, h))
    chunk_out = [jnp.concatenate([head_out[(c, j)] for j in range(2 * N_HEADS)], axis=1).astype(bf16)
                 for c in range(nch)]
    mixed = chunk_out[0] if nch == 1 else jnp.concatenate(chunk_out, axis=0)
    m = _dot(mixed, wout_ref[...])
    y = _layernorm_rows(DEEPNORM_ALPHA * x2 + m, g_ref[...], be_ref[...])
    y_ref[...] = y.reshape(BB, TL, D_MODEL)


def _mixer0(x, c2, s2, w_in, w_out, lb_logits, norm_w, ln_g, ln_b, sret0, shg0, *, BB, TL):
    B, L, _ = x.shape
    R = BB * TL
    T = min(R, CHUNK_ROWS)
    if BB == 1:
        cb, tl = 1, T
    else:
        assert T % TL == 0
        cb, tl = T // TL, TL
    per_seq = sret0.shape[0] == B and B > 1
    st_block = (BB, N_HEADS, D_HEAD, D_HEAD)
    n_score_dots = N_HEADS * (R // T) * (2 + (tl - 1).bit_length()) if cb == 1 else 1
    kern = functools.partial(_mixer0_kernel, BB=BB, TL=TL, cb=cb, tl=tl)
    return pl.pallas_call(
        kern, grid=(B // BB, L // TL),
        in_specs=[
            pl.BlockSpec((BB, TL, D_MODEL), lambda b, t: (b, t, 0)),
            pl.BlockSpec((TL, D_HEAD), lambda b, t: (t, 0)),
            pl.BlockSpec((TL, D_HEAD), lambda b, t: (t, 0)),
            _const_spec(w_in.shape), _const_spec(w_out.shape), _const_spec(lb_logits.shape),
            _const_spec(norm_w.shape), _const_spec(ln_g.shape), _const_spec(ln_b.shape),
            _state_spec(st_block, per_seq), _state_spec(st_block, per_seq),
        ],
        out_specs=[
            pl.BlockSpec((BB, TL, D_MODEL), lambda b, t: (b, t, 0)),
            _state_spec(st_block, True), _state_spec(st_block, True),
        ],
        out_shape=[
            jax.ShapeDtypeStruct((B, L, D_MODEL), f32),
            jax.ShapeDtypeStruct((B, N_HEADS, D_HEAD, D_HEAD), f32),
            jax.ShapeDtypeStruct((B, N_HEADS, D_HEAD, D_HEAD), f32),
        ],
        scratch_shapes=[
            pltpu.VMEM((N_HEADS, T, T), f32),
            pltpu.VMEM((N_HEADS, T, D_HEAD), f32),
            pltpu.VMEM((N_HEADS, T, D_HEAD), f32),
            pltpu.VMEM((T, T), bf16),
            pltpu.VMEM((T, T), jnp.int32),
            pltpu.VMEM((n_score_dots, D_HEAD, T), bf16),
        ],
        compiler_params=_params(BIG_BLOCK_VMEM_LIMIT_BYTES), name=f"mixer0_b{BB}_t{TL}",
    )(x, c2, s2, w_in, w_out, lb_logits, norm_w, ln_g, ln_b, sret0, shg0)


def _ffn_kernel(x_ref, wup_ref, cw_ref, cb_ref, wdn_ref, g_ref, be_ref, buf0_ref, *rest, BB, TL, stacked):
    R = BB * TL
    K = CONV_W_FFN - 1
    if stacked:
        prev_ref, y_ref, out_ref, u_ref = rest
        buf_ref = out_ref.at[1]
    else:
        y_ref, buf_ref, u_ref = rest

    @pl.when(pl.program_id(1) == 0)
    def _init_state():
        buf_ref[...] = buf0_ref[...]
        if stacked:
            out_ref[0] = prev_ref[...]

    x2 = x_ref[...].reshape(R, D_MODEL)
    xb = x2.astype(bf16)
    u = _dot(xb, wup_ref[:, :D_FF])
    v = _dot(xb, wup_ref[:, D_FF:])
    u_ref[:, SUBLANES - K:SUBLANES, :] = buf_ref[...]
    u_ref[:, SUBLANES:, :] = u.reshape(BB, TL, D_FF)
    cw = cw_ref[...]
    uc = cb_ref[...] + u_ref[:, SUBLANES - K:SUBLANES - K + TL, :] * cw[0:1]
    for j in range(1, CONV_W_FFN):
        uc = uc + u_ref[:, SUBLANES - K + j:SUBLANES - K + j + TL, :] * cw[j:j + 1]
    buf_ref[...] = u_ref[:, SUBLANES + TL - K:SUBLANES + TL, :]
    hmid = _gelu(uc.reshape(R, D_FF)) * v
    fo = _dot(hmid.astype(bf16), wdn_ref[...])
    y = _layernorm_rows(DEEPNORM_ALPHA * x2 + fo, g_ref[...], be_ref[...])
    y_ref[...] = y.reshape(BB, TL, D_MODEL)


def _ffn(x, w_up, conv_w, conv_b, w_down, ln_g, ln_b, buf0, *, layer, BB, TL, prev_buf=None):
    B, L, _ = x.shape
    per_seq = buf0.shape[-3] == B and B > 1
    K = CONV_W_FFN - 1
    stacked = prev_buf is not None
    kern = functools.partial(_ffn_kernel, BB=BB, TL=TL, stacked=stacked)
    buf_spec = (pl.BlockSpec((2, BB, K, D_FF), lambda b, t: (0, b, 0, 0)) if stacked
                else _state_spec((BB, K, D_FF), True))
    buf_shape = (2, B, K, D_FF) if stacked else (B, K, D_FF)
    buf0_spec = (_state_spec((BB, K, D_FF), per_seq) if buf0.ndim == 3
                 else pl.BlockSpec((None, BB, K, D_FF), lambda b, t: (layer, b, 0, 0)))
    layer_spec = lambda rows, cols: pl.BlockSpec((rows, cols), lambda b, t: (layer, 0), pipeline_mode=pl.Buffered(1))
    return pl.pallas_call(
        kern, grid=(B // BB, L // TL),
        in_specs=[
            pl.BlockSpec((BB, TL, D_MODEL), lambda b, t: (b, t, 0)),
            layer_spec(D_MODEL, 2 * D_FF), _const_spec(conv_w.shape), _const_spec(conv_b.shape),
            layer_spec(D_FF, D_MODEL), _const_spec(ln_g.shape), _const_spec(ln_b.shape),
            buf0_spec,
        ] + ([_state_spec((BB, K, D_FF), True)] if stacked else []),
        out_specs=[pl.BlockSpec((BB, TL, D_MODEL), lambda b, t: (b, t, 0)), buf_spec],
        out_shape=[
            jax.ShapeDtypeStruct((B, L, D_MODEL), f32),
            jax.ShapeDtypeStruct(buf_shape, f32),
        ],
        scratch_shapes=[pltpu.VMEM((BB, SUBLANES + TL, D_FF), f32)],
        compiler_params=_params(BIG_BLOCK_VMEM_LIMIT_BYTES), name=f"ffn_b{BB}_t{TL}",
    )(x, w_up, conv_w, conv_b, w_down, ln_g, ln_b, buf0, *([prev_buf] if stacked else []))


def _rglru_kernel(x_ref, win_ref, cw_ref, cb_ref, wga_ref, bga_ref, wgx_ref, bgx_ref, lam_ref,
                  wout_ref, g_ref, be_ref, h0_ref, buf0_ref,
                  y_ref, h_ref, buf_ref, *, BB, TL):
    R = BB * TL
    K = CONV_W_LRU - 1
    G = TL // SUBLANES

    @pl.when(pl.program_id(1) == 0)
    def _init_state():
        h_ref[...] = h0_ref[...]
        buf_ref[...] = buf0_ref[...]

    x2 = x_ref[...].reshape(R, D_MODEL)
    xb = x2.astype(bf16)
    W = LRU_STEP_COLS
    nchunk = D_RNN // W
    carry = buf_ref[...]
    new_carry = []
    sub = lax.broadcasted_iota(jnp.int32, (R // SUBLANES, SUBLANES, W), 1)
    h0 = h_ref[...]

    def project(ci):
        c0 = ci * W
        return _dot(xb, win_ref[:, c0:c0 + W]), _dot(xb, win_ref[:, D_RNN + c0:D_RNN + c0 + W])

    def recur(ci, gate, rnn):
        c0 = ci * W
        cs = slice(c0, c0 + W)
        rnn3 = rnn.reshape(BB, TL, W)
        xc = _causal_conv(rnn3, [carry[k][:, None, cs] for k in range(K)], cw_ref, cb_ref, cs).reshape(R, W)
        new_carry.append(jnp.stack([rnn3[:, TL - K + k, :] for k in range(K)], axis=0))
        xcb = xc.astype(bf16)
        blocks = [(j, c0 // LRU_BLOCK + j) for j in range(W // LRU_BLOCK)]
        ga = jnp.concatenate([_dot(xcb[:, j * LRU_BLOCK:(j + 1) * LRU_BLOCK], wga_ref[n]) for j, n in blocks], axis=1)
        gx = jnp.concatenate([_dot(xcb[:, j * LRU_BLOCK:(j + 1) * LRU_BLOCK], wgx_ref[n]) for j, n in blocks], axis=1)
        rgate = jax.nn.sigmoid(ga + bga_ref[:, cs])
        igate = jax.nn.sigmoid(gx + bgx_ref[:, cs])
        nla = rgate * (-LRU_C * jax.nn.log_sigmoid(lam_ref[:, cs]))
        a = jnp.exp(-nla)
        w1 = jnp.tanh(nla) * (1.0 + a * a)
        bv = jnp.where(w1 > 0.0, w1 * lax.rsqrt(w1), 0.0) * (igate * xc)
        a4 = a.reshape(R // SUBLANES, SUBLANES, W)
        b4 = bv.reshape(R // SUBLANES, SUBLANES, W)
        for d in (1, 2, 4):
            keep = sub >= d
            b4 = jnp.where(keep, a4 * pltpu.roll(b4, d, 1) + b4, b4)
            a4 = jnp.where(keep, a4 * pltpu.roll(a4, d, 1), a4)
        a5 = a4.reshape(BB, G, SUBLANES, W)
        b5 = b4.reshape(BB, G, SUBLANES, W)
        hc = h0[:, None, cs]
        tiles = []
        for gi in range(G):
            hg = a5[:, gi] * hc + b5[:, gi]
            hc = hg[:, SUBLANES - 1:SUBLANES, :]
            tiles.append(hg)
        hseq = tiles[0] if G == 1 else jnp.concatenate(tiles, axis=1)
        return (_gelu(gate) * hseq.reshape(R, W)).astype(bf16), hc

    nxt = project(0)
    ys, h_last = [], []
    for ci in range(nchunk):
        cur, nxt = nxt, (project(ci + 1) if ci + 1 < nchunk else None)
        yv, hc = recur(ci, *cur)
        ys.append(yv)
        h_last.append(hc)
    buf_ref[...] = jnp.concatenate(new_carry, axis=2)
    h_ref[...] = jnp.concatenate(h_last, axis=2).reshape(BB, D_RNN)
    yv = jnp.concatenate(ys, axis=1)
    groups = LRU_OUT_ROW_GROUPS if R % (LRU_OUT_ROW_GROUPS * 2 * SUBLANES) == 0 else 1
    step = R // groups
    outs = [_layernorm_rows(DEEPNORM_ALPHA * x2[i * step:(i + 1) * step]
                            + _dot(yv[i * step:(i + 1) * step], wout_ref[...]), g_ref[...], be_ref[...])
            for i in range(groups)]
    y = outs[0] if groups == 1 else jnp.concatenate(outs, axis=0)
    y_ref[...] = y.reshape(BB, TL, D_MODEL)


def _rglru(x, w_in, conv_w, conv_b, wga, bga, wgx, bgx, lam, w_out, ln_g, ln_b, h0, buf0, *, BB, TL):
    B, L, _ = x.shape
    per_seq = h0.shape[0] == B and B > 1
    K = CONV_W_LRU - 1
    kern = functools.partial(_rglru_kernel, BB=BB, TL=TL)
    consts = (w_in, conv_w, conv_b, wga, bga, wgx, bgx, lam, w_out, ln_g, ln_b)

    def buf_spec(per):
        return pl.BlockSpec((K, BB, D_RNN), lambda b, t: (0, b if per else 0, 0))

    return pl.pallas_call(
        kern, grid=(B // BB, L // TL),
        in_specs=[pl.BlockSpec((BB, TL, D_MODEL), lambda b, t: (b, t, 0))]
        + [_const_spec(c.shape) for c in consts]
        + [_state_spec((BB, D_RNN), per_seq), buf_spec(per_seq)],
        out_specs=[
            pl.BlockSpec((BB, TL, D_MODEL), lambda b, t: (b, t, 0)),
            _state_spec((BB, D_RNN), True), buf_spec(True),
        ],
        out_shape=[
            jax.ShapeDtypeStruct((B, L, D_MODEL), f32),
            jax.ShapeDtypeStruct((B, D_RNN), f32),
            jax.ShapeDtypeStruct((K, B, D_RNN), f32),
        ],
        compiler_params=_params(), name=f"rglru_b{BB}_t{TL}",
    )(x, *consts, h0, buf0)


def _rglru_tm_kernel(x_ref, win_ref, cw_ref, cb_ref, wga_ref, bga_ref, wgx_ref, bgx_ref, lam_ref,
                     wout_ref, g_ref, be_ref, h0_ref, buf0_ref,
                     y_ref, h_ref, buf_ref, perm_ref, permt_ref, hrun_ref, crun_ref, *, TL):
    B = SUBLANES
    NPART = LRU_TM_PARTS
    TP = TL // NPART
    RP = B * TP
    K = CONV_W_LRU - 1
    W = LRU_STEP_COLS
    nchunk = D_RNN // W
    step = pl.program_id(0)

    @pl.when(step == 0)
    def _init():
        r = lax.broadcasted_iota(jnp.int32, (RP, RP), 0)
        c = lax.broadcasted_iota(jnp.int32, (RP, RP), 1)
        hit = ((r & (B - 1)) * TP + (r >> LOG2_SUBLANES)) == c
        perm_ref[...] = jnp.where(hit, 1.0, 0.0).astype(bf16)
        hit_t = ((c & (B - 1)) * TP + (c >> LOG2_SUBLANES)) == r
        permt_ref[...] = jnp.where(hit_t, 1.0, 0.0).astype(bf16)
        hrun_ref[...] = jnp.broadcast_to(h0_ref[...], (B, D_RNN))
        for k in range(K):
            crun_ref[k] = jnp.broadcast_to(buf0_ref[k], (B, D_RNN))

    def head(p):
        x2 = x_ref[:, p * TP:(p + 1) * TP, :].reshape(RP, D_MODEL)
        return x2, _dot(perm_ref[...], x2.astype(bf16)).astype(bf16)

    def project(xt, ci):
        c0 = ci * W
        return _dot(xt, win_ref[:, c0:c0 + W]), _dot(xt, win_ref[:, D_RNN + c0:D_RNN + c0 + W])

    def body(ci, gate, rnn):
        c0 = ci * W
        cs = slice(c0, c0 + W)
        rnn = rnn.reshape(TP, B, W)
        ext = jnp.concatenate([crun_ref[:, :, cs], rnn], axis=0)
        xc = cb_ref[:, cs] + ext[0:TP] * cw_ref[0:1, cs]
        for j in range(1, CONV_W_LRU):
            xc = xc + ext[j:j + TP] * cw_ref[j:j + 1, cs]
        crun_ref[:, :, cs] = rnn[TP - K:]
        xc = xc.reshape(RP, W)
        xcb = xc.astype(bf16)
        blocks = [(j, c0 // LRU_BLOCK + j) for j in range(W // LRU_BLOCK)]
        ga = jnp.concatenate([_dot(xcb[:, j * LRU_BLOCK:(j + 1) * LRU_BLOCK], wga_ref[n]) for j, n in blocks], axis=1)
        gx = jnp.concatenate([_dot(xcb[:, j * LRU_BLOCK:(j + 1) * LRU_BLOCK], wgx_ref[n]) for j, n in blocks], axis=1)
        rgate = jax.nn.sigmoid(ga + bga_ref[:, cs])
        igate = jax.nn.sigmoid(gx + bgx_ref[:, cs])
        nla = rgate * (-LRU_C * jax.nn.log_sigmoid(lam_ref[:, cs]))
        a2 = jnp.exp(-nla)
        a = a2.reshape(TP, B, W)
        w1 = jnp.tanh(nla) * (1.0 + a2 * a2)
        bv = (jnp.where(w1 > 0.0, w1 * lax.rsqrt(w1), 0.0) * (igate * xc)).reshape(TP, B, W)
        h = hrun_ref[:, cs]
        hs = []
        for t in range(TP):
            h = a[t] * h + bv[t]
            hs.append(h)
        hrun_ref[:, cs] = h
        return (_gelu(gate) * jnp.stack(hs, axis=0).reshape(RP, W)).astype(bf16)

    def tail(p, x2, ys):
        yv = _dot(permt_ref[...], jnp.concatenate(ys, axis=1)).astype(bf16)
        y = _layernorm_rows(DEEPNORM_ALPHA * x2 + _dot(yv, wout_ref[...]), g_ref[...], be_ref[...])
        y_ref[:, p * TP:(p + 1) * TP, :] = y.reshape(B, TP, D_MODEL)

    x2s, xts, nxt = {}, {}, None
    x2s[0], xts[0] = head(0)
    nxt = project(xts[0], 0)
    for p in range(NPART):
        if p + 1 < NPART:
            x2s[p + 1], xts[p + 1] = head(p + 1)
        ys = []
        for ci in range(nchunk):
            cur = nxt
            if ci + 1 < nchunk:
                nxt = project(xts[p], ci + 1)
            elif p + 1 < NPART:
                nxt = project(xts[p + 1], 0)
            ys.append(body(ci, *cur))
        tail(p, x2s[p], ys)

    @pl.when(step == pl.num_programs(0) - 1)
    def _emit_state():
        h_ref[...] = hrun_ref[...]
        buf_ref[...] = crun_ref[...]


def _rglru_tm(x, w_in, conv_w, conv_b, wga, bga, wgx, bgx, lam, w_out, ln_g, ln_b, h0, buf0, *, TL):
    B, L, _ = x.shape
    assert B == SUBLANES and h0.shape[0] == 1 and buf0.shape[1] == 1
    K = CONV_W_LRU - 1
    R = B * TL // LRU_TM_PARTS
    consts = (w_in, conv_w, conv_b, wga, bga, wgx, bgx, lam, w_out, ln_g, ln_b, h0, buf0)
    const_spec = lambda shape: pl.BlockSpec(shape, lambda t: (0,) * len(shape), pipeline_mode=pl.Buffered(1))
    return pl.pallas_call(
        functools.partial(_rglru_tm_kernel, TL=TL), grid=(L // TL,),
        in_specs=[pl.BlockSpec((B, TL, D_MODEL), lambda t: (0, t, 0))] + [const_spec(c.shape) for c in consts],
        out_specs=[
            pl.BlockSpec((B, TL, D_MODEL), lambda t: (0, t, 0)),
            pl.BlockSpec((B, D_RNN), lambda t: (0, 0)), pl.BlockSpec((K, B, D_RNN), lambda t: (0, 0, 0)),
        ],
        out_shape=[
            jax.ShapeDtypeStruct((B, L, D_MODEL), f32),
            jax.ShapeDtypeStruct((B, D_RNN), f32),
            jax.ShapeDtypeStruct((K, B, D_RNN), f32),
        ],
        scratch_shapes=[
            pltpu.VMEM((R, R), bf16), pltpu.VMEM((R, R), bf16),
            pltpu.VMEM((B, D_RNN), f32),
            pltpu.VMEM((K, B, D_RNN), f32),
        ],
        compiler_params=pltpu.CompilerParams(dimension_semantics=("arbitrary",), vmem_limit_bytes=VMEM_LIMIT_BYTES),
        name=f"rglru_tm_t{TL}",
    )(x, *consts)


def _rope_tables(pos0, length):
    half = D_HEAD // 2
    pos = pos0 + jnp.arange(length, dtype=jnp.int32)
    inv = ROPE_BASE ** (-jnp.arange(half, dtype=f32) / half)
    ang = pos.astype(f32)[:, None] * inv[None, :]
    cos, sin = jnp.cos(ang), jnp.sin(ang)
    return jnp.concatenate([cos, cos], axis=1), jnp.concatenate([-sin, sin], axis=1)


def _trunk(x, pos0, states, p, *, BB, TL, BB0, TL0=None, lru_tm_steps=None, stack_ffn_bufs=False):
    s_ret, s_hgrn, h_lru, buf_lru, buf_ffn0, buf_ffn1 = states
    c2, s2 = _rope_tables(pos0, x.shape[1])
    x, s_ret, s_hgrn = _mixer0(x, c2, s2, p['w_in_ab'], p['w_out_ab'], p['hgrn_lb_logits'], p['hgrn_norm_w'],
                               p['ln_mix_g'][0], p['ln_mix_b'][0], s_ret, s_hgrn, BB=BB0, TL=TL0 or TL)
    x, buf_ffn0 = _ffn(x, p['w_ffn_up'], p['ffn_conv_w'][0], p['ffn_conv_b'][0], p['w_ffn_down'],
                       p['ln_ffn_g'][0], p['ln_ffn_b'][0], buf_ffn0, layer=0, BB=BB, TL=TL)
    lru_args = (x, p['w_in_c'], p['conv_w_c'], p['conv_b_c'], p['w_gate_a'], p['b_gate_a'], p['w_gate_x'],
                p['b_gate_x'], p['lru_lambda'], p['w_out_c'], p['ln_mix_g'][1], p['ln_mix_b'][1], h_lru, buf_lru)
    if lru_tm_steps is None:
        x, h_lru, buf_lru = _rglru(*lru_args, BB=BB, TL=TL)
    else:
        x, h_lru, buf_lru = _rglru_tm(*lru_args, TL=lru_tm_steps)
    x, buf_ffn1 = _ffn(x, p['w_ffn_up'], p['ffn_conv_w'][1], p['ffn_conv_b'][1], p['w_ffn_down'],
                       p['ln_ffn_g'][1], p['ln_ffn_b'][1], buf_ffn1, layer=1, BB=BB, TL=TL,
                       prev_buf=buf_ffn0 if stack_ffn_bufs else None)
    return x, (s_ret, s_hgrn, h_lru, buf_lru, buf_ffn0, buf_ffn1)


def kernel(x_prompt, x_sample, state_ret, state_hgrn, state_rglru_h, state_rglru_conv, state_ffn_conv, meta_tokens, w_in_ab, w_out_ab, hgrn_lb_logits, hgrn_norm_w, w_in_c, conv_w_c, conv_b_c, w_gate_a, b_gate_a, w_gate_x, b_gate_x, lru_lambda, w_out_c, ln_mix_g, ln_mix_b, ln_ffn_g, ln_ffn_b, w_ffn_up, ffn_conv_w, ffn_conv_b, w_ffn_down):
    row = lambda z: z.reshape(1, -1)
    wb = [w.astype(bf16) for w in (w_in_ab, w_out_ab, w_in_c, w_gate_a, w_gate_x, w_out_c,
                                   w_ffn_up.reshape(DEPTH * D_MODEL, 2 * D_FF),
                                   w_ffn_down.reshape(DEPTH * D_FF, D_MODEL))]
    p = dict(
        w_in_ab=wb[0], w_out_ab=wb[1],
        hgrn_lb_logits=hgrn_lb_logits, hgrn_norm_w=row(hgrn_norm_w),
        w_in_c=wb[2], conv_w_c=conv_w_c, conv_b_c=row(conv_b_c),
        w_gate_a=wb[3], b_gate_a=row(b_gate_a),
        w_gate_x=wb[4], b_gate_x=row(b_gate_x),
        lru_lambda=row(lru_lambda), w_out_c=wb[5],
        ln_mix_g=[row(ln_mix_g[i]) for i in range(DEPTH)], ln_mix_b=[row(ln_mix_b[i]) for i in range(DEPTH)],
        ln_ffn_g=[row(ln_ffn_g[i]) for i in range(DEPTH)], ln_ffn_b=[row(ln_ffn_b[i]) for i in range(DEPTH)],
        w_ffn_up=wb[6], w_ffn_down=wb[7],
        ffn_conv_w=[ffn_conv_w[i] for i in range(DEPTH)],
        ffn_conv_b=[row(ffn_conv_b[i]) for i in range(DEPTH)],
    )
    dt = x_prompt.dtype
    zero_states = (
        jnp.zeros((1, N_HEADS, D_HEAD, D_HEAD), dt), jnp.zeros((1, N_HEADS, D_HEAD, D_HEAD), dt),
        jnp.zeros((1, D_RNN), dt), jnp.zeros((CONV_W_LRU - 1, 1, D_RNN), dt),
        jnp.zeros((1, CONV_W_FFN - 1, D_FF), dt), jnp.zeros((1, CONV_W_FFN - 1, D_FF), dt),
    )
    _, meta_states = _trunk(meta_tokens.astype(dt)[None], 0, zero_states, p, BB=1, TL=N_META, BB0=1)
    y_prompt, ps = _trunk(x_prompt, N_META, meta_states, p, BB=1, TL=PROMPT_BLOCK_STEPS, BB0=1,
                          TL0=PROMPT_MIXER0_BLOCK_STEPS, lru_tm_steps=LRU_TM_BLOCK_STEPS, stack_ffn_bufs=True)
    sample_states = (state_ret, state_hgrn, state_rglru_h, jnp.transpose(state_rglru_conv, (1, 0, 2)),
                     state_ffn_conv, state_ffn_conv)
    y_sample, ss = _trunk(x_sample, PAST_LEN, sample_states, p, BB=32, TL=x_sample.shape[1], BB0=16,
                          stack_ffn_bufs=True)
    return (y_prompt, y_sample, ps[0], ss[0], ps[1], ss[1], ps[2], ss[2],
            jnp.transpose(ps[3], (1, 0, 2)), jnp.transpose(ss[3], (1, 0, 2)), ps[5], ss[5])
```

```python
import functools
import math

import jax
import jax.numpy as jnp
from jax import lax
from jax.experimental import pallas as pl
from jax.experimental.pallas import tpu as pltpu

f32 = jnp.float32
bf16 = jnp.bfloat16

D_MODEL = 1024
N_META = 16
PAST_LEN = 16384
N_HEADS = 4
D_HEAD = 128
SEG = N_HEADS * D_HEAD
ROPE_BASE = 10000.0
D_RNN = 1024
N_LRU_BLOCKS = 8
LRU_BLOCK = D_RNN // N_LRU_BLOCKS
CONV_W_LRU = 4
LRU_C = 8.0
D_FF = 2816
CONV_W_FFN = 3
LN_EPS = 1e-5
DEPTH = 2
DEEPNORM_ALPHA = (2.0 * DEPTH) ** 0.25
LOG_GAMMA = tuple(math.log1p(-(2.0 ** (-5.0 - h))) for h in range(N_HEADS))

SUBLANES = 8
CHUNK_ROWS = 128
LRU_STEP_COLS = 256
LRU_OUT_ROW_GROUPS = 2
PROMPT_BLOCK_STEPS = 1024
PROMPT_MIXER0_BLOCK_STEPS = 1024
LRU_TM_BLOCK_STEPS = 128
LRU_TM_PARTS = 4
LOG2_SUBLANES = SUBLANES.bit_length() - 1
VMEM_LIMIT_BYTES = 56 * 1024 * 1024
BIG_BLOCK_VMEM_LIMIT_BYTES = 62 * 1024 * 1024

_GELU_K1 = -2.0 * math.log2(math.e) * math.sqrt(2.0 / math.pi)
_GELU_K3 = _GELU_K1 * 0.044715


def _gelu(x):
    return x / (1.0 + jnp.exp2(x * (_GELU_K1 + _GELU_K3 * (x * x))))


def _dot(a, b):
    return jnp.dot(a, b, preferred_element_type=f32)


def _dot_nt(a, b, t_ref=None):
    if t_ref is None:
        return lax.dot_general(a, b, (((1,), (1,)), ((), ())), preferred_element_type=f32)
    t_ref[...] = b.T
    return jnp.dot(a, t_ref[...], preferred_element_type=f32)


def _layernorm_rows(y, g, b):
    mu = jnp.mean(y, axis=-1, keepdims=True)
    yc = y - mu
    var = jnp.mean(yc * yc, axis=-1, keepdims=True)
    return yc * lax.rsqrt(var + LN_EPS) * g + b


def _causal_conv(x3, carry, w_ref, b_ref, cs):
    BB, TL, W = x3.shape
    K = len(carry)
    w = [w_ref[j:j + 1, cs] for j in range(K + 1)]

    def shift(p, j):
        bnd = w[0] * carry[K - j]
        for i in range(1, j):
            bnd = bnd + w[i] * carry[K - j + i]
        if TL == SUBLANES:
            t = lax.broadcasted_iota(jnp.int32, p.shape, 1)
            return jnp.where(t == 0, bnd, pltpu.roll(p, 1, 1))
        assert BB == 1
        rolled = pltpu.roll(p.reshape(TL, W), 1, 0)
        t = lax.broadcasted_iota(jnp.int32, (SUBLANES, W), 0)
        head = jnp.where(t == 0, bnd.reshape(1, W), rolled[:SUBLANES])
        return jnp.concatenate([head, rolled[SUBLANES:]], axis=0).reshape(BB, TL, W)

    p = w[0] * x3
    for j in range(1, K + 1):
        p = w[j] * x3 + shift(p, j)
    return b_ref[:, cs] + p


def _const_spec(shape):
    nd = len(shape)
    return pl.BlockSpec(shape, lambda b, t: (0,) * nd, pipeline_mode=pl.Buffered(1))


def _state_spec(block, per_seq):
    nd = len(block)
    if per_seq:
        return pl.BlockSpec(block, lambda b, t: (b,) + (0,) * (nd - 1))
    return pl.BlockSpec(block, lambda b, t: (0,) * nd)


def _params(vmem_limit_bytes=VMEM_LIMIT_BYTES):
    return pltpu.CompilerParams(dimension_semantics=("arbitrary", "arbitrary"),
                                vmem_limit_bytes=vmem_limit_bytes)


def _mixer0_init_tables(dmask_ref, wq_ref, wk_ref, tri_ref, lvl_ref, *, cb, tl):
    T = cb * tl
    tl_shift = tl.bit_length() - 1
    r = lax.broadcasted_iota(jnp.int32, (T, T), 0)
    c = lax.broadcasted_iota(jnp.int32, (T, T), 1)
    same = (r >> tl_shift) == (c >> tl_shift)
    rel = (r & (tl - 1)) - (c & (tl - 1))
    causal = same & (rel >= 0)
    relf = jnp.maximum(rel, 0).astype(f32)
    tri_ref[...] = jnp.where(causal, 1.0, 0.0).astype(bf16)
    lvl = jnp.where(r == c, 0, -1)
    s, li = 1, 1
    while s < tl:
        blk = (r >> li) == (c >> li)
        hit = blk & ((r & (2 * s - 1)) >= s) & ((c & (2 * s - 1)) < s)
        lvl = jnp.where(hit, li, lvl)
        s, li = 2 * s, li + 1
    lvl_ref[...] = lvl
    tr = (lax.broadcasted_iota(jnp.int32, (T, D_HEAD), 0) & (tl - 1)).astype(f32)
    for h in range(N_HEADS):
        lg = LOG_GAMMA[h]
        dmask_ref[h] = jnp.where(causal, jnp.exp(lg * relf), 0.0)
        wq_ref[h] = jnp.exp(lg * (tr + 1.0))
        wk_ref[h] = jnp.exp(lg * ((tl - 1.0) - tr))


def _hgrn_scores(gqs, kks, fs, bs, lvl_ref, dot_nt, *, T, tl):
    n = len(gqs)
    row = lax.broadcasted_iota(jnp.int32, (T, D_HEAD), 0)
    ntile = T // SUBLANES

    def owned(li, g):
        return lvl_ref[g * SUBLANES:(g + 1) * SUBLANES, :] == li

    full = [dot_nt(gqs[u].astype(bf16), kks[u].astype(bf16)) for u in range(n)]
    tiles = [[jnp.where(owned(0, g), full[u][g * SUBLANES:(g + 1) * SUBLANES], 0.0) for g in range(ntile)]
             for u in range(n)]

    def take(u, li, p, first_tile, n_tiles, p_row0):
        for g in range(n_tiles):
            t = first_tile + g
            tiles[u][t] = jnp.where(owned(li, t), p[p_row0 + g * SUBLANES:p_row0 + (g + 1) * SUBLANES], tiles[u][t])

    s, li = 1, 1
    while s < tl:
        nblk = T // (2 * s)
        if s < SUBLANES:
            up = (row & (2 * s - 1)) >= s
            zs = []
            for u in range(n):
                gq, kk, f, b = gqs[u], kks[u], fs[u], bs[u]
                if s == 1:
                    z = jnp.where(up, gq * f, kk)
                elif s == 2:
                    w = row & 3
                    e = jnp.where(w == 0, pltpu.roll(f, T - 1, 0),
                                  jnp.where(w == 1, 1.0, jnp.where(w == 2, f, f * pltpu.roll(f, 1, 0))))
                    z = jnp.where(up, gq, kk) * e
                else:
                    parts = [jnp.abs(b[m * 2 * s:(m + 1) * 2 * s, :] - b[m * 2 * s + s - 1:m * 2 * s + s, :])
                             for m in range(nblk)]
                    z = jnp.where(up, gq, kk) * jnp.exp2(-jnp.concatenate(parts, axis=0))
                zs.append(z.astype(bf16))
            ps = [dot_nt(zb, zb) for zb in zs]
            for u in range(n):
                take(u, li, ps[u], 0, ntile, 0)
        else:
            zs, qus = [], []
            for u in range(n):
                gq, kk, b = gqs[u], kks[u], bs[u]
                both, upper = [], []
                for m in range(nblk):
                    r0 = m * 2 * s
                    beta = b[r0 + s - 1:r0 + s, :]
                    k_lo = kk[r0:r0 + s] * jnp.exp2(beta - b[r0:r0 + s])
                    q_hi = gq[r0 + s:r0 + 2 * s] * jnp.exp2(b[r0 + s:r0 + 2 * s] - beta)
                    both += [k_lo, q_hi]
                    upper.append(q_hi)
                zs.append(jnp.concatenate(both, axis=0).astype(bf16))
                qus.append((upper[0] if nblk == 1 else jnp.concatenate(upper, axis=0)).astype(bf16))
            ps = [dot_nt(qus[u], zs[u]) for u in range(n)]
            for u in range(n):
                for m in range(nblk):
                    take(u, li, ps[u], (m * 2 * s + s) // SUBLANES, s // SUBLANES, m * s)
        s, li = 2 * s, li + 1
    return [jnp.concatenate(tiles[u], axis=0).astype(bf16) for u in range(n)]


def _mixer0_kernel(x_ref, c2_ref, s2_ref, win_ref, wout_ref, lbl_ref, nw_ref, g_ref, be_ref,
                   sret0_ref, shg0_ref, y_ref, sret_ref, shg_ref,
                   dmask_ref, wq_ref, wk_ref, tri_ref, lvl_ref, kt_ref, *, BB, TL, cb, tl):
    T = cb * tl
    R = BB * TL
    nch = R // T
    slots = iter(range(kt_ref.shape[0]))

    def dot_nt(a, b):
        return _dot_nt(a, b, kt_ref.at[next(slots)] if (cb == 1 and T == CHUNK_ROWS) else None)

    @pl.when((pl.program_id(0) == 0) & (pl.program_id(1) == 0))
    def _init_tables():
        _mixer0_init_tables(dmask_ref, wq_ref, wk_ref, tri_ref, lvl_ref, cb=cb, tl=tl)

    @pl.when(pl.program_id(1) == 0)
    def _init_state():
        sret_ref[...] = sret0_ref[...]
        shg_ref[...] = shg0_ref[...]

    x2 = x_ref[...].reshape(R, D_MODEL)
    xb = x2.astype(bf16)

    def proj(i):
        return _dot(xb, win_ref[:, i * SEG:(i + 1) * SEG])

    lbl = lbl_ref[...]
    le = jnp.exp(lbl - jnp.max(lbl, axis=0, keepdims=True))
    lb = le[0:1] / jnp.sum(le, axis=0, keepdims=True)
    nw = nw_ref[...]

    def seqs(z):
        return z.reshape(cb, tl, D_HEAD)

    def rows(z3):
        return z3.reshape(T, D_HEAD)

    def qs(q3, s3):
        if cb == 1:
            return _dot(q3[0], s3[0])[None]
        return jnp.einsum('bqd,bdv->bqv', q3, s3, preferred_element_type=f32)

    def ktv(k3, v3):
        if cb == 1:
            return lax.dot_general(k3[0], v3[0], (((0,), (0,)), ((), ())),
                                   preferred_element_type=f32)[None]
        return jnp.einsum('btd,btv->bdv', k3, v3, preferred_element_type=f32)

    units = [(c, h) for c in range(nch) for h in range(N_HEADS)]
    nu = len(units)

    def sl(z, c, h):
        return z[c * T:(c + 1) * T, h * D_HEAD:(h + 1) * D_HEAD]

    def seq0(c):
        return 0 if BB == 1 else c * cb

    def rope_tables(c):
        ts = slice(c * tl, (c + 1) * tl) if BB == 1 else slice(0, tl)
        return c2_ref[ts, :][None], s2_ref[ts, :][None]

    def rope(z, cos2, sin2):
        return rows(seqs(z) * cos2 + seqs(pltpu.roll(z, D_HEAD // 2, 1)) * sin2)

    tabs = [rope_tables(c) for c in range(nch)]
    ktabs = [(c2 * (D_HEAD ** -0.5), s2 * (D_HEAD ** -0.5)) for c2, s2 in tabs]

    rq, rk = proj(0), proj(1)
    q = [rope(sl(rq, c, h), *tabs[c]) for c, h in units]
    k = [rope(sl(rk, c, h), *ktabs[c]) for c, h in units]
    rv = proj(2)
    v = [sl(rv, c, h).astype(bf16) for c, h in units]
    att = [dot_nt(q[u].astype(bf16), k[u].astype(bf16)) for u in range(nu)]
    att = [(att[u] * dmask_ref[h]).astype(bf16) for u, (c, h) in enumerate(units)]
    o_ret = [_dot(att[u], v[u]) for u in range(nu)]
    qw = [seqs((q[u] * wq_ref[h]).astype(bf16)) for u, (c, h) in enumerate(units)]
    upd_ret = [ktv(seqs((k[u] * wk_ref[h]).astype(bf16)), seqs(v[u])) for u, (c, h) in enumerate(units)]

    hf = proj(5)
    lbs = [lb[:, h * D_HEAD:(h + 1) * D_HEAD] for h in range(N_HEADS)]
    f = [lbs[h] + (1.0 - lbs[h]) * jax.nn.sigmoid(sl(hf, c, h)) for c, h in units]
    lf = [jnp.log2(z) for z in f]
    kk = [1.0 - z for z in f]
    hq, hi = proj(4), proj(6)
    gq = [sl(hq, c, h) for c, h in units]
    gv = [sl(hi, c, h).astype(bf16) for c, h in units]
    split = []
    for z in lf:
        l1 = z.astype(bf16)
        r1 = z - l1.astype(f32)
        l2 = r1.astype(bf16)
        l3 = (r1 - l2.astype(f32)).astype(bf16)
        split.append(jnp.concatenate([l1, l2, l3], axis=1))
    cs = [_dot(tri_ref[...], z) for z in split]
    b = [z[:, :D_HEAD] + z[:, D_HEAD:2 * D_HEAD] + z[:, 2 * D_HEAD:] for z in cs]
    scores = _hgrn_scores(gq, kk, f, b, lvl_ref, dot_nt, T=T, tl=tl)
    o_hg = [_dot(scores[u], gv[u]) for u in range(nu)]
    qe = [seqs((gq[u] * jnp.exp2(b[u])).astype(bf16)) for u in range(nu)]
    b3 = [seqs(z) for z in b]
    bl = [z[:, tl - 1:tl, :] for z in b3]
    upd_hg = [ktv((seqs(kk[u]) * jnp.exp2(bl[u] - b3[u])).astype(bf16), seqs(gv[u])) for u in range(nu)]
    decay = []
    for z in bl:
        ez = jnp.exp2(z)
        d = [jnp.broadcast_to(ez[j], (D_HEAD, D_HEAD)).T for j in range(cb)]
        decay.append(d[0][None] if cb == 1 else jnp.stack(d))

    s_ret, s_hg = {}, {}
    for u, (c, h) in enumerate(units):
        s0 = seq0(c)
        first = BB > 1 or c == 0
        sp = sret_ref[s0:s0 + cb, h] if first else s_ret[h]
        sg = shg_ref[s0:s0 + cb, h] if first else s_hg[h]
        o_ret[u] = o_ret[u] + rows(qs(qw[u], sp.astype(bf16)))
        o_hg[u] = o_hg[u] + rows(qs(qe[u], sg.astype(bf16)))
        s_ret[h] = math.exp(LOG_GAMMA[h] * tl) * sp + upd_ret[u]
        s_hg[h] = decay[u] * sg + upd_hg[u]
        if BB > 1 or c == nch - 1:
            sret_ref[s0:s0 + cb, h] = s_ret[h]
            shg_ref[s0:s0 + cb, h] = s_hg[h]

    rg, hg = proj(3), proj(7)
    head_out = {}
    for u, (c, h) in enumerate(units):
        o = o_ret[u]
        mu = jnp.mean(o, axis=-1, keepdims=True)
        oc = o - mu
        var = jnp.mean(oc * oc, axis=-1, keepdims=True)
        head_out[(c, h)] = oc * lax.rsqrt(var + LN_EPS) * jax.nn.silu(sl(rg, c, h))
        og = o_hg[u]
        ms = jnp.mean(og * og, axis=-1, keepdims=True)
        head_out[(c, N_HEADS + h)] = og * lax.rsqrt(ms + LN_EPS) * nw * jax.nn.silu(sl(hg, c, h))
    chunk_out = [jnp.concatenate([head_out[(c, j)] for j in range(2 * N_HEADS)], axis=1).astype(bf16)
                 for c in range(nch)]
    mixed = chunk_out[0] if nch == 1 else jnp.concatenate(chunk_out, axis=0)
    m = _dot(mixed, wout_ref[...])
    y = _layernorm_rows(DEEPNORM_ALPHA * x2 + m, g_ref[0:1, :], be_ref[0:1, :])
    y_ref[...] = y.reshape(BB, TL, D_MODEL)


def _mixer0(x, c2, s2, w_in, w_out, lb_logits, norm_w, ln_g, ln_b, sret0, shg0, *, BB, TL):
    B, L, _ = x.shape
    R = BB * TL
    T = min(R, CHUNK_ROWS)
    if BB == 1:
        cb, tl = 1, T
    else:
        assert T % TL == 0
        cb, tl = T // TL, TL
    per_seq = sret0.shape[0] == B and B > 1
    st_block = (BB, N_HEADS, D_HEAD, D_HEAD)
    n_score_dots = N_HEADS * (R // T) * (2 + (tl - 1).bit_length()) if cb == 1 else 1
    kern = functools.partial(_mixer0_kernel, BB=BB, TL=TL, cb=cb, tl=tl)
    return pl.pallas_call(
        kern, grid=(B // BB, L // TL),
        in_specs=[
            pl.BlockSpec((BB, TL, D_MODEL), lambda b, t: (b, t, 0)),
            pl.BlockSpec((TL, D_HEAD), lambda b, t: (t, 0)),
            pl.BlockSpec((TL, D_HEAD), lambda b, t: (t, 0)),
            _const_spec(w_in.shape), _const_spec(w_out.shape), _const_spec(lb_logits.shape),
            _const_spec(norm_w.shape), _const_spec(ln_g.shape), _const_spec(ln_b.shape),
            _state_spec(st_block, per_seq), _state_spec(st_block, per_seq),
        ],
        out_specs=[
            pl.BlockSpec((BB, TL, D_MODEL), lambda b, t: (b, t, 0)),
            _state_spec(st_block, True), _state_spec(st_block, True),
        ],
        out_shape=[
            jax.ShapeDtypeStruct((B, L, D_MODEL), f32),
            jax.ShapeDtypeStruct((B, N_HEADS, D_HEAD, D_HEAD), f32),
            jax.ShapeDtypeStruct((B, N_HEADS, D_HEAD, D_HEAD), f32),
        ],
        scratch_shapes=[
            pltpu.VMEM((N_HEADS, T, T), f32),
            pltpu.VMEM((N_HEADS, T, D_HEAD), f32),
            pltpu.VMEM((N_HEADS, T, D_HEAD), f32),
            pltpu.VMEM((T, T), bf16),
            pltpu.VMEM((T, T), jnp.int32),
            pltpu.VMEM((n_score_dots, D_HEAD, T), bf16),
        ],
        compiler_params=_params(BIG_BLOCK_VMEM_LIMIT_BYTES), name=f"mixer0_b{BB}_t{TL}",
    )(x, c2, s2, w_in, w_out, lb_logits, norm_w, ln_g, ln_b, sret0, shg0)


def _ffn_kernel(x_ref, wup_ref, cw_ref, cb_ref, wdn_ref, g_ref, be_ref, buf0_ref, *rest, BB, TL, stacked, layer):
    R = BB * TL
    K = CONV_W_FFN - 1
    lr = slice(layer, layer + 1)
    if stacked:
        prev_ref, y_ref, out_ref, u_ref = rest
        buf_ref = out_ref.at[1]
    else:
        y_ref, buf_ref, u_ref = rest

    @pl.when(pl.program_id(1) == 0)
    def _init_state():
        buf_ref[...] = buf0_ref[...]
        if stacked:
            out_ref[0] = prev_ref[...]

    x2 = x_ref[...].reshape(R, D_MODEL)
    xb = x2.astype(bf16)
    u = _dot(xb, wup_ref[:, :D_FF])
    v = _dot(xb, wup_ref[:, D_FF:])
    u_ref[:, SUBLANES - K:SUBLANES, :] = buf_ref[...]
    u_ref[:, SUBLANES:, :] = u.reshape(BB, TL, D_FF)
    uc = cb_ref[lr, :] + u_ref[:, SUBLANES - K:SUBLANES - K + TL, :] * cw_ref[0, lr, :]
    for j in range(1, CONV_W_FFN):
        uc = uc + u_ref[:, SUBLANES - K + j:SUBLANES - K + j + TL, :] * cw_ref[j, lr, :]
    buf_ref[...] = u_ref[:, SUBLANES + TL - K:SUBLANES + TL, :]
    hmid = _gelu(uc.reshape(R, D_FF)) * v
    fo = _dot(hmid.astype(bf16), wdn_ref[...])
    y = _layernorm_rows(DEEPNORM_ALPHA * x2 + fo, g_ref[lr, :], be_ref[lr, :])
    y_ref[...] = y.reshape(BB, TL, D_MODEL)


def _ffn(x, w_up, conv_w, conv_b, w_down, ln_g, ln_b, buf0, *, layer, BB, TL, prev_buf=None):
    B, L, _ = x.shape
    per_seq = buf0.shape[-3] == B and B > 1
    K = CONV_W_FFN - 1
    stacked = prev_buf is not None
    kern = functools.partial(_ffn_kernel, BB=BB, TL=TL, stacked=stacked, layer=layer)
    buf_spec = (pl.BlockSpec((2, BB, K, D_FF), lambda b, t: (0, b, 0, 0)) if stacked
                else _state_spec((BB, K, D_FF), True))
    buf_shape = (2, B, K, D_FF) if stacked else (B, K, D_FF)
    buf0_spec = (_state_spec((BB, K, D_FF), per_seq) if buf0.ndim == 3
                 else pl.BlockSpec((None, BB, K, D_FF), lambda b, t: (layer, b, 0, 0)))
    layer_spec = lambda rows, cols: pl.BlockSpec((rows, cols), lambda b, t: (layer, 0), pipeline_mode=pl.Buffered(1))
    return pl.pallas_call(
        kern, grid=(B // BB, L // TL),
        in_specs=[
            pl.BlockSpec((BB, TL, D_MODEL), lambda b, t: (b, t, 0)),
            layer_spec(D_MODEL, 2 * D_FF), _const_spec(conv_w.shape), _const_spec(conv_b.shape),
            layer_spec(D_FF, D_MODEL), _const_spec(ln_g.shape), _const_spec(ln_b.shape),
            buf0_spec,
        ] + ([_state_spec((BB, K, D_FF), True)] if stacked else []),
        out_specs=[pl.BlockSpec((BB, TL, D_MODEL), lambda b, t: (b, t, 0)), buf_spec],
        out_shape=[
            jax.ShapeDtypeStruct((B, L, D_MODEL), f32),
            jax.ShapeDtypeStruct(buf_shape, f32),
        ],
        scratch_shapes=[pltpu.VMEM((BB, SUBLANES + TL, D_FF), f32)],
        compiler_params=_params(BIG_BLOCK_VMEM_LIMIT_BYTES), name=f"ffn_b{BB}_t{TL}",
    )(x, w_up, conv_w, conv_b, w_down, ln_g, ln_b, buf0, *([prev_buf] if stacked else []))


def _rglru_kernel(x_ref, win_ref, cw_ref, cb_ref, wga_ref, bga_ref, wgx_ref, bgx_ref, lam_ref,
                  wout_ref, g_ref, be_ref, h0_ref, buf0_ref,
                  y_ref, h_ref, buf_ref, *, BB, TL):
    R = BB * TL
    K = CONV_W_LRU - 1
    G = TL // SUBLANES

    @pl.when(pl.program_id(1) == 0)
    def _init_state():
        h_ref[...] = h0_ref[...]
        buf_ref[...] = buf0_ref[...]

    x2 = x_ref[...].reshape(R, D_MODEL)
    xb = x2.astype(bf16)
    W = LRU_STEP_COLS
    nchunk = D_RNN // W
    carry = buf_ref[...]
    new_carry = []
    sub = lax.broadcasted_iota(jnp.int32, (R // SUBLANES, SUBLANES, W), 1)
    h0 = h_ref[...]

    def project(ci):
        c0 = ci * W
        return _dot(xb, win_ref[:, c0:c0 + W]), _dot(xb, win_ref[:, D_RNN + c0:D_RNN + c0 + W])

    def recur(ci, gate, rnn):
        c0 = ci * W
        cs = slice(c0, c0 + W)
        rnn3 = rnn.reshape(BB, TL, W)
        xc = _causal_conv(rnn3, [carry[k][:, None, cs] for k in range(K)], cw_ref, cb_ref, cs).reshape(R, W)
        new_carry.append(jnp.stack([rnn3[:, TL - K + k, :] for k in range(K)], axis=0))
        xcb = xc.astype(bf16)
        blocks = [(j, c0 // LRU_BLOCK + j) for j in range(W // LRU_BLOCK)]
        ga = jnp.concatenate([_dot(xcb[:, j * LRU_BLOCK:(j + 1) * LRU_BLOCK], wga_ref[n]) for j, n in blocks], axis=1)
        gx = jnp.concatenate([_dot(xcb[:, j * LRU_BLOCK:(j + 1) * LRU_BLOCK], wgx_ref[n]) for j, n in blocks], axis=1)
        rgate = jax.nn.sigmoid(ga + bga_ref[:, cs])
        igate = jax.nn.sigmoid(gx + bgx_ref[:, cs])
        nla = rgate * (-LRU_C * jax.nn.log_sigmoid(lam_ref[:, cs]))
        a = jnp.exp(-nla)
        w1 = jnp.tanh(nla) * (1.0 + a * a)
        bv = jnp.where(w1 > 0.0, w1 * lax.rsqrt(w1), 0.0) * (igate * xc)
        a4 = a.reshape(R // SUBLANES, SUBLANES, W)
        b4 = bv.reshape(R // SUBLANES, SUBLANES, W)
        for d in (1, 2, 4):
            keep = sub >= d
            b4 = jnp.where(keep, a4 * pltpu.roll(b4, d, 1) + b4, b4)
            a4 = jnp.where(keep, a4 * pltpu.roll(a4, d, 1), a4)
        a5 = a4.reshape(BB, G, SUBLANES, W)
        b5 = b4.reshape(BB, G, SUBLANES, W)
        hc = h0[:, None, cs]
        tiles = []
        for gi in range(G):
            hg = a5[:, gi] * hc + b5[:, gi]
            hc = hg[:, SUBLANES - 1:SUBLANES, :]
            tiles.append(hg)
        hseq = tiles[0] if G == 1 else jnp.concatenate(tiles, axis=1)
        return (_gelu(gate) * hseq.reshape(R, W)).astype(bf16), hc

    nxt = project(0)
    ys, h_last = [], []
    for ci in range(nchunk):
        cur, nxt = nxt, (project(ci + 1) if ci + 1 < nchunk else None)
        yv, hc = recur(ci, *cur)
        ys.append(yv)
        h_last.append(hc)
    buf_ref[...] = jnp.concatenate(new_carry, axis=2)
    h_ref[...] = jnp.concatenate(h_last, axis=2).reshape(BB, D_RNN)
    yv = jnp.concatenate(ys, axis=1)
    groups = LRU_OUT_ROW_GROUPS if R % (LRU_OUT_ROW_GROUPS * 2 * SUBLANES) == 0 else 1
    step = R // groups
    outs = [_layernorm_rows(DEEPNORM_ALPHA * x2[i * step:(i + 1) * step]
                            + _dot(yv[i * step:(i + 1) * step], wout_ref[...]), g_ref[1:2, :], be_ref[1:2, :])
            for i in range(groups)]
    y = outs[0] if groups == 1 else jnp.concatenate(outs, axis=0)
    y_ref[...] = y.reshape(BB, TL, D_MODEL)


def _rglru(x, w_in, conv_w, conv_b, wga, bga, wgx, bgx, lam, w_out, ln_g, ln_b, h0, buf0, *, BB, TL):
    B, L, _ = x.shape
    per_seq = h0.shape[0] == B and B > 1
    K = CONV_W_LRU - 1
    kern = functools.partial(_rglru_kernel, BB=BB, TL=TL)
    consts = (w_in, conv_w, conv_b, wga, bga, wgx, bgx, lam, w_out, ln_g, ln_b)

    def buf_spec(per):
        return pl.BlockSpec((K, BB, D_RNN), lambda b, t: (0, b if per else 0, 0))

    return pl.pallas_call(
        kern, grid=(B // BB, L // TL),
        in_specs=[pl.BlockSpec((BB, TL, D_MODEL), lambda b, t: (b, t, 0))]
        + [_const_spec(c.shape) for c in consts]
        + [_state_spec((BB, D_RNN), per_seq), buf_spec(per_seq)],
        out_specs=[
            pl.BlockSpec((BB, TL, D_MODEL), lambda b, t: (b, t, 0)),
            _state_spec((BB, D_RNN), True), buf_spec(True),
        ],
        out_shape=[
            jax.ShapeDtypeStruct((B, L, D_MODEL), f32),
            jax.ShapeDtypeStruct((B, D_RNN), f32),
            jax.ShapeDtypeStruct((K, B, D_RNN), f32),
        ],
        compiler_params=_params(), name=f"rglru_b{BB}_t{TL}",
    )(x, *consts, h0, buf0)


def _rglru_tm_kernel(x_ref, win_ref, cw_ref, cb_ref, wga_ref, bga_ref, wgx_ref, bgx_ref, lam_ref,
                     wout_ref, g_ref, be_ref, h0_ref, buf0_ref,
                     y_ref, h_ref, buf_ref, perm_ref, permt_ref, hrun_ref, crun_ref, *, TL):
    B = SUBLANES
    NPART = LRU_TM_PARTS
    TP = TL // NPART
    RP = B * TP
    K = CONV_W_LRU - 1
    W = LRU_STEP_COLS
    nchunk = D_RNN // W
    step = pl.program_id(0)

    @pl.when(step == 0)
    def _init():
        r = lax.broadcasted_iota(jnp.int32, (RP, RP), 0)
        c = lax.broadcasted_iota(jnp.int32, (RP, RP), 1)
        hit = ((r & (B - 1)) * TP + (r >> LOG2_SUBLANES)) == c
        perm_ref[...] = jnp.where(hit, 1.0, 0.0).astype(bf16)
        hit_t = ((c & (B - 1)) * TP + (c >> LOG2_SUBLANES)) == r
        permt_ref[...] = jnp.where(hit_t, 1.0, 0.0).astype(bf16)
        hrun_ref[...] = jnp.broadcast_to(h0_ref[...], (B, D_RNN))
        for k in range(K):
            crun_ref[k] = jnp.broadcast_to(buf0_ref[k], (B, D_RNN))

    def head(p):
        x2 = x_ref[:, p * TP:(p + 1) * TP, :].reshape(RP, D_MODEL)
        return x2, _dot(perm_ref[...], x2.astype(bf16)).astype(bf16)

    def project(xt, ci):
        c0 = ci * W
        return _dot(xt, win_ref[:, c0:c0 + W]), _dot(xt, win_ref[:, D_RNN + c0:D_RNN + c0 + W])

    def body(ci, gate, rnn):
        c0 = ci * W
        cs = slice(c0, c0 + W)
        rnn = rnn.reshape(TP, B, W)
        ext = jnp.concatenate([crun_ref[:, :, cs], rnn], axis=0)
        xc = cb_ref[:, cs] + ext[0:TP] * cw_ref[0:1, cs]
        for j in range(1, CONV_W_LRU):
            xc = xc + ext[j:j + TP] * cw_ref[j:j + 1, cs]
        crun_ref[:, :, cs] = rnn[TP - K:]
        xc = xc.reshape(RP, W)
        xcb = xc.astype(bf16)
        blocks = [(j, c0 // LRU_BLOCK + j) for j in range(W // LRU_BLOCK)]
        ga = jnp.concatenate([_dot(xcb[:, j * LRU_BLOCK:(j + 1) * LRU_BLOCK], wga_ref[n]) for j, n in blocks], axis=1)
        gx = jnp.concatenate([_dot(xcb[:, j * LRU_BLOCK:(j + 1) * LRU_BLOCK], wgx_ref[n]) for j, n in blocks], axis=1)
        rgate = jax.nn.sigmoid(ga + bga_ref[:, cs])
        igate = jax.nn.sigmoid(gx + bgx_ref[:, cs])
        nla = rgate * (-LRU_C * jax.nn.log_sigmoid(lam_ref[:, cs]))
        a2 = jnp.exp(-nla)
        a = a2.reshape(TP, B, W)
        w1 = jnp.tanh(nla) * (1.0 + a2 * a2)
        bv = (jnp.where(w1 > 0.0, w1 * lax.rsqrt(w1), 0.0) * (igate * xc)).reshape(TP, B, W)
        h = hrun_ref[:, cs]
        hs = []
        for t in range(TP):
            h = a[t] * h + bv[t]
            hs.append(h)
        hrun_ref[:, cs] = h
        return (_gelu(gate) * jnp.stack(hs, axis=0).reshape(RP, W)).astype(bf16)

    def tail(p, x2, ys):
        yv = _dot(permt_ref[...], jnp.concatenate(ys, axis=1)).astype(bf16)
        y = _layernorm_rows(DEEPNORM_ALPHA * x2 + _dot(yv, wout_ref[...]), g_ref[1:2, :], be_ref[1:2, :])
        y_ref[:, p * TP:(p + 1) * TP, :] = y.reshape(B, TP, D_MODEL)

    x2s, xts, nxt = {}, {}, None
    x2s[0], xts[0] = head(0)
    nxt = project(xts[0], 0)
    for p in range(NPART):
        if p + 1 < NPART:
            x2s[p + 1], xts[p + 1] = head(p + 1)
        ys = []
        for ci in range(nchunk):
            cur = nxt
            if ci + 1 < nchunk:
                nxt = project(xts[p], ci + 1)
            elif p + 1 < NPART:
                nxt = project(xts[p + 1], 0)
            ys.append(body(ci, *cur))
        tail(p, x2s[p], ys)

    @pl.when(step == pl.num_programs(0) - 1)
    def _emit_state():
        h_ref[...] = hrun_ref[...]
        buf_ref[...] = crun_ref[...]


def _rglru_tm(x, w_in, conv_w, conv_b, wga, bga, wgx, bgx, lam, w_out, ln_g, ln_b, h0, buf0, *, TL):
    B, L, _ = x.shape
    assert B == SUBLANES and h0.shape[0] == 1 and buf0.shape[1] == 1
    K = CONV_W_LRU - 1
    R = B * TL // LRU_TM_PARTS
    consts = (w_in, conv_w, conv_b, wga, bga, wgx, bgx, lam, w_out, ln_g, ln_b, h0, buf0)
    const_spec = lambda shape: pl.BlockSpec(shape, lambda t: (0,) * len(shape), pipeline_mode=pl.Buffered(1))
    return pl.pallas_call(
        functools.partial(_rglru_tm_kernel, TL=TL), grid=(L // TL,),
        in_specs=[pl.BlockSpec((B, TL, D_MODEL), lambda t: (0, t, 0))] + [const_spec(c.shape) for c in consts],
        out_specs=[
            pl.BlockSpec((B, TL, D_MODEL), lambda t: (0, t, 0)),
            pl.BlockSpec((B, D_RNN), lambda t: (0, 0)), pl.BlockSpec((K, B, D_RNN), lambda t: (0, 0, 0)),
        ],
        out_shape=[
            jax.ShapeDtypeStruct((B, L, D_MODEL), f32),
            jax.ShapeDtypeStruct((B, D_RNN), f32),
            jax.ShapeDtypeStruct((K, B, D_RNN), f32),
        ],
        scratch_shapes=[
            pltpu.VMEM((R, R), bf16), pltpu.VMEM((R, R), bf16),
            pltpu.VMEM((B, D_RNN), f32),
            pltpu.VMEM((K, B, D_RNN), f32),
        ],
        compiler_params=pltpu.CompilerParams(dimension_semantics=("arbitrary",), vmem_limit_bytes=VMEM_LIMIT_BYTES),
        name=f"rglru_tm_t{TL}",
    )(x, *consts)


def _rope_tables(pos0, length):
    half = D_HEAD // 2
    pos = pos0 + jnp.arange(length, dtype=jnp.int32)
    inv = ROPE_BASE ** (-jnp.arange(half, dtype=f32) / half)
    ang = pos.astype(f32)[:, None] * inv[None, :]
    cos, sin = jnp.cos(ang), jnp.sin(ang)
    return jnp.concatenate([cos, cos], axis=1), jnp.concatenate([-sin, sin], axis=1)


def _trunk(x, pos0, states, p, *, BB, TL, BB0, TL0=None, lru_tm_steps=None, stack_ffn_bufs=False):
    s_ret, s_hgrn, h_lru, buf_lru, buf_ffn0, buf_ffn1 = states
    c2, s2 = _rope_tables(pos0, x.shape[1])
    x, s_ret, s_hgrn = _mixer0(x, c2, s2, p['w_in_ab'], p['w_out_ab'], p['hgrn_lb_logits'], p['hgrn_norm_w'],
                               p['ln_mix_g'], p['ln_mix_b'], s_ret, s_hgrn, BB=BB0, TL=TL0 or TL)
    x, buf_ffn0 = _ffn(x, p['w_ffn_up'], p['ffn_conv_w'], p['ffn_conv_b'], p['w_ffn_down'],
                       p['ln_ffn_g'], p['ln_ffn_b'], buf_ffn0, layer=0, BB=BB, TL=TL)
    lru_args = (x, p['w_in_c'], p['conv_w_c'], p['conv_b_c'], p['w_gate_a'], p['b_gate_a'], p['w_gate_x'],
                p['b_gate_x'], p['lru_lambda'], p['w_out_c'], p['ln_mix_g'], p['ln_mix_b'], h_lru, buf_lru)
    if lru_tm_steps is None:
        x, h_lru, buf_lru = _rglru(*lru_args, BB=BB, TL=TL)
    else:
        x, h_lru, buf_lru = _rglru_tm(*lru_args, TL=lru_tm_steps)
    x, buf_ffn1 = _ffn(x, p['w_ffn_up'], p['ffn_conv_w'], p['ffn_conv_b'], p['w_ffn_down'],
                       p['ln_ffn_g'], p['ln_ffn_b'], buf_ffn1, layer=1, BB=BB, TL=TL,
                       prev_buf=buf_ffn0 if stack_ffn_bufs else None)
    return x, (s_ret, s_hgrn, h_lru, buf_lru, buf_ffn0, buf_ffn1)


def kernel(x_prompt, x_sample, state_ret, state_hgrn, state_rglru_h, state_rglru_conv, state_ffn_conv, meta_tokens, w_in_ab, w_out_ab, hgrn_lb_logits, hgrn_norm_w, w_in_c, conv_w_c, conv_b_c, w_gate_a, b_gate_a, w_gate_x, b_gate_x, lru_lambda, w_out_c, ln_mix_g, ln_mix_b, ln_ffn_g, ln_ffn_b, w_ffn_up, ffn_conv_w, ffn_conv_b, w_ffn_down):
    row = lambda z: z.reshape(1, -1)
    wb = [w.astype(bf16) for w in (w_in_ab, w_out_ab, w_in_c, w_gate_a, w_gate_x, w_out_c,
                                   w_ffn_up.reshape(DEPTH * D_MODEL, 2 * D_FF),
                                   w_ffn_down.reshape(DEPTH * D_FF, D_MODEL))]
    p = dict(
        w_in_ab=wb[0], w_out_ab=wb[1],
        hgrn_lb_logits=hgrn_lb_logits, hgrn_norm_w=row(hgrn_norm_w),
        w_in_c=wb[2], conv_w_c=conv_w_c, conv_b_c=row(conv_b_c),
        w_gate_a=wb[3], b_gate_a=row(b_gate_a),
        w_gate_x=wb[4], b_gate_x=row(b_gate_x),
        lru_lambda=row(lru_lambda), w_out_c=wb[5],
        ln_mix_g=ln_mix_g, ln_mix_b=ln_mix_b, ln_ffn_g=ln_ffn_g, ln_ffn_b=ln_ffn_b,
        w_ffn_up=wb[6], w_ffn_down=wb[7],
        ffn_conv_w=jnp.transpose(ffn_conv_w, (1, 0, 2)), ffn_conv_b=ffn_conv_b,
    )
    dt = x_prompt.dtype
    zero_states = (
        jnp.zeros((1, N_HEADS, D_HEAD, D_HEAD), dt), jnp.zeros((1, N_HEADS, D_HEAD, D_HEAD), dt),
        jnp.zeros((1, D_RNN), dt), jnp.zeros((CONV_W_LRU - 1, 1, D_RNN), dt),
        jnp.zeros((1, CONV_W_FFN - 1, D_FF), dt), jnp.zeros((1, CONV_W_FFN - 1, D_FF), dt),
    )
    _, meta_states = _trunk(meta_tokens.astype(dt)[None], 0, zero_states, p, BB=1, TL=N_META, BB0=1)
    y_prompt, ps = _trunk(x_prompt, N_META, meta_states, p, BB=1, TL=PROMPT_BLOCK_STEPS, BB0=1,
                          TL0=PROMPT_MIXER0_BLOCK_STEPS, lru_tm_steps=LRU_TM_BLOCK_STEPS, stack_ffn_bufs=True)
    sample_states = (state_ret, state_hgrn, state_rglru_h, jnp.transpose(state_rglru_conv, (1, 0, 2)),
                     state_ffn_conv, state_ffn_conv)
    y_sample, ss = _trunk(x_sample, PAST_LEN, sample_states, p, BB=32, TL=x_sample.shape[1], BB0=16,
                          stack_ffn_bufs=True)
    return (y_prompt, y_sample, ps[0], ss[0], ps[1], ss[1], ps[2], ss[2],
            jnp.transpose(ps[3], (1, 0, 2)), jnp.transpose(ss[3], (1, 0, 2)), ps[5], ss[5])
```

```python
import functools
import math

import jax
import jax.numpy as jnp
from jax import lax
from jax.experimental import pallas as pl
from jax.experimental.pallas import tpu as pltpu

f32 = jnp.float32
bf16 = jnp.bfloat16

D_MODEL = 1024
N_META = 16
PAST_LEN = 16384
N_HEADS = 4
D_HEAD = 128
SEG = N_HEADS * D_HEAD
ROPE_BASE = 10000.0
D_RNN = 1024
N_LRU_BLOCKS = 8
LRU_BLOCK = D_RNN // N_LRU_BLOCKS
CONV_W_LRU = 4
LRU_C = 8.0
D_FF = 2816
CONV_W_FFN = 3
LN_EPS = 1e-5
DEPTH = 2
DEEPNORM_ALPHA = (2.0 * DEPTH) ** 0.25
LOG_GAMMA = tuple(math.log1p(-(2.0 ** (-5.0 - h))) for h in range(N_HEADS))

SUBLANES = 8
CHUNK_ROWS = 128
LRU_STEP_COLS = 256
LRU_OUT_ROW_GROUPS = 2
PROMPT_BLOCK_STEPS = 1024
PROMPT_MIXER0_BLOCK_STEPS = 1024
LRU_TM_BLOCK_STEPS = 256
LRU_TM_PARTS = 8
LOG2_SUBLANES = SUBLANES.bit_length() - 1
VMEM_LIMIT_BYTES = 56 * 1024 * 1024
BIG_BLOCK_VMEM_LIMIT_BYTES = 62 * 1024 * 1024

_GELU_K1 = -2.0 * math.log2(math.e) * math.sqrt(2.0 / math.pi)
_GELU_K3 = _GELU_K1 * 0.044715


def _gelu(x):
    return x / (1.0 + jnp.exp2(x * (_GELU_K1 + _GELU_K3 * (x * x))))


def _dot(a, b):
    return jnp.dot(a, b, preferred_element_type=f32)


def _dot_nt(a, b, t_ref=None):
    if t_ref is None:
        return lax.dot_general(a, b, (((1,), (1,)), ((), ())), preferred_element_type=f32)
    t_ref[...] = b.T
    return jnp.dot(a, t_ref[...], preferred_element_type=f32)


def _layernorm_rows(y, g, b):
    mu = jnp.mean(y, axis=-1, keepdims=True)
    yc = y - mu
    var = jnp.mean(yc * yc, axis=-1, keepdims=True)
    return yc * lax.rsqrt(var + LN_EPS) * g + b


def _causal_conv(x3, carry, w_ref, b_ref, cs):
    BB, TL, W = x3.shape
    K = len(carry)
    w = [w_ref[j:j + 1, cs] for j in range(K + 1)]

    def shift(p, j):
        bnd = w[0] * carry[K - j]
        for i in range(1, j):
            bnd = bnd + w[i] * carry[K - j + i]
        if TL == SUBLANES:
            t = lax.broadcasted_iota(jnp.int32, p.shape, 1)
            return jnp.where(t == 0, bnd, pltpu.roll(p, 1, 1))
        assert BB == 1
        rolled = pltpu.roll(p.reshape(TL, W), 1, 0)
        t = lax.broadcasted_iota(jnp.int32, (SUBLANES, W), 0)
        head = jnp.where(t == 0, bnd.reshape(1, W), rolled[:SUBLANES])
        return jnp.concatenate([head, rolled[SUBLANES:]], axis=0).reshape(BB, TL, W)

    p = w[0] * x3
    for j in range(1, K + 1):
        p = w[j] * x3 + shift(p, j)
    return b_ref[:, cs] + p


def _const_spec(shape):
    nd = len(shape)
    return pl.BlockSpec(shape, lambda b, t: (0,) * nd, pipeline_mode=pl.Buffered(1))


def _state_spec(block, per_seq):
    nd = len(block)
    if per_seq:
        return pl.BlockSpec(block, lambda b, t: (b,) + (0,) * (nd - 1))
    return pl.BlockSpec(block, lambda b, t: (0,) * nd)


def _params(vmem_limit_bytes=VMEM_LIMIT_BYTES):
    return pltpu.CompilerParams(dimension_semantics=("arbitrary", "arbitrary"),
                                vmem_limit_bytes=vmem_limit_bytes)


def _mixer0_init_tables(dmask_ref, wq_ref, wk_ref, tri_ref, lvl_ref, *, cb, tl):
    T = cb * tl
    tl_shift = tl.bit_length() - 1
    r = lax.broadcasted_iota(jnp.int32, (T, T), 0)
    c = lax.broadcasted_iota(jnp.int32, (T, T), 1)
    same = (r >> tl_shift) == (c >> tl_shift)
    rel = (r & (tl - 1)) - (c & (tl - 1))
    causal = same & (rel >= 0)
    relf = jnp.maximum(rel, 0).astype(f32)
    tri_ref[...] = jnp.where(causal, 1.0, 0.0).astype(bf16)
    lvl = jnp.where(r == c, 0, -1)
    s, li = 1, 1
    while s < tl:
        blk = (r >> li) == (c >> li)
        hit = blk & ((r & (2 * s - 1)) >= s) & ((c & (2 * s - 1)) < s)
        lvl = jnp.where(hit, li, lvl)
        s, li = 2 * s, li + 1
    lvl_ref[...] = lvl
    tr = (lax.broadcasted_iota(jnp.int32, (T, D_HEAD), 0) & (tl - 1)).astype(f32)
    for h in range(N_HEADS):
        lg = LOG_GAMMA[h]
        dmask_ref[h] = jnp.where(causal, jnp.exp(lg * relf), 0.0)
        wq_ref[h] = jnp.exp(lg * (tr + 1.0))
        wk_ref[h] = jnp.exp(lg * ((tl - 1.0) - tr))


def _hgrn_scores(gqs, kks, fs, bs, lvl_ref, dot_nt, *, T, tl):
    n = len(gqs)
    row = lax.broadcasted_iota(jnp.int32, (T, D_HEAD), 0)
    ntile = T // SUBLANES

    def owned(li, g):
        return lvl_ref[g * SUBLANES:(g + 1) * SUBLANES, :] == li

    full = [dot_nt(gqs[u].astype(bf16), kks[u].astype(bf16)) for u in range(n)]
    tiles = [[jnp.where(owned(0, g), full[u][g * SUBLANES:(g + 1) * SUBLANES], 0.0) for g in range(ntile)]
             for u in range(n)]

    def take(u, li, p, first_tile, n_tiles, p_row0):
        for g in range(n_tiles):
            t = first_tile + g
            tiles[u][t] = jnp.where(owned(li, t), p[p_row0 + g * SUBLANES:p_row0 + (g + 1) * SUBLANES], tiles[u][t])

    s, li = 1, 1
    while s < tl:
        nblk = T // (2 * s)
        if s < SUBLANES:
            up = (row & (2 * s - 1)) >= s
            zs = []
            for u in range(n):
                gq, kk, f, b = gqs[u], kks[u], fs[u], bs[u]
                if s == 1:
                    z = jnp.where(up, gq * f, kk)
                elif s == 2:
                    w = row & 3
                    e = jnp.where(w == 0, pltpu.roll(f, T - 1, 0),
                                  jnp.where(w == 1, 1.0, jnp.where(w == 2, f, f * pltpu.roll(f, 1, 0))))
                    z = jnp.where(up, gq, kk) * e
                else:
                    parts = [jnp.abs(b[m * 2 * s:(m + 1) * 2 * s, :] - b[m * 2 * s + s - 1:m * 2 * s + s, :])
                             for m in range(nblk)]
                    z = jnp.where(up, gq, kk) * jnp.exp2(-jnp.concatenate(parts, axis=0))
                zs.append(z.astype(bf16))
            ps = [dot_nt(zb, zb) for zb in zs]
            for u in range(n):
                take(u, li, ps[u], 0, ntile, 0)
        else:
            zs, qus = [], []
            for u in range(n):
                gq, kk, b = gqs[u], kks[u], bs[u]
                both, upper = [], []
                for m in range(nblk):
                    r0 = m * 2 * s
                    beta = b[r0 + s - 1:r0 + s, :]
                    k_lo = kk[r0:r0 + s] * jnp.exp2(beta - b[r0:r0 + s])
                    q_hi = gq[r0 + s:r0 + 2 * s] * jnp.exp2(b[r0 + s:r0 + 2 * s] - beta)
                    both += [k_lo, q_hi]
                    upper.append(q_hi)
                zs.append(jnp.concatenate(both, axis=0).astype(bf16))
                qus.append((upper[0] if nblk == 1 else jnp.concatenate(upper, axis=0)).astype(bf16))
            ps = [dot_nt(qus[u], zs[u]) for u in range(n)]
            for u in range(n):
                for m in range(nblk):
                    take(u, li, ps[u], (m * 2 * s + s) // SUBLANES, s // SUBLANES, m * s)
        s, li = 2 * s, li + 1
    return [jnp.concatenate(tiles[u], axis=0).astype(bf16) for u in range(n)]


def _mixer0_kernel(x_ref, c2_ref, s2_ref, win_ref, wout_ref, lbl_ref, nw_ref, g_ref, be_ref,
                   sret0_ref, shg0_ref, y_ref, sret_ref, shg_ref,
                   dmask_ref, wq_ref, wk_ref, tri_ref, lvl_ref, kt_ref, *, BB, TL, cb, tl):
    T = cb * tl
    R = BB * TL
    nch = R // T
    slots = iter(range(kt_ref.shape[0]))

    def dot_nt(a, b):
        return _dot_nt(a, b, kt_ref.at[next(slots)] if (cb == 1 and T == CHUNK_ROWS) else None)

    @pl.when((pl.program_id(0) == 0) & (pl.program_id(1) == 0))
    def _init_tables():
        _mixer0_init_tables(dmask_ref, wq_ref, wk_ref, tri_ref, lvl_ref, cb=cb, tl=tl)

    @pl.when(pl.program_id(1) == 0)
    def _init_state():
        sret_ref[...] = sret0_ref[...]
        shg_ref[...] = shg0_ref[...]

    x2 = x_ref[...].reshape(R, D_MODEL)
    xb = x2.astype(bf16)

    def proj(i):
        return _dot(xb, win_ref[:, i * SEG:(i + 1) * SEG])

    lbl = lbl_ref[...]
    le = jnp.exp(lbl - jnp.max(lbl, axis=0, keepdims=True))
    lb = le[0:1] / jnp.sum(le, axis=0, keepdims=True)
    nw = nw_ref[...]

    def seqs(z):
        return z.reshape(cb, tl, D_HEAD)

    def rows(z3):
        return z3.reshape(T, D_HEAD)

    def qs(q3, s3):
        if cb == 1:
            return _dot(q3[0], s3[0])[None]
        return jnp.einsum('bqd,bdv->bqv', q3, s3, preferred_element_type=f32)

    def ktv(k3, v3):
        if cb == 1:
            return lax.dot_general(k3[0], v3[0], (((0,), (0,)), ((), ())),
                                   preferred_element_type=f32)[None]
        return jnp.einsum('btd,btv->bdv', k3, v3, preferred_element_type=f32)

    units = [(c, h) for c in range(nch) for h in range(N_HEADS)]
    nu = len(units)

    def sl(z, c, h):
        return z[c * T:(c + 1) * T, h * D_HEAD:(h + 1) * D_HEAD]

    def seq0(c):
        return 0 if BB == 1 else c * cb

    def rope_tables(c):
        ts = slice(c * tl, (c + 1) * tl) if BB == 1 else slice(0, tl)
        return c2_ref[ts, :][None], s2_ref[ts, :][None]

    def rope(z, cos2, sin2):
        return rows(seqs(z) * cos2 + seqs(pltpu.roll(z, D_HEAD // 2, 1)) * sin2)

    tabs = [rope_tables(c) for c in range(nch)]
    ktabs = [(c2 * (D_HEAD ** -0.5), s2 * (D_HEAD ** -0.5)) for c2, s2 in tabs]

    rq, rk = proj(0), proj(1)
    q = [rope(sl(rq, c, h), *tabs[c]) for c, h in units]
    k = [rope(sl(rk, c, h), *ktabs[c]) for c, h in units]
    rv = proj(2)
    v = [sl(rv, c, h).astype(bf16) for c, h in units]
    att = [dot_nt(q[u].astype(bf16), k[u].astype(bf16)) for u in range(nu)]
    att = [(att[u] * dmask_ref[h]).astype(bf16) for u, (c, h) in enumerate(units)]
    o_ret = [_dot(att[u], v[u]) for u in range(nu)]
    qw = [seqs((q[u] * wq_ref[h]).astype(bf16)) for u, (c, h) in enumerate(units)]
    upd_ret = [ktv(seqs((k[u] * wk_ref[h]).astype(bf16)), seqs(v[u])) for u, (c, h) in enumerate(units)]

    hf = proj(5)
    lbs = [lb[:, h * D_HEAD:(h + 1) * D_HEAD] for h in range(N_HEADS)]
    f = [lbs[h] + (1.0 - lbs[h]) * jax.nn.sigmoid(sl(hf, c, h)) for c, h in units]
    lf = [jnp.log2(z) for z in f]
    kk = [1.0 - z for z in f]
    hq, hi = proj(4), proj(6)
    gq = [sl(hq, c, h) for c, h in units]
    gv = [sl(hi, c, h).astype(bf16) for c, h in units]
    split = []
    for z in lf:
        l1 = z.astype(bf16)
        r1 = z - l1.astype(f32)
        l2 = r1.astype(bf16)
        l3 = (r1 - l2.astype(f32)).astype(bf16)
        split.append(jnp.concatenate([l1, l2, l3], axis=1))
    cs = [_dot(tri_ref[...], z) for z in split]
    b = [z[:, :D_HEAD] + z[:, D_HEAD:2 * D_HEAD] + z[:, 2 * D_HEAD:] for z in cs]
    scores = _hgrn_scores(gq, kk, f, b, lvl_ref, dot_nt, T=T, tl=tl)
    o_hg = [_dot(scores[u], gv[u]) for u in range(nu)]
    qe = [seqs((gq[u] * jnp.exp2(b[u])).astype(bf16)) for u in range(nu)]
    b3 = [seqs(z) for z in b]
    bl = [z[:, tl - 1:tl, :] for z in b3]
    upd_hg = [ktv((seqs(kk[u]) * jnp.exp2(bl[u] - b3[u])).astype(bf16), seqs(gv[u])) for u in range(nu)]
    decay = []
    for z in bl:
        ez = jnp.exp2(z)
        d = [jnp.broadcast_to(ez[j], (D_HEAD, D_HEAD)).T for j in range(cb)]
        decay.append(d[0][None] if cb == 1 else jnp.stack(d))

    s_ret, s_hg = {}, {}
    for u, (c, h) in enumerate(units):
        s0 = seq0(c)
        first = BB > 1 or c == 0
        sp = sret_ref[s0:s0 + cb, h] if first else s_ret[h]
        sg = shg_ref[s0:s0 + cb, h] if first else s_hg[h]
        o_ret[u] = o_ret[u] + rows(qs(qw[u], sp.astype(bf16)))
        o_hg[u] = o_hg[u] + rows(qs(qe[u], sg.astype(bf16)))
        s_ret[h] = math.exp(LOG_GAMMA[h] * tl) * sp + upd_ret[u]
        s_hg[h] = decay[u] * sg + upd_hg[u]
        if BB > 1 or c == nch - 1:
            sret_ref[s0:s0 + cb, h] = s_ret[h]
            shg_ref[s0:s0 + cb, h] = s_hg[h]

    rg, hg = proj(3), proj(7)
    head_out = {}
    for u, (c, h) in enumerate(units):
        o = o_ret[u]
        mu = jnp.mean(o, axis=-1, keepdims=True)
        oc = o - mu
        var = jnp.mean(oc * oc, axis=-1, keepdims=True)
        head_out[(c, h)] = oc * lax.rsqrt(var + LN_EPS) * jax.nn.silu(sl(rg, c, h))
        og = o_hg[u]
        ms = jnp.mean(og * og, axis=-1, keepdims=True)
        head_out[(c, N_HEADS + h)] = og * lax.rsqrt(ms + LN_EPS) * nw * jax.nn.silu(sl(hg, c, h))
    chunk_out = [jnp.concatenate([head_out[(c, j)] for j in range(2 * N_HEADS)], axis=1).astype(bf16)
                 for c in range(nch)]
    mixed = chunk_out[0] if nch == 1 else jnp.concatenate(chunk_out, axis=0)
    m = _dot(mixed, wout_ref[...])
    y = _layernorm_rows(DEEPNORM_ALPHA * x2 + m, g_ref[0:1, :], be_ref[0:1, :])
    y_ref[...] = y.reshape(BB, TL, D_MODEL)


def _mixer0(x, c2, s2, w_in, w_out, lb_logits, norm_w, ln_g, ln_b, sret0, shg0, *, BB, TL):
    B, L, _ = x.shape
    R = BB * TL
    T = min(R, CHUNK_ROWS)
    if BB == 1:
        cb, tl = 1, T
    else:
        assert T % TL == 0
        cb, tl = T // TL, TL
    per_seq = sret0.shape[0] == B and B > 1
    st_block = (BB, N_HEADS, D_HEAD, D_HEAD)
    n_score_dots = N_HEADS * (R // T) * (2 + (tl - 1).bit_length()) if cb == 1 else 1
    kern = functools.partial(_mixer0_kernel, BB=BB, TL=TL, cb=cb, tl=tl)
    return pl.pallas_call(
        kern, grid=(B // BB, L // TL),
        in_specs=[
            pl.BlockSpec((BB, TL, D_MODEL), lambda b, t: (b, t, 0)),
            pl.BlockSpec((TL, D_HEAD), lambda b, t: (t, 0)),
            pl.BlockSpec((TL, D_HEAD), lambda b, t: (t, 0)),
            _const_spec(w_in.shape), _const_spec(w_out.shape), _const_spec(lb_logits.shape),
            _const_spec(norm_w.shape), _const_spec(ln_g.shape), _const_spec(ln_b.shape),
            _state_spec(st_block, per_seq), _state_spec(st_block, per_seq),
        ],
        out_specs=[
            pl.BlockSpec((BB, TL, D_MODEL), lambda b, t: (b, t, 0)),
            _state_spec(st_block, True), _state_spec(st_block, True),
        ],
        out_shape=[
            jax.ShapeDtypeStruct((B, L, D_MODEL), f32),
            jax.ShapeDtypeStruct((B, N_HEADS, D_HEAD, D_HEAD), f32),
            jax.ShapeDtypeStruct((B, N_HEADS, D_HEAD, D_HEAD), f32),
        ],
        scratch_shapes=[
            pltpu.VMEM((N_HEADS, T, T), f32),
            pltpu.VMEM((N_HEADS, T, D_HEAD), f32),
            pltpu.VMEM((N_HEADS, T, D_HEAD), f32),
            pltpu.VMEM((T, T), bf16),
            pltpu.VMEM((T, T), jnp.int32),
            pltpu.VMEM((n_score_dots, D_HEAD, T), bf16),
        ],
        compiler_params=_params(BIG_BLOCK_VMEM_LIMIT_BYTES), name=f"mixer0_b{BB}_t{TL}",
    )(x, c2, s2, w_in, w_out, lb_logits, norm_w, ln_g, ln_b, sret0, shg0)


def _ffn_kernel(x_ref, wup_ref, cw_ref, cb_ref, wdn_ref, g_ref, be_ref, buf0_ref, *rest, BB, TL, stacked, layer):
    R = BB * TL
    K = CONV_W_FFN - 1
    lr = slice(layer, layer + 1)
    if stacked:
        prev_ref, y_ref, out_ref, u_ref = rest
        buf_ref = out_ref.at[1]
    else:
        y_ref, buf_ref, u_ref = rest

    @pl.when(pl.program_id(1) == 0)
    def _init_state():
        buf_ref[...] = buf0_ref[...]
        if stacked:
            out_ref[0] = prev_ref[...]

    x2 = x_ref[...].reshape(R, D_MODEL)
    xb = x2.astype(bf16)
    u = _dot(xb, wup_ref[:, :D_FF])
    v = _dot(xb, wup_ref[:, D_FF:])
    u_ref[:, SUBLANES - K:SUBLANES, :] = buf_ref[...]
    u_ref[:, SUBLANES:, :] = u.reshape(BB, TL, D_FF)
    uc = cb_ref[lr, :] + u_ref[:, SUBLANES - K:SUBLANES - K + TL, :] * cw_ref[0, lr, :]
    for j in range(1, CONV_W_FFN):
        uc = uc + u_ref[:, SUBLANES - K + j:SUBLANES - K + j + TL, :] * cw_ref[j, lr, :]
    buf_ref[...] = u_ref[:, SUBLANES + TL - K:SUBLANES + TL, :]
    hmid = _gelu(uc.reshape(R, D_FF)) * v
    fo = _dot(hmid.astype(bf16), wdn_ref[...])
    y = _layernorm_rows(DEEPNORM_ALPHA * x2 + fo, g_ref[lr, :], be_ref[lr, :])
    y_ref[...] = y.reshape(BB, TL, D_MODEL)


def _ffn(x, w_up, conv_w, conv_b, w_down, ln_g, ln_b, buf0, *, layer, BB, TL, prev_buf=None):
    B, L, _ = x.shape
    per_seq = buf0.shape[-3] == B and B > 1
    K = CONV_W_FFN - 1
    stacked = prev_buf is not None
    kern = functools.partial(_ffn_kernel, BB=BB, TL=TL, stacked=stacked, layer=layer)
    buf_spec = (pl.BlockSpec((2, BB, K, D_FF), lambda b, t: (0, b, 0, 0)) if stacked
                else _state_spec((BB, K, D_FF), True))
    buf_shape = (2, B, K, D_FF) if stacked else (B, K, D_FF)
    buf0_spec = (_state_spec((BB, K, D_FF), per_seq) if buf0.ndim == 3
                 else pl.BlockSpec((None, BB, K, D_FF), lambda b, t: (layer, b, 0, 0)))
    layer_spec = lambda rows, cols: pl.BlockSpec((rows, cols), lambda b, t: (layer, 0), pipeline_mode=pl.Buffered(1))
    return pl.pallas_call(
        kern, grid=(B // BB, L // TL),
        in_specs=[
            pl.BlockSpec((BB, TL, D_MODEL), lambda b, t: (b, t, 0)),
            layer_spec(D_MODEL, 2 * D_FF), _const_spec(conv_w.shape), _const_spec(conv_b.shape),
            layer_spec(D_FF, D_MODEL), _const_spec(ln_g.shape), _const_spec(ln_b.shape),
            buf0_spec,
        ] + ([_state_spec((BB, K, D_FF), True)] if stacked else []),
        out_specs=[pl.BlockSpec((BB, TL, D_MODEL), lambda b, t: (b, t, 0)), buf_spec],
        out_shape=[
            jax.ShapeDtypeStruct((B, L, D_MODEL), f32),
            jax.ShapeDtypeStruct(buf_shape, f32),
        ],
        scratch_shapes=[pltpu.VMEM((BB, SUBLANES + TL, D_FF), f32)],
        compiler_params=_params(BIG_BLOCK_VMEM_LIMIT_BYTES), name=f"ffn_b{BB}_t{TL}",
    )(x, w_up, conv_w, conv_b, w_down, ln_g, ln_b, buf0, *([prev_buf] if stacked else []))


def _rglru_kernel(x_ref, win_ref, cw_ref, cb_ref, wga_ref, bga_ref, wgx_ref, bgx_ref, lam_ref,
                  wout_ref, g_ref, be_ref, h0_ref, buf0_ref,
                  y_ref, h_ref, buf_ref, *, BB, TL):
    R = BB * TL
    K = CONV_W_LRU - 1
    G = TL // SUBLANES

    @pl.when(pl.program_id(1) == 0)
    def _init_state():
        h_ref[...] = h0_ref[...]
        buf_ref[...] = buf0_ref[...]

    x2 = x_ref[...].reshape(R, D_MODEL)
    xb = x2.astype(bf16)
    W = LRU_STEP_COLS
    nchunk = D_RNN // W
    carry = buf_ref[...]
    new_carry = []
    sub = lax.broadcasted_iota(jnp.int32, (R // SUBLANES, SUBLANES, W), 1)
    h0 = h_ref[...]

    def project(ci):
        c0 = ci * W
        return _dot(xb, win_ref[:, c0:c0 + W]), _dot(xb, win_ref[:, D_RNN + c0:D_RNN + c0 + W])

    def recur(ci, gate, rnn):
        c0 = ci * W
        cs = slice(c0, c0 + W)
        rnn3 = rnn.reshape(BB, TL, W)
        xc = _causal_conv(rnn3, [carry[k][:, None, cs] for k in range(K)], cw_ref, cb_ref, cs).reshape(R, W)
        new_carry.append(jnp.stack([rnn3[:, TL - K + k, :] for k in range(K)], axis=0))
        xcb = xc.astype(bf16)
        blocks = [(j, c0 // LRU_BLOCK + j) for j in range(W // LRU_BLOCK)]
        ga = jnp.concatenate([_dot(xcb[:, j * LRU_BLOCK:(j + 1) * LRU_BLOCK], wga_ref[n]) for j, n in blocks], axis=1)
        gx = jnp.concatenate([_dot(xcb[:, j * LRU_BLOCK:(j + 1) * LRU_BLOCK], wgx_ref[n]) for j, n in blocks], axis=1)
        rgate = jax.nn.sigmoid(ga + bga_ref[:, cs])
        igate = jax.nn.sigmoid(gx + bgx_ref[:, cs])
        nla = rgate * (-LRU_C * jax.nn.log_sigmoid(lam_ref[:, cs]))
        a = jnp.exp(-nla)
        w1 = jnp.tanh(nla) * (1.0 + a * a)
        bv = jnp.where(w1 > 0.0, w1 * lax.rsqrt(w1), 0.0) * (igate * xc)
        a4 = a.reshape(R // SUBLANES, SUBLANES, W)
        b4 = bv.reshape(R // SUBLANES, SUBLANES, W)
        for d in (1, 2, 4):
            keep = sub >= d
            b4 = jnp.where(keep, a4 * pltpu.roll(b4, d, 1) + b4, b4)
            a4 = jnp.where(keep, a4 * pltpu.roll(a4, d, 1), a4)
        a5 = a4.reshape(BB, G, SUBLANES, W)
        b5 = b4.reshape(BB, G, SUBLANES, W)
        hc = h0[:, None, cs]
        tiles = []
        for gi in range(G):
            hg = a5[:, gi] * hc + b5[:, gi]
            hc = hg[:, SUBLANES - 1:SUBLANES, :]
            tiles.append(hg)
        hseq = tiles[0] if G == 1 else jnp.concatenate(tiles, axis=1)
        return (_gelu(gate) * hseq.reshape(R, W)).astype(bf16), hc

    nxt = project(0)
    ys, h_last = [], []
    for ci in range(nchunk):
        cur, nxt = nxt, (project(ci + 1) if ci + 1 < nchunk else None)
        yv, hc = recur(ci, *cur)
        ys.append(yv)
        h_last.append(hc)
    buf_ref[...] = jnp.concatenate(new_carry, axis=2)
    h_ref[...] = jnp.concatenate(h_last, axis=2).reshape(BB, D_RNN)
    yv = jnp.concatenate(ys, axis=1)
    groups = LRU_OUT_ROW_GROUPS if R % (LRU_OUT_ROW_GROUPS * 2 * SUBLANES) == 0 else 1
    step = R // groups
    outs = [_layernorm_rows(DEEPNORM_ALPHA * x2[i * step:(i + 1) * step]
                            + _dot(yv[i * step:(i + 1) * step], wout_ref[...]), g_ref[1:2, :], be_ref[1:2, :])
            for i in range(groups)]
    y = outs[0] if groups == 1 else jnp.concatenate(outs, axis=0)
    y_ref[...] = y.reshape(BB, TL, D_MODEL)


def _rglru(x, w_in, conv_w, conv_b, wga, bga, wgx, bgx, lam, w_out, ln_g, ln_b, h0, buf0, *, BB, TL):
    B, L, _ = x.shape
    per_seq = h0.shape[0] == B and B > 1
    K = CONV_W_LRU - 1
    kern = functools.partial(_rglru_kernel, BB=BB, TL=TL)
    consts = (w_in, conv_w, conv_b, wga, bga, wgx, bgx, lam, w_out, ln_g, ln_b)

    def buf_spec(per):
        return pl.BlockSpec((K, BB, D_RNN), lambda b, t: (0, b if per else 0, 0))

    return pl.pallas_call(
        kern, grid=(B // BB, L // TL),
        in_specs=[pl.BlockSpec((BB, TL, D_MODEL), lambda b, t: (b, t, 0))]
        + [_const_spec(c.shape) for c in consts]
        + [_state_spec((BB, D_RNN), per_seq), buf_spec(per_seq)],
        out_specs=[
            pl.BlockSpec((BB, TL, D_MODEL), lambda b, t: (b, t, 0)),
            _state_spec((BB, D_RNN), True), buf_spec(True),
        ],
        out_shape=[
            jax.ShapeDtypeStruct((B, L, D_MODEL), f32),
            jax.ShapeDtypeStruct((B, D_RNN), f32),
            jax.ShapeDtypeStruct((K, B, D_RNN), f32),
        ],
        compiler_params=_params(), name=f"rglru_b{BB}_t{TL}",
    )(x, *consts, h0, buf0)


def _rglru_tm_kernel(x_ref, win_ref, cw_ref, cb_ref, wga_ref, bga_ref, wgx_ref, bgx_ref, lam_ref,
                     wout_ref, g_ref, be_ref, h0_ref, buf0_ref,
                     y_ref, h_ref, buf_ref, perm_ref, permt_ref, hrun_ref, crun_ref, *, TL):
    B = SUBLANES
    NPART = LRU_TM_PARTS
    TP = TL // NPART
    RP = B * TP
    K = CONV_W_LRU - 1
    W = LRU_STEP_COLS
    nchunk = D_RNN // W
    step = pl.program_id(0)

    @pl.when(step == 0)
    def _init():
        r = lax.broadcasted_iota(jnp.int32, (RP, RP), 0)
        c = lax.broadcasted_iota(jnp.int32, (RP, RP), 1)
        hit = ((r & (B - 1)) * TP + (r >> LOG2_SUBLANES)) == c
        perm_ref[...] = jnp.where(hit, 1.0, 0.0).astype(bf16)
        hit_t = ((c & (B - 1)) * TP + (c >> LOG2_SUBLANES)) == r
        permt_ref[...] = jnp.where(hit_t, 1.0, 0.0).astype(bf16)
        hrun_ref[...] = jnp.broadcast_to(h0_ref[...], (B, D_RNN))
        for k in range(K):
            crun_ref[k] = jnp.broadcast_to(buf0_ref[k], (B, D_RNN))

    def head(p):
        x2 = x_ref[:, p * TP:(p + 1) * TP, :].reshape(RP, D_MODEL)
        return x2, _dot(perm_ref[...], x2.astype(bf16)).astype(bf16)

    def project(xt, ci):
        c0 = ci * W
        return _dot(xt, win_ref[:, c0:c0 + W]), _dot(xt, win_ref[:, D_RNN + c0:D_RNN + c0 + W])

    def body(ci, gate, rnn):
        c0 = ci * W
        cs = slice(c0, c0 + W)
        rnn = rnn.reshape(TP, B, W)
        ext = jnp.concatenate([crun_ref[:, :, cs], rnn], axis=0)
        xc = cb_ref[:, cs] + ext[0:TP] * cw_ref[0:1, cs]
        for j in range(1, CONV_W_LRU):
            xc = xc + ext[j:j + TP] * cw_ref[j:j + 1, cs]
        crun_ref[:, :, cs] = rnn[TP - K:]
        xc = xc.reshape(RP, W)
        xcb = xc.astype(bf16)
        blocks = [(j, c0 // LRU_BLOCK + j) for j in range(W // LRU_BLOCK)]
        ga = jnp.concatenate([_dot(xcb[:, j * LRU_BLOCK:(j + 1) * LRU_BLOCK], wga_ref[n]) for j, n in blocks], axis=1)
        gx = jnp.concatenate([_dot(xcb[:, j * LRU_BLOCK:(j + 1) * LRU_BLOCK], wgx_ref[n]) for j, n in blocks], axis=1)
        rgate = jax.nn.sigmoid(ga + bga_ref[:, cs])
        igate = jax.nn.sigmoid(gx + bgx_ref[:, cs])
        nla = rgate * (-LRU_C * jax.nn.log_sigmoid(lam_ref[:, cs]))
        a2 = jnp.exp(-nla)
        a = a2.reshape(TP, B, W)
        w1 = jnp.tanh(nla) * (1.0 + a2 * a2)
        bv = (jnp.where(w1 > 0.0, w1 * lax.rsqrt(w1), 0.0) * (igate * xc)).reshape(TP, B, W)
        h = hrun_ref[:, cs]
        hs = []
        for t in range(TP):
            h = a[t] * h + bv[t]
            hs.append(h)
        hrun_ref[:, cs] = h
        return (_gelu(gate) * jnp.stack(hs, axis=0).reshape(RP, W)).astype(bf16)

    def tail(p, x2, ys):
        yv = _dot(permt_ref[...], jnp.concatenate(ys, axis=1)).astype(bf16)
        y = _layernorm_rows(DEEPNORM_ALPHA * x2 + _dot(yv, wout_ref[...]), g_ref[1:2, :], be_ref[1:2, :])
        y_ref[:, p * TP:(p + 1) * TP, :] = y.reshape(B, TP, D_MODEL)

    x2s, xts, nxt = {}, {}, None
    x2s[0], xts[0] = head(0)
    nxt = project(xts[0], 0)
    for p in range(NPART):
        if p + 1 < NPART:
            x2s[p + 1], xts[p + 1] = head(p + 1)
        ys = []
        for ci in range(nchunk):
            cur = nxt
            if ci + 1 < nchunk:
                nxt = project(xts[p], ci + 1)
            elif p + 1 < NPART:
                nxt = project(xts[p + 1], 0)
            ys.append(body(ci, *cur))
        tail(p, x2s[p], ys)

    @pl.when(step == pl.num_programs(0) - 1)
    def _emit_state():
        h_ref[...] = hrun_ref[...]
        buf_ref[...] = crun_ref[...]


def _rglru_tm(x, w_in, conv_w, conv_b, wga, bga, wgx, bgx, lam, w_out, ln_g, ln_b, h0, buf0, *, TL):
    B, L, _ = x.shape
    assert B == SUBLANES and h0.shape[0] == 1 and buf0.shape[1] == 1
    K = CONV_W_LRU - 1
    R = B * TL // LRU_TM_PARTS
    consts = (w_in, conv_w, conv_b, wga, bga, wgx, bgx, lam, w_out, ln_g, ln_b, h0, buf0)
    const_spec = lambda shape: pl.BlockSpec(shape, lambda t: (0,) * len(shape), pipeline_mode=pl.Buffered(1))
    return pl.pallas_call(
        functools.partial(_rglru_tm_kernel, TL=TL), grid=(L // TL,),
        in_specs=[pl.BlockSpec((B, TL, D_MODEL), lambda t: (0, t, 0))] + [const_spec(c.shape) for c in consts],
        out_specs=[
            pl.BlockSpec((B, TL, D_MODEL), lambda t: (0, t, 0)),
            pl.BlockSpec((B, D_RNN), lambda t: (0, 0)), pl.BlockSpec((K, B, D_RNN), lambda t: (0, 0, 0)),
        ],
        out_shape=[
            jax.ShapeDtypeStruct((B, L, D_MODEL), f32),
            jax.ShapeDtypeStruct((B, D_RNN), f32),
            jax.ShapeDtypeStruct((K, B, D_RNN), f32),
        ],
        scratch_shapes=[
            pltpu.VMEM((R, R), bf16), pltpu.VMEM((R, R), bf16),
            pltpu.VMEM((B, D_RNN), f32),
            pltpu.VMEM((K, B, D_RNN), f32),
        ],
        compiler_params=pltpu.CompilerParams(dimension_semantics=("arbitrary",), vmem_limit_bytes=VMEM_LIMIT_BYTES),
        name=f"rglru_tm_t{TL}",
    )(x, *consts)


def _rope_tables(pos0, length):
    half = D_HEAD // 2
    pos = pos0 + jnp.arange(length, dtype=jnp.int32)
    inv = ROPE_BASE ** (-jnp.arange(half, dtype=f32) / half)
    ang = pos.astype(f32)[:, None] * inv[None, :]
    cos, sin = jnp.cos(ang), jnp.sin(ang)
    return jnp.concatenate([cos, cos], axis=1), jnp.concatenate([-sin, sin], axis=1)


def _trunk(x, pos0, states, p, *, BB, TL, BB0, TL0=None, lru_tm_steps=None, stack_ffn_bufs=False):
    s_ret, s_hgrn, h_lru, buf_lru, buf_ffn0, buf_ffn1 = states
    c2, s2 = _rope_tables(pos0, x.shape[1])
    x, s_ret, s_hgrn = _mixer0(x, c2, s2, p['w_in_ab'], p['w_out_ab'], p['hgrn_lb_logits'], p['hgrn_norm_w'],
                               p['ln_mix_g'], p['ln_mix_b'], s_ret, s_hgrn, BB=BB0, TL=TL0 or TL)
    x, buf_ffn0 = _ffn(x, p['w_ffn_up'], p['ffn_conv_w'], p['ffn_conv_b'], p['w_ffn_down'],
                       p['ln_ffn_g'], p['ln_ffn_b'], buf_ffn0, layer=0, BB=BB, TL=TL)
    lru_args = (x, p['w_in_c'], p['conv_w_c'], p['conv_b_c'], p['w_gate_a'], p['b_gate_a'], p['w_gate_x'],
                p['b_gate_x'], p['lru_lambda'], p['w_out_c'], p['ln_mix_g'], p['ln_mix_b'], h_lru, buf_lru)
    if lru_tm_steps is None:
        x, h_lru, buf_lru = _rglru(*lru_args, BB=BB, TL=TL)
    else:
        x, h_lru, buf_lru = _rglru_tm(*lru_args, TL=lru_tm_steps)
    x, buf_ffn1 = _ffn(x, p['w_ffn_up'], p['ffn_conv_w'], p['ffn_conv_b'], p['w_ffn_down'],
                       p['ln_ffn_g'], p['ln_ffn_b'], buf_ffn1, layer=1, BB=BB, TL=TL,
                       prev_buf=buf_ffn0 if stack_ffn_bufs else None)
    return x, (s_ret, s_hgrn, h_lru, buf_lru, buf_ffn0, buf_ffn1)


def kernel(x_prompt, x_sample, state_ret, state_hgrn, state_rglru_h, state_rglru_conv, state_ffn_conv, meta_tokens, w_in_ab, w_out_ab, hgrn_lb_logits, hgrn_norm_w, w_in_c, conv_w_c, conv_b_c, w_gate_a, b_gate_a, w_gate_x, b_gate_x, lru_lambda, w_out_c, ln_mix_g, ln_mix_b, ln_ffn_g, ln_ffn_b, w_ffn_up, ffn_conv_w, ffn_conv_b, w_ffn_down):
    row = lambda z: z.reshape(1, -1)
    wb = [w.astype(bf16) for w in (w_in_ab, w_out_ab, w_in_c, w_gate_a, w_gate_x, w_out_c,
                                   w_ffn_up.reshape(DEPTH * D_MODEL, 2 * D_FF),
                                   w_ffn_down.reshape(DEPTH * D_FF, D_MODEL))]
    p = dict(
        w_in_ab=wb[0], w_out_ab=wb[1],
        hgrn_lb_logits=hgrn_lb_logits, hgrn_norm_w=row(hgrn_norm_w),
        w_in_c=wb[2], conv_w_c=conv_w_c, conv_b_c=row(conv_b_c),
        w_gate_a=wb[3], b_gate_a=row(b_gate_a),
        w_gate_x=wb[4], b_gate_x=row(b_gate_x),
        lru_lambda=row(lru_lambda), w_out_c=wb[5],
        ln_mix_g=ln_mix_g, ln_mix_b=ln_mix_b, ln_ffn_g=ln_ffn_g, ln_ffn_b=ln_ffn_b,
        w_ffn_up=wb[6], w_ffn_down=wb[7],
        ffn_conv_w=jnp.transpose(ffn_conv_w, (1, 0, 2)), ffn_conv_b=ffn_conv_b,
    )
    dt = x_prompt.dtype
    zero_states = (
        jnp.zeros((1, N_HEADS, D_HEAD, D_HEAD), dt), jnp.zeros((1, N_HEADS, D_HEAD, D_HEAD), dt),
        jnp.zeros((1, D_RNN), dt), jnp.zeros((CONV_W_LRU - 1, 1, D_RNN), dt),
        jnp.zeros((1, CONV_W_FFN - 1, D_FF), dt), jnp.zeros((1, CONV_W_FFN - 1, D_FF), dt),
    )
    _, meta_states = _trunk(meta_tokens.astype(dt)[None], 0, zero_states, p, BB=1, TL=N_META, BB0=1)
    y_prompt, ps = _trunk(x_prompt, N_META, meta_states, p, BB=1, TL=PROMPT_BLOCK_STEPS, BB0=1,
                          TL0=PROMPT_MIXER0_BLOCK_STEPS, lru_tm_steps=LRU_TM_BLOCK_STEPS, stack_ffn_bufs=True)
    sample_states = (state_ret, state_hgrn, state_rglru_h, jnp.transpose(state_rglru_conv, (1, 0, 2)),
                     state_ffn_conv, state_ffn_conv)
    y_sample, ss = _trunk(x_sample, PAST_LEN, sample_states, p, BB=32, TL=x_sample.shape[1], BB0=16,
                          stack_ffn_bufs=True)
    return (y_prompt, y_sample, ps[0], ss[0], ps[1], ss[1], ps[2], ss[2],
            jnp.transpose(ps[3], (1, 0, 2)), jnp.transpose(ss[3], (1, 0, 2)), ps[5], ss[5])
```

```python
import functools
import math

import jax
import jax.numpy as jnp
from jax import lax
from jax.experimental import pallas as pl
from jax.experimental.pallas import tpu as pltpu

f32 = jnp.float32
bf16 = jnp.bfloat16

D_MODEL = 1024
N_META = 16
PAST_LEN = 16384
N_HEADS = 4
D_HEAD = 128
SEG = N_HEADS * D_HEAD
ROPE_BASE = 10000.0
D_RNN = 1024
N_LRU_BLOCKS = 8
LRU_BLOCK = D_RNN // N_LRU_BLOCKS
CONV_W_LRU = 4
LRU_C = 8.0
D_FF = 2816
CONV_W_FFN = 3
LN_EPS = 1e-5
DEPTH = 2
DEEPNORM_ALPHA = (2.0 * DEPTH) ** 0.25
LOG_GAMMA = tuple(math.log1p(-(2.0 ** (-5.0 - h))) for h in range(N_HEADS))

SUBLANES = 8
CHUNK_ROWS = 128
LRU_STEP_COLS = 256
LRU_OUT_ROW_GROUPS = 2
PROMPT_BLOCK_STEPS = 1024
PROMPT_MIXER0_BLOCK_STEPS = 1024
LRU_TM_BLOCK_STEPS = 128
LRU_TM_PARTS = 4
LOG2_SUBLANES = SUBLANES.bit_length() - 1
VMEM_LIMIT_BYTES = 56 * 1024 * 1024
BIG_BLOCK_VMEM_LIMIT_BYTES = 62 * 1024 * 1024

_GELU_K1 = -2.0 * math.log2(math.e) * math.sqrt(2.0 / math.pi)
_GELU_K3 = _GELU_K1 * 0.044715


def _gelu(x):
    return x / (1.0 + jnp.exp2(x * (_GELU_K1 + _GELU_K3 * (x * x))))


def _dot(a, b):
    return jnp.dot(a, b, preferred_element_type=f32)


def _dot_nt(a, b, t_ref=None):
    if t_ref is None:
        return lax.dot_general(a, b, (((1,), (1,)), ((), ())), preferred_element_type=f32)
    t_ref[...] = b.T
    return jnp.dot(a, t_ref[...], preferred_element_type=f32)


def _layernorm_rows(y, g, b):
    mu = jnp.mean(y, axis=-1, keepdims=True)
    yc = y - mu
    var = jnp.mean(yc * yc, axis=-1, keepdims=True)
    return yc * lax.rsqrt(var + LN_EPS) * g + b


def _causal_conv(x3, carry, w_ref, b_ref, cs):
    BB, TL, W = x3.shape
    K = len(carry)
    w = [w_ref[j:j + 1, cs] for j in range(K + 1)]

    def shift(p, j):
        bnd = w[0] * carry[K - j]
        for i in range(1, j):
            bnd = bnd + w[i] * carry[K - j + i]
        if TL == SUBLANES:
            t = lax.broadcasted_iota(jnp.int32, p.shape, 1)
            return jnp.where(t == 0, bnd, pltpu.roll(p, 1, 1))
        assert BB == 1
        rolled = pltpu.roll(p.reshape(TL, W), 1, 0)
        t = lax.broadcasted_iota(jnp.int32, (SUBLANES, W), 0)
        head = jnp.where(t == 0, bnd.reshape(1, W), rolled[:SUBLANES])
        return jnp.concatenate([head, rolled[SUBLANES:]], axis=0).reshape(BB, TL, W)

    p = w[0] * x3
    for j in range(1, K + 1):
        p = w[j] * x3 + shift(p, j)
    return b_ref[:, cs] + p


def _const_spec(shape):
    nd = len(shape)
    return pl.BlockSpec(shape, lambda b, t: (0,) * nd, pipeline_mode=pl.Buffered(1))


def _state_spec(block, per_seq):
    nd = len(block)
    if per_seq:
        return pl.BlockSpec(block, lambda b, t: (b,) + (0,) * (nd - 1))
    return pl.BlockSpec(block, lambda b, t: (0,) * nd)


def _params(vmem_limit_bytes=VMEM_LIMIT_BYTES):
    return pltpu.CompilerParams(dimension_semantics=("arbitrary", "arbitrary"),
                                vmem_limit_bytes=vmem_limit_bytes)


def _mixer0_init_tables(dmask_ref, wq_ref, wk_ref, tri_ref, lvl_ref, *, cb, tl):
    T = cb * tl
    tl_shift = tl.bit_length() - 1
    r = lax.broadcasted_iota(jnp.int32, (T, T), 0)
    c = lax.broadcasted_iota(jnp.int32, (T, T), 1)
    same = (r >> tl_shift) == (c >> tl_shift)
    rel = (r & (tl - 1)) - (c & (tl - 1))
    causal = same & (rel >= 0)
    relf = jnp.maximum(rel, 0).astype(f32)
    tri_ref[...] = jnp.where(causal, 1.0, 0.0).astype(bf16)
    lvl = jnp.where(r == c, 0, -1)
    s, li = 1, 1
    while s < tl:
        blk = (r >> li) == (c >> li)
        hit = blk & ((r & (2 * s - 1)) >= s) & ((c & (2 * s - 1)) < s)
        lvl = jnp.where(hit, li, lvl)
        s, li = 2 * s, li + 1
    lvl_ref[...] = lvl
    tr = (lax.broadcasted_iota(jnp.int32, (T, D_HEAD), 0) & (tl - 1)).astype(f32)
    for h in range(N_HEADS):
        lg = LOG_GAMMA[h]
        dmask_ref[h] = jnp.where(causal, jnp.exp(lg * relf), 0.0)
        wq_ref[h] = jnp.exp(lg * (tr + 1.0))
        wk_ref[h] = jnp.exp(lg * ((tl - 1.0) - tr))


def _hgrn_scores(gqs, kks, fs, bs, lvl_ref, dot_nt, *, T, tl):
    n = len(gqs)
    row = lax.broadcasted_iota(jnp.int32, (T, D_HEAD), 0)
    ntile = T // SUBLANES

    def owned(li, g):
        return lvl_ref[g * SUBLANES:(g + 1) * SUBLANES, :] == li

    full = [dot_nt(gqs[u].astype(bf16), kks[u].astype(bf16)) for u in range(n)]
    tiles = [[jnp.where(owned(0, g), full[u][g * SUBLANES:(g + 1) * SUBLANES], 0.0) for g in range(ntile)]
             for u in range(n)]

    def take(u, li, p, first_tile, n_tiles, p_row0):
        for g in range(n_tiles):
            t = first_tile + g
            tiles[u][t] = jnp.where(owned(li, t), p[p_row0 + g * SUBLANES:p_row0 + (g + 1) * SUBLANES], tiles[u][t])

    s, li = 1, 1
    while s < tl:
        nblk = T // (2 * s)
        if s < SUBLANES:
            up = (row & (2 * s - 1)) >= s
            zs = []
            for u in range(n):
                gq, kk, f, b = gqs[u], kks[u], fs[u], bs[u]
                if s == 1:
                    z = jnp.where(up, gq * f, kk)
                elif s == 2:
                    w = row & 3
                    e = jnp.where(w == 0, pltpu.roll(f, T - 1, 0),
                                  jnp.where(w == 1, 1.0, jnp.where(w == 2, f, f * pltpu.roll(f, 1, 0))))
                    z = jnp.where(up, gq, kk) * e
                else:
                    parts = [jnp.abs(b[m * 2 * s:(m + 1) * 2 * s, :] - b[m * 2 * s + s - 1:m * 2 * s + s, :])
                             for m in range(nblk)]
                    z = jnp.where(up, gq, kk) * jnp.exp2(-jnp.concatenate(parts, axis=0))
                zs.append(z.astype(bf16))
            ps = [dot_nt(zb, zb) for zb in zs]
            for u in range(n):
                take(u, li, ps[u], 0, ntile, 0)
        else:
            zs, qus = [], []
            for u in range(n):
                gq, kk, b = gqs[u], kks[u], bs[u]
                both, upper = [], []
                for m in range(nblk):
                    r0 = m * 2 * s
                    beta = b[r0 + s - 1:r0 + s, :]
                    k_lo = kk[r0:r0 + s] * jnp.exp2(beta - b[r0:r0 + s])
                    q_hi = gq[r0 + s:r0 + 2 * s] * jnp.exp2(b[r0 + s:r0 + 2 * s] - beta)
                    both += [k_lo, q_hi]
                    upper.append(q_hi)
                zs.append(jnp.concatenate(both, axis=0).astype(bf16))
                qus.append((upper[0] if nblk == 1 else jnp.concatenate(upper, axis=0)).astype(bf16))
            ps = [dot_nt(qus[u], zs[u]) for u in range(n)]
            for u in range(n):
                for m in range(nblk):
                    take(u, li, ps[u], (m * 2 * s + s) // SUBLANES, s // SUBLANES, m * s)
        s, li = 2 * s, li + 1
    return [jnp.concatenate(tiles[u], axis=0).astype(bf16) for u in range(n)]


def _mixer0_kernel(x_ref, c2_ref, s2_ref, win_ref, wout_ref, lbl_ref, nw_ref, g_ref, be_ref,
                   sret0_ref, shg0_ref, y_ref, sret_ref, shg_ref,
                   dmask_ref, wq_ref, wk_ref, tri_ref, lvl_ref, kt_ref, *rest, BB, TL, cb, tl, hoist_proj=False):
    T = cb * tl
    R = BB * TL
    nch = R // T
    slots = iter(range(kt_ref.shape[0]))

    def dot_nt(a, b):
        return _dot_nt(a, b, kt_ref.at[next(slots)] if (cb == 1 and T == CHUNK_ROWS) else None)

    @pl.when((pl.program_id(0) == 0) & (pl.program_id(1) == 0))
    def _init_tables():
        _mixer0_init_tables(dmask_ref, wq_ref, wk_ref, tri_ref, lvl_ref, cb=cb, tl=tl)

    @pl.when(pl.program_id(1) == 0)
    def _init_state():
        sret_ref[...] = sret0_ref[...]
        shg_ref[...] = shg0_ref[...]

    if hoist_proj:
        proj_ref, = rest

        @pl.when(pl.program_id(0) == 0)
        def _project_all():
            xa = x_ref[...].reshape(-1, D_MODEL).astype(bf16)
            for i in range(2 * N_HEADS):
                proj_ref[:, i * SEG:(i + 1) * SEG] = _dot(xa, win_ref[:, i * SEG:(i + 1) * SEG])

        r0 = pl.multiple_of(pl.program_id(0) * R, R)
        x2 = x_ref[pl.ds(pl.program_id(0) * BB, BB)].reshape(R, D_MODEL)

        def proj(i):
            return proj_ref[pl.ds(r0, R), i * SEG:(i + 1) * SEG]
    else:
        x2 = x_ref[...].reshape(R, D_MODEL)
        xb = x2.astype(bf16)

        def proj(i):
            return _dot(xb, win_ref[:, i * SEG:(i + 1) * SEG])

    lbl = lbl_ref[...]
    le = jnp.exp(lbl - jnp.max(lbl, axis=0, keepdims=True))
    lb = le[0:1] / jnp.sum(le, axis=0, keepdims=True)
    nw = nw_ref[...]

    def seqs(z):
        return z.reshape(cb, tl, D_HEAD)

    def rows(z3):
        return z3.reshape(T, D_HEAD)

    def qs(q3, s3):
        if cb == 1:
            return _dot(q3[0], s3[0])[None]
        return jnp.einsum('bqd,bdv->bqv', q3, s3, preferred_element_type=f32)

    def ktv(k3, v3):
        if cb == 1:
            return lax.dot_general(k3[0], v3[0], (((0,), (0,)), ((), ())),
                                   preferred_element_type=f32)[None]
        return jnp.einsum('btd,btv->bdv', k3, v3, preferred_element_type=f32)

    units = [(c, h) for c in range(nch) for h in range(N_HEADS)]
    nu = len(units)

    def sl(z, c, h):
        return z[c * T:(c + 1) * T, h * D_HEAD:(h + 1) * D_HEAD]

    def seq0(c):
        return 0 if BB == 1 else c * cb

    def rope_tables(c):
        ts = slice(c * tl, (c + 1) * tl) if BB == 1 else slice(0, tl)
        return c2_ref[ts, :][None], s2_ref[ts, :][None]

    def rope(z, cos2, sin2):
        return rows(seqs(z) * cos2 + seqs(pltpu.roll(z, D_HEAD // 2, 1)) * sin2)

    tabs = [rope_tables(c) for c in range(nch)]
    ktabs = [(c2 * (D_HEAD ** -0.5), s2 * (D_HEAD ** -0.5)) for c2, s2 in tabs]

    rq, rk = proj(0), proj(1)
    q = [rope(sl(rq, c, h), *tabs[c]) for c, h in units]
    k = [rope(sl(rk, c, h), *ktabs[c]) for c, h in units]
    rv = proj(2)
    v = [sl(rv, c, h).astype(bf16) for c, h in units]
    att = [dot_nt(q[u].astype(bf16), k[u].astype(bf16)) for u in range(nu)]
    att = [(att[u] * dmask_ref[h]).astype(bf16) for u, (c, h) in enumerate(units)]
    o_ret = [_dot(att[u], v[u]) for u in range(nu)]
    qw = [seqs((q[u] * wq_ref[h]).astype(bf16)) for u, (c, h) in enumerate(units)]
    upd_ret = [ktv(seqs((k[u] * wk_ref[h]).astype(bf16)), seqs(v[u])) for u, (c, h) in enumerate(units)]

    hf = proj(5)
    lbs = [lb[:, h * D_HEAD:(h + 1) * D_HEAD] for h in range(N_HEADS)]
    f = [lbs[h] + (1.0 - lbs[h]) * jax.nn.sigmoid(sl(hf, c, h)) for c, h in units]
    lf = [jnp.log2(z) for z in f]
    kk = [1.0 - z for z in f]
    hq, hi = proj(4), proj(6)
    gq = [sl(hq, c, h) for c, h in units]
    gv = [sl(hi, c, h).astype(bf16) for c, h in units]
    split = []
    for z in lf:
        l1 = z.astype(bf16)
        r1 = z - l1.astype(f32)
        l2 = r1.astype(bf16)
        l3 = (r1 - l2.astype(f32)).astype(bf16)
        split.append(jnp.concatenate([l1, l2, l3], axis=1))
    cs = [_dot(tri_ref[...], z) for z in split]
    b = [z[:, :D_HEAD] + z[:, D_HEAD:2 * D_HEAD] + z[:, 2 * D_HEAD:] for z in cs]
    scores = _hgrn_scores(gq, kk, f, b, lvl_ref, dot_nt, T=T, tl=tl)
    o_hg = [_dot(scores[u], gv[u]) for u in range(nu)]
    qe = [seqs((gq[u] * jnp.exp2(b[u])).astype(bf16)) for u in range(nu)]
    b3 = [seqs(z) for z in b]
    bl = [z[:, tl - 1:tl, :] for z in b3]
    upd_hg = [ktv((seqs(kk[u]) * jnp.exp2(bl[u] - b3[u])).astype(bf16), seqs(gv[u])) for u in range(nu)]
    decay = []
    for z in bl:
        ez = jnp.exp2(z)
        d = [jnp.broadcast_to(ez[j], (D_HEAD, D_HEAD)).T for j in range(cb)]
        decay.append(d[0][None] if cb == 1 else jnp.stack(d))

    s_ret, s_hg = {}, {}
    for u, (c, h) in enumerate(units):
        s0 = seq0(c)
        first = BB > 1 or c == 0
        sp = sret_ref[s0:s0 + cb, h] if first else s_ret[h]
        sg = shg_ref[s0:s0 + cb, h] if first else s_hg[h]
        o_ret[u] = o_ret[u] + rows(qs(qw[u], sp.astype(bf16)))
        o_hg[u] = o_hg[u] + rows(qs(qe[u], sg.astype(bf16)))
        s_ret[h] = math.exp(LOG_GAMMA[h] * tl) * sp + upd_ret[u]
        s_hg[h] = decay[u] * sg + upd_hg[u]
        if BB > 1 or c == nch - 1:
            sret_ref[s0:s0 + cb, h] = s_ret[h]
            shg_ref[s0:s0 + cb, h] = s_hg[h]

    rg, hg = proj(3), proj(7)
    head_out = {}
    for u, (c, h) in enumerate(units):
        o = o_ret[u]
        mu = jnp.mean(o, axis=-1, keepdims=True)
        oc = o - mu
        var = jnp.mean(oc * oc, axis=-1, keepdims=True)
        head_out[(c, h)] = oc * lax.rsqrt(var + LN_EPS) * jax.nn.silu(sl(rg, c, h))
        og = o_hg[u]
        ms = jnp.mean(og * og, axis=-1, keepdims=True)
        head_out[(c, N_HEADS + h)] = og * lax.rsqrt(ms + LN_EPS) * nw * jax.nn.silu(sl(hg, c, h))
    chunk_out = [jnp.concatenate([head_out[(c, j)] for j in range(2 * N_HEADS)], axis=1).astype(bf16)
                 for c in range(nch)]
    mixed = chunk_out[0] if nch == 1 else jnp.concatenate(chunk_out, axis=0)
    m = _dot(mixed, wout_ref[...])
    y = _layernorm_rows(DEEPNORM_ALPHA * x2 + m, g_ref[0:1, :], be_ref[0:1, :])
    y_ref[...] = y.reshape(BB, TL, D_MODEL)


def _mixer0(x, c2, s2, w_in, w_out, lb_logits, norm_w, ln_g, ln_b, sret0, shg0, *, BB, TL, hoist_proj=False):
    B, L, _ = x.shape
    assert not hoist_proj or L == TL
    R = BB * TL
    T = min(R, CHUNK_ROWS)
    if BB == 1:
        cb, tl = 1, T
    else:
        assert T % TL == 0
        cb, tl = T // TL, TL
    per_seq = sret0.shape[0] == B and B > 1
    st_block = (BB, N_HEADS, D_HEAD, D_HEAD)
    n_score_dots = N_HEADS * (R // T) * (2 + (tl - 1).bit_length()) if cb == 1 else 1
    kern = functools.partial(_mixer0_kernel, BB=BB, TL=TL, cb=cb, tl=tl, hoist_proj=hoist_proj)
    return pl.pallas_call(
        kern, grid=(B // BB, L // TL),
        in_specs=[
            _const_spec(x.shape) if hoist_proj else pl.BlockSpec((BB, TL, D_MODEL), lambda b, t: (b, t, 0)),
            pl.BlockSpec((TL, D_HEAD), lambda b, t: (t, 0)),
            pl.BlockSpec((TL, D_HEAD), lambda b, t: (t, 0)),
            _const_spec(w_in.shape), _const_spec(w_out.shape), _const_spec(lb_logits.shape),
            _const_spec(norm_w.shape), _const_spec(ln_g.shape), _const_spec(ln_b.shape),
            _state_spec(st_block, per_seq), _state_spec(st_block, per_seq),
        ],
        out_specs=[
            pl.BlockSpec((BB, TL, D_MODEL), lambda b, t: (b, t, 0)),
            _state_spec(st_block, True), _state_spec(st_block, True),
        ],
        out_shape=[
            jax.ShapeDtypeStruct((B, L, D_MODEL), f32),
            jax.ShapeDtypeStruct((B, N_HEADS, D_HEAD, D_HEAD), f32),
            jax.ShapeDtypeStruct((B, N_HEADS, D_HEAD, D_HEAD), f32),
        ],
        scratch_shapes=[
            pltpu.VMEM((N_HEADS, T, T), f32),
            pltpu.VMEM((N_HEADS, T, D_HEAD), f32),
            pltpu.VMEM((N_HEADS, T, D_HEAD), f32),
            pltpu.VMEM((T, T), bf16),
            pltpu.VMEM((T, T), jnp.int32),
            pltpu.VMEM((n_score_dots, D_HEAD, T), bf16),
        ] + ([pltpu.VMEM((B * L, 2 * N_HEADS * SEG), f32)] if hoist_proj else []),
        compiler_params=_params(BIG_BLOCK_VMEM_LIMIT_BYTES), name=f"mixer0_b{BB}_t{TL}",
    )(x, c2, s2, w_in, w_out, lb_logits, norm_w, ln_g, ln_b, sret0, shg0)


def _ffn_kernel(x_ref, wup_ref, cw_ref, cb_ref, wdn_ref, g_ref, be_ref, buf0_ref, *rest, BB, TL, stacked, layer):
    R = BB * TL
    K = CONV_W_FFN - 1
    lr = slice(layer, layer + 1)
    if stacked:
        prev_ref, y_ref, out_ref, u_ref = rest
        buf_ref = out_ref.at[1]
    else:
        y_ref, buf_ref, u_ref = rest

    @pl.when(pl.program_id(1) == 0)
    def _init_state():
        buf_ref[...] = buf0_ref[...]
        if stacked:
            out_ref[0] = prev_ref[...]

    x2 = x_ref[...].reshape(R, D_MODEL)
    xb = x2.astype(bf16)
    u = _dot(xb, wup_ref[:, :D_FF])
    v = _dot(xb, wup_ref[:, D_FF:])
    u_ref[:, SUBLANES - K:SUBLANES, :] = buf_ref[...]
    u_ref[:, SUBLANES:, :] = u.reshape(BB, TL, D_FF)
    uc = cb_ref[lr, :] + u_ref[:, SUBLANES - K:SUBLANES - K + TL, :] * cw_ref[0, lr, :]
    for j in range(1, CONV_W_FFN):
        uc = uc + u_ref[:, SUBLANES - K + j:SUBLANES - K + j + TL, :] * cw_ref[j, lr, :]
    buf_ref[...] = u_ref[:, SUBLANES + TL - K:SUBLANES + TL, :]
    hmid = _gelu(uc.reshape(R, D_FF)) * v
    fo = _dot(hmid.astype(bf16), wdn_ref[...])
    y = _layernorm_rows(DEEPNORM_ALPHA * x2 + fo, g_ref[lr, :], be_ref[lr, :])
    y_ref[...] = y.reshape(BB, TL, D_MODEL)


def _ffn(x, w_up, conv_w, conv_b, w_down, ln_g, ln_b, buf0, *, layer, BB, TL, prev_buf=None):
    B, L, _ = x.shape
    per_seq = buf0.shape[-3] == B and B > 1
    K = CONV_W_FFN - 1
    stacked = prev_buf is not None
    kern = functools.partial(_ffn_kernel, BB=BB, TL=TL, stacked=stacked, layer=layer)
    buf_spec = (pl.BlockSpec((2, BB, K, D_FF), lambda b, t: (0, b, 0, 0)) if stacked
                else _state_spec((BB, K, D_FF), True))
    buf_shape = (2, B, K, D_FF) if stacked else (B, K, D_FF)
    buf0_spec = (_state_spec((BB, K, D_FF), per_seq) if buf0.ndim == 3
                 else pl.BlockSpec((None, BB, K, D_FF), lambda b, t: (layer, b, 0, 0)))
    layer_spec = lambda rows, cols: pl.BlockSpec((rows, cols), lambda b, t: (layer, 0), pipeline_mode=pl.Buffered(1))
    return pl.pallas_call(
        kern, grid=(B // BB, L // TL),
        in_specs=[
            pl.BlockSpec((BB, TL, D_MODEL), lambda b, t: (b, t, 0)),
            layer_spec(D_MODEL, 2 * D_FF), _const_spec(conv_w.shape), _const_spec(conv_b.shape),
            layer_spec(D_FF, D_MODEL), _const_spec(ln_g.shape), _const_spec(ln_b.shape),
            buf0_spec,
        ] + ([_state_spec((BB, K, D_FF), True)] if stacked else []),
        out_specs=[pl.BlockSpec((BB, TL, D_MODEL), lambda b, t: (b, t, 0)), buf_spec],
        out_shape=[
            jax.ShapeDtypeStruct((B, L, D_MODEL), f32),
            jax.ShapeDtypeStruct(buf_shape, f32),
        ],
        scratch_shapes=[pltpu.VMEM((BB, SUBLANES + TL, D_FF), f32)],
        compiler_params=_params(BIG_BLOCK_VMEM_LIMIT_BYTES), name=f"ffn_b{BB}_t{TL}",
    )(x, w_up, conv_w, conv_b, w_down, ln_g, ln_b, buf0, *([prev_buf] if stacked else []))


def _rglru_kernel(x_ref, win_ref, cw_ref, cb_ref, wga_ref, bga_ref, wgx_ref, bgx_ref, lam_ref,
                  wout_ref, g_ref, be_ref, h0_ref, buf0_ref,
                  y_ref, h_ref, buf_ref, *, BB, TL):
    R = BB * TL
    K = CONV_W_LRU - 1
    G = TL // SUBLANES

    @pl.when(pl.program_id(1) == 0)
    def _init_state():
        h_ref[...] = h0_ref[...]
        buf_ref[...] = buf0_ref[...]

    x2 = x_ref[...].reshape(R, D_MODEL)
    xb = x2.astype(bf16)
    W = LRU_STEP_COLS
    nchunk = D_RNN // W
    carry = buf_ref[...]
    new_carry = []
    sub = lax.broadcasted_iota(jnp.int32, (R // SUBLANES, SUBLANES, W), 1)
    h0 = h_ref[...]

    def project(ci):
        c0 = ci * W
        return _dot(xb, win_ref[:, c0:c0 + W]), _dot(xb, win_ref[:, D_RNN + c0:D_RNN + c0 + W])

    def recur(ci, gate, rnn):
        c0 = ci * W
        cs = slice(c0, c0 + W)
        rnn3 = rnn.reshape(BB, TL, W)
        xc = _causal_conv(rnn3, [carry[k][:, None, cs] for k in range(K)], cw_ref, cb_ref, cs).reshape(R, W)
        new_carry.append(jnp.stack([rnn3[:, TL - K + k, :] for k in range(K)], axis=0))
        xcb = xc.astype(bf16)
        blocks = [(j, c0 // LRU_BLOCK + j) for j in range(W // LRU_BLOCK)]
        ga = jnp.concatenate([_dot(xcb[:, j * LRU_BLOCK:(j + 1) * LRU_BLOCK], wga_ref[n]) for j, n in blocks], axis=1)
        gx = jnp.concatenate([_dot(xcb[:, j * LRU_BLOCK:(j + 1) * LRU_BLOCK], wgx_ref[n]) for j, n in blocks], axis=1)
        rgate = jax.nn.sigmoid(ga + bga_ref[:, cs])
        igate = jax.nn.sigmoid(gx + bgx_ref[:, cs])
        nla = rgate * (-LRU_C * jax.nn.log_sigmoid(lam_ref[:, cs]))
        a = jnp.exp(-nla)
        w1 = jnp.tanh(nla) * (1.0 + a * a)
        bv = jnp.where(w1 > 0.0, w1 * lax.rsqrt(w1), 0.0) * (igate * xc)
        a4 = a.reshape(R // SUBLANES, SUBLANES, W)
        b4 = bv.reshape(R // SUBLANES, SUBLANES, W)
        for d in (1, 2, 4):
            keep = sub >= d
            b4 = jnp.where(keep, a4 * pltpu.roll(b4, d, 1) + b4, b4)
            a4 = jnp.where(keep, a4 * pltpu.roll(a4, d, 1), a4)
        a5 = a4.reshape(BB, G, SUBLANES, W)
        b5 = b4.reshape(BB, G, SUBLANES, W)
        hc = h0[:, None, cs]
        tiles = []
        for gi in range(G):
            hg = a5[:, gi] * hc + b5[:, gi]
            hc = hg[:, SUBLANES - 1:SUBLANES, :]
            tiles.append(hg)
        hseq = tiles[0] if G == 1 else jnp.concatenate(tiles, axis=1)
        return (_gelu(gate) * hseq.reshape(R, W)).astype(bf16), hc

    nxt = project(0)
    ys, h_last = [], []
    for ci in range(nchunk):
        cur, nxt = nxt, (project(ci + 1) if ci + 1 < nchunk else None)
        yv, hc = recur(ci, *cur)
        ys.append(yv)
        h_last.append(hc)
    buf_ref[...] = jnp.concatenate(new_carry, axis=2)
    h_ref[...] = jnp.concatenate(h_last, axis=2).reshape(BB, D_RNN)
    yv = jnp.concatenate(ys, axis=1)
    groups = LRU_OUT_ROW_GROUPS if R % (LRU_OUT_ROW_GROUPS * 2 * SUBLANES) == 0 else 1
    step = R // groups
    outs = [_layernorm_rows(DEEPNORM_ALPHA * x2[i * step:(i + 1) * step]
                            + _dot(yv[i * step:(i + 1) * step], wout_ref[...]), g_ref[1:2, :], be_ref[1:2, :])
            for i in range(groups)]
    y = outs[0] if groups == 1 else jnp.concatenate(outs, axis=0)
    y_ref[...] = y.reshape(BB, TL, D_MODEL)


def _rglru(x, w_in, conv_w, conv_b, wga, bga, wgx, bgx, lam, w_out, ln_g, ln_b, h0, buf0, *, BB, TL):
    B, L, _ = x.shape
    per_seq = h0.shape[0] == B and B > 1
    K = CONV_W_LRU - 1
    kern = functools.partial(_rglru_kernel, BB=BB, TL=TL)
    consts = (w_in, conv_w, conv_b, wga, bga, wgx, bgx, lam, w_out, ln_g, ln_b)

    def buf_spec(per):
        return pl.BlockSpec((K, BB, D_RNN), lambda b, t: (0, b if per else 0, 0))

    return pl.pallas_call(
        kern, grid=(B // BB, L // TL),
        in_specs=[pl.BlockSpec((BB, TL, D_MODEL), lambda b, t: (b, t, 0))]
        + [_const_spec(c.shape) for c in consts]
        + [_state_spec((BB, D_RNN), per_seq), buf_spec(per_seq)],
        out_specs=[
            pl.BlockSpec((BB, TL, D_MODEL), lambda b, t: (b, t, 0)),
            _state_spec((BB, D_RNN), True), buf_spec(True),
        ],
        out_shape=[
            jax.ShapeDtypeStruct((B, L, D_MODEL), f32),
            jax.ShapeDtypeStruct((B, D_RNN), f32),
            jax.ShapeDtypeStruct((K, B, D_RNN), f32),
        ],
        compiler_params=_params(), name=f"rglru_b{BB}_t{TL}",
    )(x, *consts, h0, buf0)


def _rglru_tm_kernel(x_ref, win_ref, cw_ref, cb_ref, wga_ref, bga_ref, wgx_ref, bgx_ref, lam_ref,
                     wout_ref, g_ref, be_ref, h0_ref, buf0_ref,
                     y_ref, h_ref, buf_ref, perm_ref, permt_ref, hrun_ref, crun_ref, *, TL):
    B = SUBLANES
    NPART = LRU_TM_PARTS
    TP = TL // NPART
    RP = B * TP
    K = CONV_W_LRU - 1
    W = LRU_STEP_COLS
    nchunk = D_RNN // W
    step = pl.program_id(0)

    @pl.when(step == 0)
    def _init():
        r = lax.broadcasted_iota(jnp.int32, (RP, RP), 0)
        c = lax.broadcasted_iota(jnp.int32, (RP, RP), 1)
        hit = ((r & (B - 1)) * TP + (r >> LOG2_SUBLANES)) == c
        perm_ref[...] = jnp.where(hit, 1.0, 0.0).astype(bf16)
        hit_t = ((c & (B - 1)) * TP + (c >> LOG2_SUBLANES)) == r
        permt_ref[...] = jnp.where(hit_t, 1.0, 0.0).astype(bf16)
        hrun_ref[...] = jnp.broadcast_to(h0_ref[...], (B, D_RNN))
        for k in range(K):
            crun_ref[k] = jnp.broadcast_to(buf0_ref[k], (B, D_RNN))

    def head(p):
        x2 = x_ref[:, p * TP:(p + 1) * TP, :].reshape(RP, D_MODEL)
        return x2, _dot(perm_ref[...], x2.astype(bf16)).astype(bf16)

    def project(xt, ci):
        c0 = ci * W
        return _dot(xt, win_ref[:, c0:c0 + W]), _dot(xt, win_ref[:, D_RNN + c0:D_RNN + c0 + W])

    def body(ci, gate, rnn):
        c0 = ci * W
        cs = slice(c0, c0 + W)
        rnn = rnn.reshape(TP, B, W)
        ext = jnp.concatenate([crun_ref[:, :, cs], rnn], axis=0)
        xc = cb_ref[:, cs] + ext[0:TP] * cw_ref[0:1, cs]
        for j in range(1, CONV_W_LRU):
            xc = xc + ext[j:j + TP] * cw_ref[j:j + 1, cs]
        crun_ref[:, :, cs] = rnn[TP - K:]
        xc = xc.reshape(RP, W)
        xcb = xc.astype(bf16)
        blocks = [(j, c0 // LRU_BLOCK + j) for j in range(W // LRU_BLOCK)]
        ga = jnp.concatenate([_dot(xcb[:, j * LRU_BLOCK:(j + 1) * LRU_BLOCK], wga_ref[n]) for j, n in blocks], axis=1)
        gx = jnp.concatenate([_dot(xcb[:, j * LRU_BLOCK:(j + 1) * LRU_BLOCK], wgx_ref[n]) for j, n in blocks], axis=1)
        rgate = jax.nn.sigmoid(ga + bga_ref[:, cs])
        igate = jax.nn.sigmoid(gx + bgx_ref[:, cs])
        nla = rgate * (-LRU_C * jax.nn.log_sigmoid(lam_ref[:, cs]))
        a2 = jnp.exp(-nla)
        a = a2.reshape(TP, B, W)
        w1 = jnp.tanh(nla) * (1.0 + a2 * a2)
        bv = (jnp.where(w1 > 0.0, w1 * lax.rsqrt(w1), 0.0) * (igate * xc)).reshape(TP, B, W)
        h = hrun_ref[:, cs]
        hs = []
        for t in range(TP):
            h = a[t] * h + bv[t]
            hs.append(h)
        hrun_ref[:, cs] = h
        return (_gelu(gate) * jnp.stack(hs, axis=0).reshape(RP, W)).astype(bf16)

    def tail(p, x2, ys):
        yv = _dot(permt_ref[...], jnp.concatenate(ys, axis=1)).astype(bf16)
        y = _layernorm_rows(DEEPNORM_ALPHA * x2 + _dot(yv, wout_ref[...]), g_ref[1:2, :], be_ref[1:2, :])
        y_ref[:, p * TP:(p + 1) * TP, :] = y.reshape(B, TP, D_MODEL)

    x2s, xts, nxt = {}, {}, None
    x2s[0], xts[0] = head(0)
    nxt = project(xts[0], 0)
    for p in range(NPART):
        if p + 1 < NPART:
            x2s[p + 1], xts[p + 1] = head(p + 1)
        ys = []
        for ci in range(nchunk):
            cur = nxt
            if ci + 1 < nchunk:
                nxt = project(xts[p], ci + 1)
            elif p + 1 < NPART:
                nxt = project(xts[p + 1], 0)
            ys.append(body(ci, *cur))
        tail(p, x2s[p], ys)

    @pl.when(step == pl.num_programs(0) - 1)
    def _emit_state():
        h_ref[...] = hrun_ref[...]
        buf_ref[...] = crun_ref[...]


def _rglru_tm(x, w_in, conv_w, conv_b, wga, bga, wgx, bgx, lam, w_out, ln_g, ln_b, h0, buf0, *, TL):
    B, L, _ = x.shape
    assert B == SUBLANES and h0.shape[0] == 1 and buf0.shape[1] == 1
    K = CONV_W_LRU - 1
    R = B * TL // LRU_TM_PARTS
    consts = (w_in, conv_w, conv_b, wga, bga, wgx, bgx, lam, w_out, ln_g, ln_b, h0, buf0)
    const_spec = lambda shape: pl.BlockSpec(shape, lambda t: (0,) * len(shape), pipeline_mode=pl.Buffered(1))
    return pl.pallas_call(
        functools.partial(_rglru_tm_kernel, TL=TL), grid=(L // TL,),
        in_specs=[pl.BlockSpec((B, TL, D_MODEL), lambda t: (0, t, 0))] + [const_spec(c.shape) for c in consts],
        out_specs=[
            pl.BlockSpec((B, TL, D_MODEL), lambda t: (0, t, 0)),
            pl.BlockSpec((B, D_RNN), lambda t: (0, 0)), pl.BlockSpec((K, B, D_RNN), lambda t: (0, 0, 0)),
        ],
        out_shape=[
            jax.ShapeDtypeStruct((B, L, D_MODEL), f32),
            jax.ShapeDtypeStruct((B, D_RNN), f32),
            jax.ShapeDtypeStruct((K, B, D_RNN), f32),
        ],
        scratch_shapes=[
            pltpu.VMEM((R, R), bf16), pltpu.VMEM((R, R), bf16),
            pltpu.VMEM((B, D_RNN), f32),
            pltpu.VMEM((K, B, D_RNN), f32),
        ],
        compiler_params=pltpu.CompilerParams(dimension_semantics=("arbitrary",), vmem_limit_bytes=VMEM_LIMIT_BYTES),
        name=f"rglru_tm_t{TL}",
    )(x, *consts)


def _rope_tables(pos0, length):
    half = D_HEAD // 2
    pos = pos0 + jnp.arange(length, dtype=jnp.int32)
    inv = ROPE_BASE ** (-jnp.arange(half, dtype=f32) / half)
    ang = pos.astype(f32)[:, None] * inv[None, :]
    cos, sin = jnp.cos(ang), jnp.sin(ang)
    return jnp.concatenate([cos, cos], axis=1), jnp.concatenate([-sin, sin], axis=1)


def _trunk(x, pos0, states, p, *, BB, TL, BB0, TL0=None, lru_tm_steps=None, stack_ffn_bufs=False,
           hoist_proj0=False):
    s_ret, s_hgrn, h_lru, buf_lru, buf_ffn0, buf_ffn1 = states
    c2, s2 = _rope_tables(pos0, x.shape[1])
    x, s_ret, s_hgrn = _mixer0(x, c2, s2, p['w_in_ab'], p['w_out_ab'], p['hgrn_lb_logits'], p['hgrn_norm_w'],
                               p['ln_mix_g'], p['ln_mix_b'], s_ret, s_hgrn, BB=BB0, TL=TL0 or TL,
                               hoist_proj=hoist_proj0)
    x, buf_ffn0 = _ffn(x, p['w_ffn_up'], p['ffn_conv_w'], p['ffn_conv_b'], p['w_ffn_down'],
                       p['ln_ffn_g'], p['ln_ffn_b'], buf_ffn0, layer=0, BB=BB, TL=TL)
    lru_args = (x, p['w_in_c'], p['conv_w_c'], p['conv_b_c'], p['w_gate_a'], p['b_gate_a'], p['w_gate_x'],
                p['b_gate_x'], p['lru_lambda'], p['w_out_c'], p['ln_mix_g'], p['ln_mix_b'], h_lru, buf_lru)
    if lru_tm_steps is None:
        x, h_lru, buf_lru = _rglru(*lru_args, BB=BB, TL=TL)
    else:
        x, h_lru, buf_lru = _rglru_tm(*lru_args, TL=lru_tm_steps)
    x, buf_ffn1 = _ffn(x, p['w_ffn_up'], p['ffn_conv_w'], p['ffn_conv_b'], p['w_ffn_down'],
                       p['ln_ffn_g'], p['ln_ffn_b'], buf_ffn1, layer=1, BB=BB, TL=TL,
                       prev_buf=buf_ffn0 if stack_ffn_bufs else None)
    return x, (s_ret, s_hgrn, h_lru, buf_lru, buf_ffn0, buf_ffn1)


def kernel(x_prompt, x_sample, state_ret, state_hgrn, state_rglru_h, state_rglru_conv, state_ffn_conv, meta_tokens, w_in_ab, w_out_ab, hgrn_lb_logits, hgrn_norm_w, w_in_c, conv_w_c, conv_b_c, w_gate_a, b_gate_a, w_gate_x, b_gate_x, lru_lambda, w_out_c, ln_mix_g, ln_mix_b, ln_ffn_g, ln_ffn_b, w_ffn_up, ffn_conv_w, ffn_conv_b, w_ffn_down):
    row = lambda z: z.reshape(1, -1)
    wb = [w.astype(bf16) for w in (w_in_ab, w_out_ab, w_in_c, w_gate_a, w_gate_x, w_out_c,
                                   w_ffn_up.reshape(DEPTH * D_MODEL, 2 * D_FF),
                                   w_ffn_down.reshape(DEPTH * D_FF, D_MODEL))]
    p = dict(
        w_in_ab=wb[0], w_out_ab=wb[1],
        hgrn_lb_logits=hgrn_lb_logits, hgrn_norm_w=row(hgrn_norm_w),
        w_in_c=wb[2], conv_w_c=conv_w_c, conv_b_c=row(conv_b_c),
        w_gate_a=wb[3], b_gate_a=row(b_gate_a),
        w_gate_x=wb[4], b_gate_x=row(b_gate_x),
        lru_lambda=row(lru_lambda), w_out_c=wb[5],
        ln_mix_g=ln_mix_g, ln_mix_b=ln_mix_b, ln_ffn_g=ln_ffn_g, ln_ffn_b=ln_ffn_b,
        w_ffn_up=wb[6], w_ffn_down=wb[7],
        ffn_conv_w=jnp.transpose(ffn_conv_w, (1, 0, 2)), ffn_conv_b=ffn_conv_b,
    )
    dt = x_prompt.dtype
    zero_states = (
        jnp.zeros((1, N_HEADS, D_HEAD, D_HEAD), dt), jnp.zeros((1, N_HEADS, D_HEAD, D_HEAD), dt),
        jnp.zeros((1, D_RNN), dt), jnp.zeros((CONV_W_LRU - 1, 1, D_RNN), dt),
        jnp.zeros((1, CONV_W_FFN - 1, D_FF), dt), jnp.zeros((1, CONV_W_FFN - 1, D_FF), dt),
    )
    _, meta_states = _trunk(meta_tokens.astype(dt)[None], 0, zero_states, p, BB=1, TL=N_META, BB0=1)
    y_prompt, ps = _trunk(x_prompt, N_META, meta_states, p, BB=1, TL=PROMPT_BLOCK_STEPS, BB0=1,
                          TL0=PROMPT_MIXER0_BLOCK_STEPS, lru_tm_steps=LRU_TM_BLOCK_STEPS, stack_ffn_bufs=True)
    sample_states = (state_ret, state_hgrn, state_rglru_h, jnp.transpose(state_rglru_conv, (1, 0, 2)),
                     state_ffn_conv, state_ffn_conv)
    y_sample, ss = _trunk(x_sample, PAST_LEN, sample_states, p, BB=32, TL=x_sample.shape[1], BB0=8,
                          stack_ffn_bufs=True, hoist_proj0=True)
    return (y_prompt, y_sample, ps[0], ss[0], ps[1], ss[1], ps[2], ss[2],
            jnp.transpose(ps[3], (1, 0, 2)), jnp.transpose(ss[3], (1, 0, 2)), ps[5], ss[5])
```
